```python
import jax, jax.numpy as jnp
from jax import lax
import numpy as np

D_MODEL = 1024
BATCH = 16
SEQ = 256
DEPTH = 2
DEC_BATCH = 8
DEC_SEQ = 1024
PAST_LEN = 256

GRID_W = 64
HEAD_DIM = 64
SCALE = HEAD_DIM ** -0.5
Q_BLOCK = 128
A_HEADS = 8
A_KV_HEADS = 2
A_WINDOW = 128
A_BLOCK = 128
A_Q = A_HEADS * HEAD_DIM
A_KV = A_KV_HEADS * HEAD_DIM
A_COLS = A_Q + 2 * A_KV
B_HEADS = 8
B_W = B_HEADS * HEAD_DIM
B_COLS = 3 * B_W
NA_ROWS = 8
NA_COLS = 16
NA_COL_BLOCK = 16
NA_COL_SPAN = NA_COL_BLOCK + NA_COLS
C_HEADS = 8
C_WIDTH = C_HEADS * HEAD_DIM
DECAY_RANK = 64
ICLR_RANK = 64
GATE_RANK = 128
C_COLS = 3 * C_WIDTH + 2 * DECAY_RANK + 2 * ICLR_RANK + GATE_RANK
SHIFT_WIDTH = 3
GN_EPS = 64e-5
GATE_COLS = 3 * D_MODEL
IN_COLS = A_COLS + B_COLS + C_COLS + GATE_COLS
D_FF = 2816
N_MOD = 9
ROPE_BASE = 10000.0
LN_EPS = 1e-5
NEG_INF = -1e30
ALPHA = (2 * DEPTH) ** 0.25
BETA = (8 * DEPTH) ** -0.25

kernel_name = 'hybrid_dit_window_natten_rwkv7_step'


def split_points(sizes):
    return np.cumsum(sizes)[:-1].tolist()


def split_heads(t, n):
    return t.reshape(t.shape[:-1] + (n, HEAD_DIM))


def layer_norm(x, g, b):
    x32 = x.astype(jnp.float32)
    mu = x32.mean(-1, keepdims=True)
    var = jnp.mean(jnp.square(x32 - mu), -1, keepdims=True)
    return ((x32 - mu) * lax.rsqrt(var + LN_EPS) * g + b).astype(x.dtype)


def adaln_mods(c, w, b):
    return (jax.nn.silu(c) @ w + b).reshape(c.shape[0], N_MOD, D_MODEL)


def swiglu(h, w_in, w_out):
    a, g = jnp.split(h @ w_in, 2, axis=-1)
    return (jax.nn.silu(a) * g) @ w_out


def short_conv(z, w):
    L = z.shape[1]
    pad = SHIFT_WIDTH // 2
    zp = jnp.pad(z, ((0, 0), (pad, pad), (0, 0)))
    return sum(zp[:, i:i + L] * w[i] for i in range(SHIFT_WIDTH))


def rope_2d(x):
    L = x.shape[1]
    t = jnp.arange(L)
    n_freq = HEAD_DIM // 4
    inv = ROPE_BASE ** (-jnp.arange(n_freq, dtype=jnp.float32) / n_freq)
    rows = (t // GRID_W).astype(jnp.float32)
    cols = (t % GRID_W).astype(jnp.float32)
    ang = jnp.concatenate([rows[:, None] * inv, cols[:, None] * inv], axis=-1)[:, None, :]
    cos, sin = jnp.cos(ang).astype(x.dtype), jnp.sin(ang).astype(x.dtype)
    x1, x2 = x[..., :HEAD_DIM // 2], x[..., HEAD_DIM // 2:]
    return jnp.concatenate([x1 * cos - x2 * sin, x1 * sin + x2 * cos], axis=-1)


def project_inputs(h, w_in, w_shift):
    z = h @ w_in
    za, zb, zc, zg = jnp.split(z, split_points((A_COLS, B_COLS, C_COLS, GATE_COLS)), axis=-1)
    qa, ka, va = jnp.split(za, split_points((A_Q, A_KV, A_KV)), axis=-1)
    qb, kb, vb = jnp.split(zb, 3, axis=-1)
    heads_a = (split_heads(qa, A_HEADS), split_heads(ka, A_KV_HEADS), split_heads(va, A_KV_HEADS))
    heads_b = (split_heads(qb, B_HEADS), split_heads(kb, B_HEADS), split_heads(vb, B_HEADS))
    return heads_a, heads_b, short_conv(zc, w_shift), zg


def context_attention(q, k, v, sink):
    B, Lc, H, _ = q.shape
    KVh = k.shape[2]
    G = H // KVh
    nq = Lc // Q_BLOCK
    qb = jnp.moveaxis(q.reshape(B, nq, Q_BLOCK, KVh, G, HEAD_DIM), 1, 0)

    def attend(qblk):
        s = jnp.einsum('bqkgd,bckd->bkgqc', qblk, k).astype(jnp.float32) * SCALE
        if sink is not None:
            col = jnp.broadcast_to(sink.astype(jnp.float32).reshape(KVh, G)[None, :, :, None, None], s.shape[:-1] + (1,))
            s = jnp.concatenate([s, col], axis=-1)
        p = jax.nn.softmax(s, axis=-1)[..., :Lc]
        return jnp.einsum('bkgqc,bckd->bqkgd', p.astype(v.dtype), v)

    o = lax.map(attend, qb)
    return jnp.moveaxis(o, 0, 1).reshape(B, Lc, H * HEAD_DIM)


def window_attention(q, k, v, ck, cv, sink):
    B, L = q.shape[:2]
    nb = L // A_BLOCK
    G = A_HEADS // A_KV_HEADS
    qb = q.reshape(B, nb, A_BLOCK, A_KV_HEADS, G, HEAD_DIM)
    pad = ((0, 0), (A_BLOCK, A_BLOCK), (0, 0), (0, 0))

    def band(t):
        tp = jnp.pad(t, pad).reshape(B, nb + 2, A_BLOCK, A_KV_HEADS, HEAD_DIM)
        return jnp.concatenate([tp[:, :-2], tp[:, 1:-1], tp[:, 2:]], axis=2)

    kb, vb = band(k), band(v)
    s_loc = jnp.einsum('bnqkgd,bnjkd->bnkgqj', qb, kb).astype(jnp.float32) * SCALE
    s_ctx = jnp.einsum('bnqkgd,bckd->bnkgqc', qb, ck).astype(jnp.float32) * SCALE
    qi = jnp.arange(A_BLOCK)
    kj = jnp.arange(3 * A_BLOCK)
    rel = kj[None, :] - A_BLOCK - qi[:, None]
    kpos = jnp.arange(nb)[:, None] * A_BLOCK - A_BLOCK + kj[None, :]
    valid = (jnp.abs(rel) <= A_WINDOW)[None] & ((kpos >= 0) & (kpos < L))[:, None, :]
    s_loc = jnp.where(valid[None, :, None, None], s_loc, NEG_INF)
    col = jnp.broadcast_to(sink.astype(jnp.float32).reshape(A_KV_HEADS, G)[None, None, :, :, None, None], s_loc.shape[:-1] + (1,))
    p = jax.nn.softmax(jnp.concatenate([s_loc, s_ctx, col], axis=-1), axis=-1)
    nk = 3 * A_BLOCK
    p_loc = p[..., :nk].astype(v.dtype)
    p_ctx = p[..., nk:nk + ck.shape[1]].astype(v.dtype)
    o = jnp.einsum('bnkgqj,bnjkd->bnqkgd', p_loc, vb) + jnp.einsum('bnkgqc,bckd->bnqkgd', p_ctx, cv)
    return o.reshape(B, L, A_Q)


def neighbourhood_attention(q, k, v, ck, cv, rpb):
    B, L, H, _ = q.shape
    rows = L // GRID_W
    kr = min(NA_ROWS, rows)
    n_cb = GRID_W // NA_COL_BLOCK
    r = jnp.arange(rows)
    rs = jnp.clip(r - kr // 2, 0, rows - kr)
    row_idx = rs[:, None] + jnp.arange(kr)[None, :]
    cb_start = jnp.clip(jnp.arange(n_cb) * NA_COL_BLOCK - NA_COLS // 2, 0, GRID_W - NA_COL_SPAN)
    col_idx = cb_start[:, None] + jnp.arange(NA_COL_SPAN)[None, :]
    ri, ci = row_idx[:, None, :, None], col_idx[None, :, None, :]
    kg = k.reshape(B, rows, GRID_W, H, HEAD_DIM)[:, ri, ci]
    vg = v.reshape(B, rows, GRID_W, H, HEAD_DIM)[:, ri, ci]
    qb = q.reshape(B, rows, n_cb, NA_COL_BLOCK, H, HEAD_DIM)
    s_loc = jnp.einsum('brnqhd,brnijhd->bhrnqij', qb, kg).astype(jnp.float32) * SCALE
    qcol = jnp.arange(n_cb)[:, None] * NA_COL_BLOCK + jnp.arange(NA_COL_BLOCK)[None, :]
    cs = jnp.clip(qcol - NA_COLS // 2, 0, GRID_W - NA_COLS)
    kcol = col_idx[:, None, :]
    valid = (kcol >= cs[..., None]) & (kcol < cs[..., None] + NA_COLS)
    dr_idx = row_idx - r[:, None] + NA_ROWS - 1
    dc_idx = jnp.clip(kcol - qcol[..., None] + NA_COLS - 1, 0, 2 * NA_COLS - 2)
    bias = rpb[:, dr_idx[:, None, None, :, None], dc_idx[None, :, :, None, :]].astype(jnp.float32)
    s_loc = jnp.where(valid[None, None, None, :, :, None, :], s_loc + bias[None], NEG_INF)
    s_loc = s_loc.reshape(B, H, rows, n_cb, NA_COL_BLOCK, kr * NA_COL_SPAN)
    s_ctx = jnp.einsum('brnqhd,bchd->bhrnqc', qb, ck).astype(jnp.float32) * SCALE
    p = jax.nn.softmax(jnp.concatenate([s_loc, s_ctx], axis=-1), axis=-1)
    nk = kr * NA_COL_SPAN
    p_loc, p_ctx = p[..., :nk].astype(v.dtype), p[..., nk:].astype(v.dtype)
    vg = vg.reshape(B, rows, n_cb, nk, H, HEAD_DIM)
    o = jnp.einsum('bhrnqi,brnihd->brnqhd', p_loc, vg) + jnp.einsum('bhrnqc,bchd->brnqhd', p_ctx, cv)
    return o.reshape(B, L, B_W)


def rwkv_scan(r, decay, k, v, kk, a, s0, reverse):
    def step(S, inp):
        r_t, w_t, k_t, v_t, kk_t, a_t = inp
        sa = jnp.einsum('bhvk,bhk->bhv', S, -kk_t)
        S = S * w_t[:, :, None, :] + sa[..., None] * (kk_t * a_t)[:, :, None, :] + v_t[..., None] * k_t[:, :, None, :]
        return S, jnp.einsum('bhvk,bhk->bhv', S, r_t)

    xs = tuple(jnp.swapaxes(t, 0, 1) for t in (r, decay, k, v, kk, a))
    S, ys = lax.scan(step, s0, xs, reverse=reverse)
    return jnp.swapaxes(ys, 0, 1), S


def rwkv_mix(zc, lp, s0):
    B, L, _ = zc.shape
    f32 = jnp.float32
    sizes = (C_WIDTH, C_WIDTH, C_WIDTH, DECAY_RANK, DECAY_RANK, ICLR_RANK, ICLR_RANK, GATE_RANK)
    r, k, v, wlo_f, wlo_b, alo_f, alo_b, glo = jnp.split(zc, split_points(sizes), axis=-1)
    g = jax.nn.sigmoid(glo) @ lp['gate_up']
    kk = split_heads((k * lp['k_k']).astype(f32), C_HEADS)
    kk = kk / jnp.maximum(jnp.sqrt(jnp.sum(kk * kk, axis=-1, keepdims=True)), 1e-12)
    r32 = split_heads(r.astype(f32), C_HEADS)
    v32 = split_heads(v.astype(f32), C_HEADS)
    y = jnp.zeros_like(r32)
    bonus = jnp.zeros_like(r32)
    finals = []
    for d, (wlo, alo) in enumerate(((wlo_f, alo_f), (wlo_b, alo_b))):
        logw = -jax.nn.softplus(-(lp['decay_w0'][d] + jnp.tanh(wlo) @ lp['decay_up'][d]).astype(f32)) - 0.5
        decay = jnp.exp(-jnp.exp(logw))
        a = jax.nn.sigmoid((lp['iclr_a0'][d] + alo @ lp['iclr_up'][d]).astype(f32))
        k_d = k.astype(f32) * (1.0 + (a - 1.0) * lp['k_a'].astype(f32))
        decay, a, k_d = split_heads(decay, C_HEADS), split_heads(a, C_HEADS), split_heads(k_d, C_HEADS)
        y_d, s_d = rwkv_scan(r32, decay, k_d, v32, kk, a, s0[:, d].astype(f32), reverse=(d == 1))
        y = y + y_d
        bonus = bonus + jnp.sum(r32 * k_d * lp['r_k'].astype(f32), axis=-1, keepdims=True) * v32
        finals.append(s_d)
    mu = y.mean(-1, keepdims=True)
    var = jnp.mean(jnp.square(y - mu), -1, keepdims=True)
    yn = ((y - mu) * lax.rsqrt(var + GN_EPS)).reshape(B, L, C_WIDTH) * lp['gn_g'] + lp['gn_b']
    out = (yn + bonus.reshape(B, L, C_WIDTH)) * g
    return out.astype(zc.dtype), jnp.stack(finals, axis=1)


def merge_branches(oa, ob, oc, zg, lp):
    ga, gb, gc = jnp.split(jax.nn.sigmoid(zg), 3, axis=-1)
    merged = ga * (oa @ lp['proj_a']) + gb * (ob @ lp['proj_b']) + gc * (oc @ lp['proj_c'])
    return merged @ lp['w_out']


def mixer_context(h, lp):
    (qa, ka, va), (qb, kb, vb), zc, zg = project_inputs(h, lp['w_in'], lp['w_shift'])
    oa = context_attention(qa, ka, va, lp['attn_sink'])
    ob = context_attention(qb, kb, vb, None)
    s0 = jnp.zeros((h.shape[0], 2, C_HEADS, HEAD_DIM, HEAD_DIM), jnp.float32)
    oc, s_ctx = rwkv_mix(zc, lp, s0)
    return merge_branches(oa, ob, oc, zg, lp), (ka, va, kb, vb, s_ctx)


def mixer_latent(h, lp, ck_a, cv_a, ck_b, cv_b, s_ctx):
    (qa, ka, va), (qb, kb, vb), zc, zg = project_inputs(h, lp['w_in'], lp['w_shift'])
    oa = window_attention(rope_2d(qa), rope_2d(ka), va, ck_a, cv_a, lp['attn_sink'])
    ob = neighbourhood_attention(qb, kb, vb, ck_b, cv_b, lp['na_rpb'])
    oc, _ = rwkv_mix(zc, lp, s_ctx)
    return merge_branches(oa, ob, oc, zg, lp), ()


def run_layer(x, mod, lp, mixer):
    m = [mod[:, None, i] for i in range(N_MOD)]
    h = x * (1.0 + m[1]) + m[0]
    x = layer_norm(ALPHA * x + 0.5 * m[2] * swiglu(h, lp['ffn1_w_in'], lp['ffn1_w_out']), lp['ln_g'][0], lp['ln_b'][0])
    h = x * (1.0 + m[4]) + m[3]
    o, extras = mixer(h)
    x = layer_norm(ALPHA * x + m[5] * o, lp['ln_g'][1], lp['ln_b'][1])
    h = x * (1.0 + m[7]) + m[6]
    x = layer_norm(ALPHA * x + 0.5 * m[8] * swiglu(h, lp['ffn2_w_in'], lp['ffn2_w_out']), lp['ln_g'][2], lp['ln_b'][2])
    return x, extras


def setup_inputs(seed: int = 0) -> dict:
    key = jax.random.key(seed)
    ks = iter(jax.random.split(key, 48))

    def nrm(shape, scale=1.0):
        return jax.random.normal(next(ks), shape, jnp.float32) * scale

    d = DEPTH
    taps = jnp.array([0.25, 0.5, 0.25], jnp.float32)[None, :, None]
    return {
        'x_prompt': nrm((BATCH, SEQ, D_MODEL)),
        'x_sample': nrm((DEC_BATCH, DEC_SEQ, D_MODEL)),
        'cache_attn_k': nrm((DEC_BATCH, DEPTH, PAST_LEN, A_KV_HEADS, HEAD_DIM)),
        'cache_attn_v': nrm((DEC_BATCH, DEPTH, PAST_LEN, A_KV_HEADS, HEAD_DIM)),
        'cache_na_k': nrm((DEC_BATCH, DEPTH, PAST_LEN, B_HEADS, HEAD_DIM)),
        'cache_na_v': nrm((DEC_BATCH, DEPTH, PAST_LEN, B_HEADS, HEAD_DIM)),
        'state_rwkv': nrm((DEC_BATCH, DEPTH, 2, C_HEADS, HEAD_DIM, HEAD_DIM), 0.5),
        'c': nrm((DEC_BATCH, D_MODEL)),
        'c_ctx': nrm((D_MODEL,)),
        'w_ada': nrm((d, D_MODEL, N_MOD * D_MODEL), 0.5 * D_MODEL ** -0.5),
        'b_ada': nrm((d, N_MOD * D_MODEL), 0.02),
        'ffn1_w_in': nrm((d, D_MODEL, 2 * D_FF), D_MODEL ** -0.5),
        'ffn1_w_out': nrm((d, D_FF, D_MODEL), BETA * D_FF ** -0.5),
        'ffn2_w_in': nrm((d, D_MODEL, 2 * D_FF), D_MODEL ** -0.5),
        'ffn2_w_out': nrm((d, D_FF, D_MODEL), BETA * D_FF ** -0.5),
        'w_in': nrm((d, D_MODEL, IN_COLS), D_MODEL ** -0.5),
        'w_shift': taps + nrm((d, SHIFT_WIDTH, C_COLS), 0.05),
        'attn_sink': nrm((d, A_HEADS), 0.5),
        'na_rpb': nrm((d, B_HEADS, 2 * NA_ROWS - 1, 2 * NA_COLS - 1), 0.1),
        'decay_w0': jax.random.uniform(next(ks), (d, 2, C_WIDTH), jnp.float32, -6.0, -1.0),
        'decay_up': nrm((d, 2, DECAY_RANK, C_WIDTH), 0.1 * DECAY_RANK ** -0.5),
        'iclr_a0': nrm((d, 2, C_WIDTH), 0.1),
        'iclr_up': nrm((d, 2, ICLR_RANK, C_WIDTH), 0.1 * ICLR_RANK ** -0.5),
        'gate_up': nrm((d, GATE_RANK, C_WIDTH), GATE_RANK ** -0.5),
        'k_k': 0.85 + nrm((d, C_WIDTH), 0.02),
        'k_a': 1.0 + nrm((d, C_WIDTH), 0.02),
        'r_k': nrm((d, C_HEADS, HEAD_DIM), 0.1),
        'gn_g': 1.0 + nrm((d, C_WIDTH), 0.02),
        'gn_b': nrm((d, C_WIDTH), 0.02),
        'proj_a': nrm((d, A_Q, D_MODEL), A_Q ** -0.5),
        'proj_b': nrm((d, B_W, D_MODEL), B_W ** -0.5),
        'proj_c': nrm((d, C_WIDTH, D_MODEL), C_WIDTH ** -0.5),
        'w_out': nrm((d, D_MODEL, D_MODEL), BETA * D_MODEL ** -0.5),
        'ln_g': 1.0 + nrm((d, 3, D_MODEL), 0.02),
        'ln_b': nrm((d, 3, D_MODEL), 0.02),
    }


def reference(x_prompt, x_sample, cache_attn_k, cache_attn_v, cache_na_k, cache_na_v, state_rwkv, c, c_ctx,
              w_ada, b_ada, ffn1_w_in, ffn1_w_out, ffn2_w_in, ffn2_w_out, w_in, w_shift, attn_sink, na_rpb,
              decay_w0, decay_up, iclr_a0, iclr_up, gate_up, k_k, k_a, r_k, gn_g, gn_b,
              proj_a, proj_b, proj_c, w_out, ln_g, ln_b):
    y_p, y_s = x_prompt, x_sample
    new_ka, new_va, new_kb, new_vb, new_st = [], [], [], [], []
    for l in range(DEPTH):
        lp = dict(ffn1_w_in=ffn1_w_in[l], ffn1_w_out=ffn1_w_out[l], ffn2_w_in=ffn2_w_in[l], ffn2_w_out=ffn2_w_out[l],
                  w_in=w_in[l], w_shift=w_shift[l], attn_sink=attn_sink[l], na_rpb=na_rpb[l],
                  decay_w0=decay_w0[l], decay_up=decay_up[l], iclr_a0=iclr_a0[l], iclr_up=iclr_up[l],
                  gate_up=gate_up[l], k_k=k_k[l], k_a=k_a[l], r_k=r_k[l], gn_g=gn_g[l], gn_b=gn_b[l],
                  proj_a=proj_a[l], proj_b=proj_b[l], proj_c=proj_c[l], w_out=w_out[l], ln_g=ln_g[l], ln_b=ln_b[l])
        mod_ctx = adaln_mods(c_ctx[None, :], w_ada[l], b_ada[l])
        mod_lat = adaln_mods(c, w_ada[l], b_ada[l])
        y_p, (ka, va, kb, vb, st) = run_layer(y_p, mod_ctx, lp, lambda h: mixer_context(h, lp))
        new_ka.append(ka)
        new_va.append(va)
        new_kb.append(kb)
        new_vb.append(vb)
        new_st.append(st)
        y_s, _ = run_layer(y_s, mod_lat, lp, lambda h: mixer_latent(h, lp, cache_attn_k[:, l], cache_attn_v[:, l],
                                                                       cache_na_k[:, l], cache_na_v[:, l], state_rwkv[:, l]))
    new_attn_k = jnp.stack(new_ka, axis=1)
    new_attn_v = jnp.stack(new_va, axis=1)
    new_na_k = jnp.stack(new_kb, axis=1)
    new_na_v = jnp.stack(new_vb, axis=1)
    new_rwkv_state = jnp.stack(new_st, axis=1)
    return (y_p, y_s, new_attn_k, new_attn_v, new_na_k, new_na_v, new_rwkv_state)
```

```python
import functools

import jax
import jax.numpy as jnp
import numpy as np
from jax import lax
from jax.experimental import pallas as pl
from jax.experimental.pallas import tpu as pltpu

D_MODEL = 1024
BATCH = 16
SEQ = 256
DEPTH = 2
DEC_BATCH = 8
DEC_SEQ = 1024
PAST_LEN = 256
GRID_W = 64
GRID_ROWS = DEC_SEQ // GRID_W
HEAD_DIM = 64
SCALE = HEAD_DIM ** -0.5
A_HEADS = 8
A_KV_HEADS = 2
A_GROUP = A_HEADS // A_KV_HEADS
A_WINDOW = 128
A_BLOCK = 128
A_Q = A_HEADS * HEAD_DIM
A_KV = A_KV_HEADS * HEAD_DIM
A_COLS = A_Q + 2 * A_KV
B_HEADS = 8
B_W = B_HEADS * HEAD_DIM
B_COLS = 3 * B_W
NA_ROWS = 8
NA_COLS = 16
C_HEADS = 8
C_WIDTH = C_HEADS * HEAD_DIM
DECAY_RANK = 64
ICLR_RANK = 64
GATE_RANK = 128
C_COLS = 3 * C_WIDTH + 2 * DECAY_RANK + 2 * ICLR_RANK + GATE_RANK
GN_EPS = 64e-5
GATE_COLS = 3 * D_MODEL
IN_COLS = A_COLS + B_COLS + C_COLS + GATE_COLS
D_FF = 2816
N_MOD = 9
ROPE_BASE = 10000.0
LN_EPS = 1e-5
NEG_INF = -1e30
ALPHA = (2 * DEPTH) ** 0.25

N_CTX = BATCH * SEQ
N_LAT = DEC_BATCH * DEC_SEQ
N_TOK = N_CTX + N_LAT
TM = 256
N_TILES = N_TOK // TM
CTX_TILES = N_CTX // TM
LAT_TILES_PER_SEQ = DEC_SEQ // TM
MOD_ROWS = 16
LANES = 128
SUBLANES = 8
SCAN_TB = 8
VMEM_LIMIT = 56 * 1024 * 1024

F32 = jnp.float32
BF16 = jnp.bfloat16


def _sigmoid(x):
    return 1.0 / (1.0 + jnp.exp(-x))


def _layer_norm(y, g, b):
    mu = jnp.mean(y, axis=-1, keepdims=True)
    d = y - mu
    var = jnp.mean(d * d, axis=-1, keepdims=True)
    return d * lax.rsqrt(var + LN_EPS) * g + b


def _dot(a, b):
    return jnp.dot(a, b, preferred_element_type=F32)


def _dot_nt(a, b):
    return lax.dot_general(a, b, (((1,), (1,)), ((), ())), preferred_element_type=F32)


def _dot_f32(a, b):
    return jnp.dot(a, b, preferred_element_type=F32, precision=lax.Precision.HIGHEST)


def _mod_tile_index(i):
    return jnp.where(i < CTX_TILES, DEC_BATCH, (i - CTX_TILES) // LAT_TILES_PER_SEQ)


def _resident(shape):
    zeros = (0,) * len(shape)
    return pl.BlockSpec(shape, lambda *_: zeros, pipeline_mode=pl.Buffered(1))


def _params(*sem):
    return pltpu.CompilerParams(dimension_semantics=sem, vmem_limit_bytes=VMEM_LIMIT)


def _ada_kernel(c_ref, w_ref, b_ref, o_ref):
    c = c_ref[...]
    s = (c * _sigmoid(c)).astype(BF16)
    o_ref[0] = _dot(s, w_ref[0].astype(BF16)) + b_ref[0]


def _ada_mods(c_rows, w_ada, b_ada):
    out = pl.pallas_call(
        _ada_kernel,
        grid=(DEPTH, N_MOD),
        in_specs=[
            pl.BlockSpec((MOD_ROWS, D_MODEL), lambda l, j: (0, 0)),
            pl.BlockSpec((1, D_MODEL, D_MODEL), lambda l, j: (l, 0, j)),
            pl.BlockSpec((1, 1, D_MODEL), lambda l, j: (l, 0, j)),
        ],
        out_specs=pl.BlockSpec((1, MOD_ROWS, D_MODEL), lambda l, j: (l, 0, j)),
        out_shape=jax.ShapeDtypeStruct((DEPTH, MOD_ROWS, N_MOD * D_MODEL), F32),
        compiler_params=_params("parallel", "parallel"),
        name="ada_mods",
    )(c_rows, w_ada, b_ada.reshape(DEPTH, 1, N_MOD * D_MODEL))
    return out.reshape(DEPTH, MOD_ROWS, N_MOD, D_MODEL)


def _ffn_kernel(x_ref, mod_ref, w1_ref, w2_ref, g_ref, b_ref, o_ref, *, base):
    x = x_ref[...]
    shift = mod_ref[0, base:base + 1, :]
    scale = mod_ref[0, base + 1:base + 2, :]
    gate = mod_ref[0, base + 2:base + 3, :]
    h = (x * (1.0 + scale) + shift).astype(BF16)
    a = _dot(h, w1_ref[:, :D_FF])
    g = _dot(h, w1_ref[:, D_FF:])
    u = (a * _sigmoid(a) * g).astype(BF16)
    f = _dot(u, w2_ref[...])
    o_ref[...] = _layer_norm(ALPHA * x + 0.5 * gate * f, g_ref[...], b_ref[...])


def _ffn(x, mods, w1, w2, ln_g, ln_b, base):
    return pl.pallas_call(
        functools.partial(_ffn_kernel, base=base),
        grid=(N_TILES,),
        in_specs=[
            pl.BlockSpec((TM, D_MODEL), lambda i: (i, 0)),
            pl.BlockSpec((1, N_MOD, D_MODEL), lambda i: (_mod_tile_index(i), 0, 0)),
            _resident((D_MODEL, 2 * D_FF)),
            _resident((D_FF, D_MODEL)),
            _resident((1, D_MODEL)),
            _resident((1, D_MODEL)),
        ],
        out_specs=pl.BlockSpec((TM, D_MODEL), lambda i: (i, 0)),
        out_shape=jax.ShapeDtypeStruct((N_TOK, D_MODEL), F32),
        compiler_params=_params("parallel"),
        name=f"ffn{base // 6 + 1}",
    )(x, mods, w1, w2, ln_g.reshape(1, D_MODEL), ln_b.reshape(1, D_MODEL))


_COL_A = 0
_COL_B = A_COLS
_COL_C = A_COLS + B_COLS
_COL_G = A_COLS + B_COLS + C_COLS


def _inproj_kernel(x_ref, mod_ref, w_ref, za_ref, zb_ref, zc_ref, sg_ref):
    x = x_ref[...]
    h = (x * (1.0 + mod_ref[0, 4:5, :]) + mod_ref[0, 3:4, :]).astype(BF16)
    za_ref[...] = _dot(h, w_ref[:, _COL_A:_COL_B])
    zb_ref[...] = _dot(h, w_ref[:, _COL_B:_COL_C])
    zc_ref[...] = _dot(h, w_ref[:, _COL_C:_COL_G])
    sg_ref[...] = _sigmoid(_dot(h, w_ref[:, _COL_G:]))


def _inproj(x, mods, w_in):
    widths = (A_COLS, B_COLS, C_COLS, GATE_COLS)
    return pl.pallas_call(
        _inproj_kernel,
        grid=(N_TILES,),
        in_specs=[
            pl.BlockSpec((TM, D_MODEL), lambda i: (i, 0)),
            pl.BlockSpec((1, N_MOD, D_MODEL), lambda i: (_mod_tile_index(i), 0, 0)),
            _resident((D_MODEL, IN_COLS)),
        ],
        out_specs=[pl.BlockSpec((TM, w), lambda i: (i, 0)) for w in widths],
        out_shape=[jax.ShapeDtypeStruct((N_TOK, w), F32) for w in widths],
        compiler_params=_params("parallel"),
        name="inproj",
    )(x, mods, w_in)


def _softmax_pv(scores, values, sink=None):
    m = scores[0].max(axis=-1, keepdims=True)
    for s in scores[1:]:
        m = jnp.maximum(m, s.max(axis=-1, keepdims=True))
    if sink is not None:
        m = jnp.maximum(m, sink)
    den = jnp.exp(sink - m) if sink is not None else 0.0
    acc = None
    for s, v in zip(scores, values):
        p = jnp.exp(s - m)
        den = den + p.sum(axis=-1, keepdims=True)
        pv = _dot(p.astype(BF16), v)
        acc = pv if acc is None else acc + pv
    return acc / den


def _head(x, h):
    return x[:, h * HEAD_DIM:(h + 1) * HEAD_DIM]


def _ctx_attn_kernel(sink_ref, za_ref, zb_ref, o_ref):
    za = za_ref[...]
    zb = zb_ref[...]
    outs = []
    row = lax.broadcasted_iota(jnp.int32, (A_GROUP * SEQ, 1), 0) // SEQ
    for g in range(A_KV_HEADS):
        q4 = jnp.concatenate([_head(za, A_GROUP * g + j) for j in range(A_GROUP)], axis=0).astype(BF16)
        k = _head(za, A_HEADS + g).astype(BF16)
        v = _head(za, A_HEADS + A_KV_HEADS + g).astype(BF16)
        sink = jnp.zeros((A_GROUP * SEQ, 1), F32)
        for j in range(A_GROUP):
            sink = jnp.where(row == j, sink_ref[A_GROUP * g + j], sink)
        o4 = _softmax_pv([_dot_nt(q4, k) * SCALE], [v], sink)
        outs.extend(o4[j * SEQ:(j + 1) * SEQ] for j in range(A_GROUP))
    for h in range(B_HEADS):
        q = _head(zb, h).astype(BF16)
        k = _head(zb, B_HEADS + h).astype(BF16)
        v = _head(zb, 2 * B_HEADS + h).astype(BF16)
        outs.append(_softmax_pv([_dot_nt(q, k) * SCALE], [v]))
    o_ref[...] = jnp.concatenate(outs, axis=-1)


def _ctx_attn(za, zb, sink):
    return pl.pallas_call(
        _ctx_attn_kernel,
        grid=(BATCH,),
        in_specs=[
            pl.BlockSpec(memory_space=pltpu.SMEM),
            pl.BlockSpec((SEQ, A_COLS), lambda b: (b, 0)),
            pl.BlockSpec((SEQ, B_COLS), lambda b: (b, 0)),
        ],
        out_specs=pl.BlockSpec((SEQ, A_Q + B_W), lambda b: (b, 0)),
        out_shape=jax.ShapeDtypeStruct((N_CTX, A_Q + B_W), F32),
        compiler_params=_params("parallel"),
        name="ctx_attn",
    )(sink, za, zb)


def _rope(x, cos, sin_signed):
    w = x.shape[-1]
    half = HEAD_DIM // 2
    lane = lax.broadcasted_iota(jnp.int32, x.shape, 1)
    partner = jnp.where(lane % HEAD_DIM < half, pltpu.roll(x, w - half, 1), pltpu.roll(x, half, 1))
    return x * cos + partner * sin_signed


def _win_attn_kernel(sink_ref, za_ref, ck_ref, cv_ref, cos_ref, sin_ref, o_ref):
    n = pl.program_id(1)
    nb = DEC_SEQ // A_BLOCK
    q0 = pl.multiple_of(n * A_BLOCK, A_BLOCK)
    q = _rope(za_ref[pl.ds(q0, A_BLOCK), :A_Q], cos_ref[pl.ds(q0, A_BLOCK), :], sin_ref[pl.ds(q0, A_BLOCK), :])
    ks, vs = [], []
    for j in range(3):
        kb = jnp.clip(n - 1 + j, 0, nb - 1)
        k0 = pl.multiple_of(kb * A_BLOCK, A_BLOCK)
        kblk = za_ref[pl.ds(k0, A_BLOCK), A_Q:A_Q + A_KV]
        ks.append(_rope(kblk, cos_ref[pl.ds(k0, A_BLOCK), :A_KV], sin_ref[pl.ds(k0, A_BLOCK), :A_KV]))
        vs.append(za_ref[pl.ds(k0, A_BLOCK), A_Q + A_KV:])
    kband = jnp.concatenate(ks, axis=0).astype(BF16)
    vband = jnp.concatenate(vs, axis=0).astype(BF16)
    ck = ck_ref[0].astype(BF16)
    cv = cv_ref[0].astype(BF16)
    m4 = A_GROUP * A_BLOCK
    qi = lax.broadcasted_iota(jnp.int32, (m4, 3 * A_BLOCK), 0) % A_BLOCK
    kj = lax.broadcasted_iota(jnp.int32, (m4, 3 * A_BLOCK), 1)
    rel = kj - A_BLOCK - qi
    kpos = (n - 1) * A_BLOCK + kj
    valid = (jnp.abs(rel) <= A_WINDOW) & (kpos >= 0) & (kpos < DEC_SEQ)
    row = lax.broadcasted_iota(jnp.int32, (m4, 1), 0) // A_BLOCK
    outs = []
    for g in range(A_KV_HEADS):
        q4 = jnp.concatenate([_head(q, A_GROUP * g + j) for j in range(A_GROUP)], axis=0).astype(BF16)
        s_loc = jnp.where(valid, _dot_nt(q4, _head(kband, g)) * SCALE, NEG_INF)
        s_ctx = _dot_nt(q4, _head(ck, g)) * SCALE
        sink = jnp.zeros((m4, 1), F32)
        for j in range(A_GROUP):
            sink = jnp.where(row == j, sink_ref[A_GROUP * g + j], sink)
        o4 = _softmax_pv([s_loc, s_ctx], [_head(vband, g), _head(cv, g)], sink)
        outs.extend(o4[j * A_BLOCK:(j + 1) * A_BLOCK] for j in range(A_GROUP))
    o_ref[...] = jnp.concatenate(outs, axis=-1)


def _rope_tables():
    t = np.arange(DEC_SEQ)
    n_freq = HEAD_DIM // 4
    inv = ROPE_BASE ** (-jnp.arange(n_freq, dtype=F32) / n_freq)
    rows = jnp.asarray(t // GRID_W, F32)
    cols = jnp.asarray(t % GRID_W, F32)
    ang = jnp.concatenate([rows[:, None] * inv, cols[:, None] * inv], axis=-1)
    cos, sin = jnp.cos(ang), jnp.sin(ang)
    cos_h = jnp.concatenate([cos, cos], axis=-1)
    sin_h = jnp.concatenate([-sin, sin], axis=-1)
    return jnp.tile(cos_h, (1, A_HEADS)), jnp.tile(sin_h, (1, A_HEADS))


def _win_attn(za, ck, cv, sink, cos_t, sin_t):
    off = N_CTX // DEC_SEQ
    nb = DEC_SEQ // A_BLOCK
    return pl.pallas_call(
        _win_attn_kernel,
        grid=(DEC_BATCH, nb),
        in_specs=[
            pl.BlockSpec(memory_space=pltpu.SMEM),
            pl.BlockSpec((DEC_SEQ, A_COLS), lambda b, n: (b + off, 0)),
            pl.BlockSpec((1, PAST_LEN, A_KV), lambda b, n: (b, 0, 0)),
            pl.BlockSpec((1, PAST_LEN, A_KV), lambda b, n: (b, 0, 0)),
            pl.BlockSpec((DEC_SEQ, A_Q), lambda b, n: (0, 0)),
            pl.BlockSpec((DEC_SEQ, A_Q), lambda b, n: (0, 0)),
        ],
        out_specs=pl.BlockSpec((A_BLOCK, A_Q), lambda b, n: (b * nb + n, 0)),
        out_shape=jax.ShapeDtypeStruct((N_LAT, A_Q), F32),
        compiler_params=_params("parallel", "arbitrary"),
        name="win_attn",
    )(sink, za, ck, cv, cos_t, sin_t)


_NA_QBLK = 256
_RPB_I = 2 * NA_ROWS - 1
_RPB_J = 2 * NA_COLS - 1


def _na_build_bias(rpb_ref, head, bias_ref, slot):
    shape = (GRID_W, 2 * GRID_W)
    qc = lax.broadcasted_iota(jnp.int32, shape, 0)
    lane = lax.broadcasted_iota(jnp.int32, shape, 1)
    kc = lane % GRID_W
    left = lane < GRID_W
    cs = jnp.clip(qc - NA_COLS // 2, 0, GRID_W - NA_COLS)
    idx = jnp.where((kc >= cs) & (kc < cs + NA_COLS), jnp.clip(kc - qc + NA_COLS - 1, 0, _RPB_J - 1), -1)
    base = head * (_RPB_I * _RPB_J)
    tables = []
    for i in range(_RPB_I):
        t = jnp.full(shape, NEG_INF, F32)
        for j in range(_RPB_J):
            t = jnp.where(idx == j, rpb_ref[base + i * _RPB_J + j], t)
        tables.append(t)
    neg = jnp.full(shape, NEG_INF, F32)
    half = NA_ROWS // 2
    for qr in range(GRID_ROWS):
        rs = min(max(qr - half, 0), GRID_ROWS - NA_ROWS)
        for m in range(GRID_ROWS // 2):
            kr0, kr1 = 2 * m, 2 * m + 1
            v0 = rs <= kr0 < rs + NA_ROWS
            v1 = rs <= kr1 < rs + NA_ROWS
            i0 = kr0 - qr + NA_ROWS - 1
            if v0 and v1:
                piece = jnp.where(left, tables[i0], tables[i0 + 1])
            elif v0:
                piece = jnp.where(left, tables[i0], NEG_INF)
            elif v1:
                piece = jnp.where(left, NEG_INF, tables[i0 + 1])
            else:
                piece = neg
            bias_ref[slot, qr * GRID_W:(qr + 1) * GRID_W, m * 2 * GRID_W:(m + 1) * 2 * GRID_W] = piece


def _na_attn_kernel(rpb_ref, q_ref, k_ref, v_ref, ck_ref, cv_ref, o_ref, bias_ref):
    hp = pl.program_id(0)

    @pl.when(pl.program_id(1) == 0)
    def _():
        for half in range(2):
            _na_build_bias(rpb_ref, 2 * hp + half, bias_ref, half)

    k2 = k_ref[...].astype(BF16)
    v2 = v_ref[...].astype(BF16)
    ck2 = ck_ref[0].astype(BF16)
    cv2 = cv_ref[0].astype(BF16)
    for qb in range(DEC_SEQ // _NA_QBLK):
        rows = slice(qb * _NA_QBLK, (qb + 1) * _NA_QBLK)
        q2 = q_ref[rows, :].astype(BF16)
        outs = []
        for half in range(2):
            q = _head(q2, half)
            s_loc = _dot_nt(q, _head(k2, half)) * SCALE + bias_ref[half, rows, :]
            s_ctx = _dot_nt(q, _head(ck2, half)) * SCALE
            outs.append(_softmax_pv([s_loc, s_ctx], [_head(v2, half), _head(cv2, half)]))
        o_ref[rows, :] = jnp.concatenate(outs, axis=-1)


def _na_attn(zb, ck, cv, rpb_flat):
    off = N_CTX // DEC_SEQ
    pairs = B_HEADS // 2
    pw = 2 * HEAD_DIM
    return pl.pallas_call(
        _na_attn_kernel,
        grid=(pairs, DEC_BATCH),
        in_specs=[
            pl.BlockSpec(memory_space=pltpu.SMEM),
            pl.BlockSpec((DEC_SEQ, pw), lambda hp, b: (b + off, hp)),
            pl.BlockSpec((DEC_SEQ, pw), lambda hp, b: (b + off, pairs + hp)),
            pl.BlockSpec((DEC_SEQ, pw), lambda hp, b: (b + off, 2 * pairs + hp)),
            pl.BlockSpec((1, PAST_LEN, pw), lambda hp, b: (b, 0, hp)),
            pl.BlockSpec((1, PAST_LEN, pw), lambda hp, b: (b, 0, hp)),
        ],
        out_specs=pl.BlockSpec((DEC_SEQ, pw), lambda hp, b: (b, hp)),
        out_shape=jax.ShapeDtypeStruct((N_LAT, B_W), F32),
        scratch_shapes=[pltpu.VMEM((2, DEC_SEQ, DEC_SEQ), F32)],
        compiler_params=_params("arbitrary", "arbitrary"),
        name="na_attn",
    )(rpb_flat, zb, zb, zb, ck, cv)


_C_R, _C_K, _C_V = 0, C_WIDTH, 2 * C_WIDTH
_C_WLO = 3 * C_WIDTH
_C_ALO = _C_WLO + 2 * DECAY_RANK
_C_GLO = _C_ALO + 2 * ICLR_RANK
_HALO = SUBLANES


def _rwkv_prep_kernel(z_ref, zp_ref, zn_ref, wsh_ref, dup_ref, aup_ref, gup_ref, w0_ref, a0_ref, kk_ref, ka_ref,
                      rk_ref, ones_ref,
                      r_ref, v_ref, nkk_ref, wf_ref, wb_ref, bf_ref, bb_ref, kf_ref, kb_ref, bonus_ref, g_ref):
    i = pl.program_id(0)
    j = (i - CTX_TILES) % LAT_TILES_PER_SEQ
    is_lat = i >= CTX_TILES
    has_prev = jnp.where(is_lat & (j != 0), 1.0, 0.0)
    has_next = jnp.where(is_lat & (j != LAT_TILES_PER_SEQ - 1), 1.0, 0.0)
    z = z_ref[...]
    row = lax.broadcasted_iota(jnp.int32, z.shape, 0)
    z_prev = jnp.where(row == 0, zp_ref[_HALO - 1:_HALO, :] * has_prev, pltpu.roll(z, 1, 0))
    z_next = jnp.where(row == TM - 1, zn_ref[0:1, :] * has_next, pltpu.roll(z, TM - 1, 0))
    zc = z_prev * wsh_ref[0:1, :] + z * wsh_ref[1:2, :] + z_next * wsh_ref[2:3, :]

    r = zc[:, _C_R:_C_R + C_WIDTH]
    k = zc[:, _C_K:_C_K + C_WIDTH]
    v = zc[:, _C_V:_C_V + C_WIDTH]
    wlo = zc[:, _C_WLO:_C_WLO + 2 * DECAY_RANK]
    alo = zc[:, _C_ALO:_C_ALO + 2 * ICLR_RANK]
    glo = zc[:, _C_GLO:_C_GLO + GATE_RANK]
    ones = ones_ref[...]

    g_ref[...] = _dot(_sigmoid(glo).astype(BF16), gup_ref[...])
    kk = k * kk_ref[...]
    norm = jnp.sqrt(_dot_f32(kk * kk, ones))
    kk = kk / jnp.maximum(norm, 1e-12)
    x = w0_ref[...] + _dot(jnp.tanh(wlo).astype(BF16), dup_ref[...])
    logw = -(jnp.maximum(-x, 0.0) + jnp.log(1.0 + jnp.exp(-jnp.abs(x)))) - 0.5
    decay = jnp.exp(-jnp.exp(logw))
    a = _sigmoid(a0_ref[...] + _dot(alo.astype(BF16), aup_ref[...]))
    ka = ka_ref[...]
    kd_sum = None
    for d, (w_out, b_out, k_out) in enumerate(((wf_ref, bf_ref, kf_ref), (wb_ref, bb_ref, kb_ref))):
        a_d = a[:, d * C_WIDTH:(d + 1) * C_WIDTH]
        k_d = k * (1.0 + (a_d - 1.0) * ka)
        w_out[...] = decay[:, d * C_WIDTH:(d + 1) * C_WIDTH]
        b_out[...] = kk * a_d
        k_out[...] = k_d
        kd_sum = k_d if kd_sum is None else kd_sum + k_d
    bonus_ref[...] = _dot_f32(r * kd_sum * rk_ref[...], ones) * v
    r_ref[...] = r
    v_ref[...] = v
    nkk_ref[...] = -kk


def _rwkv_prep(zc_raw, w_shift, dec_up, iclr_up, gate_up, w0, a0, k_k, k_a, r_k, ones_bd):
    tiles_per_halo = TM // _HALO
    n_halo = N_TOK // _HALO
    row = lambda n: _resident((1, n))
    outs = 11
    return pl.pallas_call(
        _rwkv_prep_kernel,
        grid=(N_TILES,),
        in_specs=[
            pl.BlockSpec((TM, C_COLS), lambda i: (i, 0)),
            pl.BlockSpec((_HALO, C_COLS), lambda i: (jnp.maximum(i * tiles_per_halo - 1, 0), 0)),
            pl.BlockSpec((_HALO, C_COLS), lambda i: (jnp.minimum((i + 1) * tiles_per_halo, n_halo - 1), 0)),
            _resident((3, C_COLS)),
            _resident((2 * DECAY_RANK, 2 * C_WIDTH)),
            _resident((2 * ICLR_RANK, 2 * C_WIDTH)),
            _resident((GATE_RANK, C_WIDTH)),
            row(2 * C_WIDTH), row(2 * C_WIDTH), row(C_WIDTH), row(C_WIDTH), row(C_WIDTH),
            _resident((C_WIDTH, C_WIDTH)),
        ],
        out_specs=[pl.BlockSpec((TM, C_WIDTH), lambda i: (i, 0))] * outs,
        out_shape=[jax.ShapeDtypeStruct((N_TOK, C_WIDTH), F32)] * outs,
        compiler_params=_params("parallel"),
        name="rwkv_prep",
    )(zc_raw, zc_raw, zc_raw, w_shift, dec_up, iclr_up, gate_up, w0, a0, k_k, k_a, r_k, ones_bd)


def _scan_kernel(w_ref, a_ref, b_ref, k_ref, r_ref, v_ref, s0_ref, y_ref, sT_ref, state):
    tb = pl.program_id(1)

    @pl.when(tb == 0)
    def _():
        state[...] = s0_ref[0]

    def step(t, carry):
        w = w_ref[0, t]
        a = a_ref[0, t]
        b = b_ref[0, t]
        k = k_ref[0, t]
        r = r_ref[0, t]

        def vgroup(vg, carry2):
            v0 = pl.multiple_of(vg * SUBLANES, SUBLANES)
            vt = v_ref[0, t, pl.ds(v0, SUBLANES), :]
            ys = []
            for j in range(SUBLANES):
                s = state[v0 + j]
                u = jnp.sum(s * a, axis=0, keepdims=True)
                s = s * w + u * b + vt[j:j + 1, :] * k
                state[v0 + j] = s
                ys.append(jnp.sum(s * r, axis=0, keepdims=True))
            y_ref[0, t, pl.ds(v0, SUBLANES), :] = jnp.concatenate(ys, axis=0)
            return carry2

        lax.fori_loop(0, HEAD_DIM // SUBLANES, vgroup, 0)
        return carry

    lax.fori_loop(0, SCAN_TB, step, 0)

    @pl.when(tb == pl.num_programs(1) - 1)
    def _():
        sT_ref[0] = state[...]


def _scan(w, a, b, k, r, v, s0):
    groups, steps = w.shape[0], w.shape[1]
    op = pl.BlockSpec((1, SCAN_TB, HEAD_DIM, LANES), lambda g, t: (g, t, 0, 0))
    st = pl.BlockSpec((1, HEAD_DIM, HEAD_DIM, LANES), lambda g, t: (g, 0, 0, 0))
    return pl.pallas_call(
        _scan_kernel,
        grid=(groups, steps // SCAN_TB),
        in_specs=[op] * 6 + [st],
        out_specs=[op, st],
        out_shape=[jax.ShapeDtypeStruct((groups, steps, HEAD_DIM, LANES), F32),
                   jax.ShapeDtypeStruct((groups, HEAD_DIM, HEAD_DIM, LANES), F32)],
        scratch_shapes=[pltpu.VMEM((HEAD_DIM, HEAD_DIM, LANES), F32)],
        compiler_params=_params("parallel", "arbitrary"),
        name="rwkv_scan",
    )(w, a, b, k, r, v, s0)


def _chains_ctx(x):
    return x.reshape(BATCH, SEQ, C_HEADS, HEAD_DIM).transpose(1, 3, 0, 2).reshape(SEQ, HEAD_DIM, BATCH * C_HEADS)


def _chains_lat(x):
    return x.reshape(DEC_BATCH, DEC_SEQ, C_HEADS, HEAD_DIM).transpose(1, 3, 0, 2).reshape(
        DEC_SEQ, HEAD_DIM, DEC_BATCH * C_HEADS)


def _to_scan(xf, xb):
    ctx = jnp.stack([_chains_ctx(xf[:N_CTX]), jnp.flip(_chains_ctx(xb[:N_CTX]), 0)])
    lat = jnp.concatenate([_chains_lat(xf[N_CTX:]), jnp.flip(_chains_lat(xb[N_CTX:]), 0)], axis=-1)[None]
    return ctx, lat


def _tokens_ctx(y):
    return y.reshape(SEQ, HEAD_DIM, BATCH, C_HEADS).transpose(2, 0, 3, 1).reshape(N_CTX, C_WIDTH)


def _tokens_lat(y):
    return y.reshape(DEC_SEQ, HEAD_DIM, DEC_BATCH, C_HEADS).transpose(2, 0, 3, 1).reshape(N_LAT, C_WIDTH)


def _from_scan(y_ctx, y_lat):
    half = DEC_BATCH * C_HEADS
    yf = jnp.concatenate([_tokens_ctx(y_ctx[0]), _tokens_lat(y_lat[0, :, :, :half])])
    yb = jnp.concatenate([_tokens_ctx(jnp.flip(y_ctx[1], 0)), _tokens_lat(jnp.flip(y_lat[0, :, :, half:], 0))])
    return yf, yb


def _merge_kernel(x_ref, mod_ref, oab_ref, yf_ref, yb_ref, bonus_ref, g_ref, sg_ref, pa_ref, pb_ref, pc_ref, wo_ref,
                  avg_ref, gng_ref, gnb_ref, lng_ref, lnb_ref, o_ref):
    y = yf_ref[...] + yb_ref[...]
    avg = avg_ref[...]
    mu = _dot_f32(y, avg)
    d = y - mu
    var = _dot_f32(d * d, avg)
    yn = d * lax.rsqrt(var + GN_EPS) * gng_ref[...] + gnb_ref[...]
    oc = ((yn + bonus_ref[...]) * g_ref[...]).astype(BF16)
    oa = oab_ref[:, :A_Q].astype(BF16)
    ob = oab_ref[:, A_Q:].astype(BF16)
    merged = (sg_ref[:, :D_MODEL] * _dot(oa, pa_ref[...])
              + sg_ref[:, D_MODEL:2 * D_MODEL] * _dot(ob, pb_ref[...])
              + sg_ref[:, 2 * D_MODEL:] * _dot(oc, pc_ref[...]))
    o = _dot(merged.astype(BF16), wo_ref[...])
    o_ref[...] = _layer_norm(ALPHA * x_ref[...] + mod_ref[0, 5:6, :] * o, lng_ref[...], lnb_ref[...])


def _merge(x, mods, oab, yf, yb, bonus, g, sg, pa, pb, pc, wo, avg, gn_g, gn_b, ln_g, ln_b):
    tile = lambda w: pl.BlockSpec((TM, w), lambda i: (i, 0))
    return pl.pallas_call(
        _merge_kernel,
        grid=(N_TILES,),
        in_specs=[
            tile(D_MODEL),
            pl.BlockSpec((1, N_MOD, D_MODEL), lambda i: (_mod_tile_index(i), 0, 0)),
            tile(A_Q + B_W), tile(C_WIDTH), tile(C_WIDTH), tile(C_WIDTH), tile(C_WIDTH), tile(GATE_COLS),
            _resident((A_Q, D_MODEL)), _resident((B_W, D_MODEL)), _resident((C_WIDTH, D_MODEL)),
            _resident((D_MODEL, D_MODEL)), _resident((C_WIDTH, C_WIDTH)),
            _resident((1, C_WIDTH)), _resident((1, C_WIDTH)), _resident((1, D_MODEL)), _resident((1, D_MODEL)),
        ],
        out_specs=tile(D_MODEL),
        out_shape=jax.ShapeDtypeStruct((N_TOK, D_MODEL), F32),
        compiler_params=_params("parallel"),
        name="merge",
    )(x, mods, oab, yf, yb, bonus, g, sg, pa, pb, pc, wo, avg,
      gn_g.reshape(1, C_WIDTH), gn_b.reshape(1, C_WIDTH), ln_g.reshape(1, D_MODEL), ln_b.reshape(1, D_MODEL))


def _block_diag2(a, b):
    za = jnp.zeros((a.shape[0], b.shape[1]), a.dtype)
    zb = jnp.zeros((b.shape[0], a.shape[1]), b.dtype)
    return jnp.concatenate([jnp.concatenate([a, za], axis=1), jnp.concatenate([zb, b], axis=1)], axis=0)


def kernel(x_prompt, x_sample, cache_attn_k, cache_attn_v, cache_na_k, cache_na_v, state_rwkv, c, c_ctx, w_ada, b_ada, ffn1_w_in, ffn1_w_out, ffn2_w_in, ffn2_w_out, w_in, w_shift, attn_sink, na_rpb, decay_w0, decay_up, iclr_a0, iclr_up, gate_up, k_k, k_a, r_k, gn_g, gn_b, proj_a, proj_b, proj_c, w_out, ln_g, ln_b):
    x = jnp.concatenate([x_prompt.reshape(N_CTX, D_MODEL), x_sample.reshape(N_LAT, D_MODEL)])
    c_rows = jnp.zeros((MOD_ROWS, D_MODEL), F32).at[:DEC_BATCH].set(c).at[DEC_BATCH].set(c_ctx)
    mods_all = _ada_mods(c_rows, w_ada, b_ada)
    cos_t, sin_t = _rope_tables()
    head_of = np.arange(C_WIDTH) // HEAD_DIM
    same_head = jnp.asarray(head_of[:, None] == head_of[None, :], F32)
    ones_bd = same_head
    avg_bd = same_head / HEAD_DIM
    zero_state = jnp.zeros((2, HEAD_DIM, HEAD_DIM, LANES), F32)

    new_ka, new_va, new_kb, new_vb, new_st = [], [], [], [], []
    for l in range(DEPTH):
        mods = mods_all[l]
        x = _ffn(x, mods, ffn1_w_in[l].astype(BF16), ffn1_w_out[l].astype(BF16), ln_g[l, 0], ln_b[l, 0], 0)
        za, zb, zc_raw, sg = _inproj(x, mods, w_in[l].astype(BF16))

        zac = za[:N_CTX].reshape(BATCH, SEQ, A_COLS)
        zbc = zb[:N_CTX].reshape(BATCH, SEQ, B_COLS)
        new_ka.append(zac[..., A_Q:A_Q + A_KV].reshape(BATCH, SEQ, A_KV_HEADS, HEAD_DIM))
        new_va.append(zac[..., A_Q + A_KV:].reshape(BATCH, SEQ, A_KV_HEADS, HEAD_DIM))
        new_kb.append(zbc[..., B_W:2 * B_W].reshape(BATCH, SEQ, B_HEADS, HEAD_DIM))
        new_vb.append(zbc[..., 2 * B_W:].reshape(BATCH, SEQ, B_HEADS, HEAD_DIM))

        o_ctx = _ctx_attn(za, zb, attn_sink[l])
        oa_lat = _win_attn(za, cache_attn_k[:, l].reshape(DEC_BATCH, PAST_LEN, A_KV),
                           cache_attn_v[:, l].reshape(DEC_BATCH, PAST_LEN, A_KV), attn_sink[l], cos_t, sin_t)
        ob_lat = _na_attn(zb, cache_na_k[:, l].reshape(DEC_BATCH, PAST_LEN, B_W),
                          cache_na_v[:, l].reshape(DEC_BATCH, PAST_LEN, B_W), na_rpb[l].reshape(-1))
        oab = jnp.concatenate([o_ctx, jnp.concatenate([oa_lat, ob_lat], axis=1)])

        r, v, nkk, w_f, w_b, b_f, b_b, k_f, k_b, bonus, g = _rwkv_prep(
            zc_raw, w_shift[l],
            _block_diag2(decay_up[l, 0], decay_up[l, 1]).astype(BF16),
            _block_diag2(iclr_up[l, 0], iclr_up[l, 1]).astype(BF16),
            gate_up[l].astype(BF16),
            decay_w0[l].reshape(1, 2 * C_WIDTH), iclr_a0[l].reshape(1, 2 * C_WIDTH),
            k_k[l].reshape(1, C_WIDTH), k_a[l].reshape(1, C_WIDTH), r_k[l].reshape(1, C_WIDTH), ones_bd)
        ops_ctx, ops_lat = zip(*(_to_scan(f, b) for f, b in ((w_f, w_b), (nkk, nkk), (b_f, b_b), (k_f, k_b), (r, r), (v, v))))
        s0_lat = state_rwkv[:, l].transpose(3, 4, 1, 0, 2).reshape(1, HEAD_DIM, HEAD_DIM, LANES)
        y_ctx, s_ctx = _scan(*ops_ctx, zero_state)
        y_lat, _ = _scan(*ops_lat, s0_lat)
        new_st.append(s_ctx.reshape(2, HEAD_DIM, HEAD_DIM, BATCH, C_HEADS).transpose(3, 0, 4, 1, 2))
        yf, yb = _from_scan(y_ctx, y_lat)

        x = _merge(x, mods, oab, yf, yb, bonus, g, sg, proj_a[l].astype(BF16), proj_b[l].astype(BF16),
                   proj_c[l].astype(BF16), w_out[l].astype(BF16), avg_bd, gn_g[l], gn_b[l], ln_g[l, 1], ln_b[l, 1])
        x = _ffn(x, mods, ffn2_w_in[l].astype(BF16), ffn2_w_out[l].astype(BF16), ln_g[l, 2], ln_b[l, 2], 6)

    y_p = x[:N_CTX].reshape(BATCH, SEQ, D_MODEL)
    y_s = x[N_CTX:].reshape(DEC_BATCH, DEC_SEQ, D_MODEL)
    return (y_p, y_s, jnp.stack(new_ka, axis=1), jnp.stack(new_va, axis=1), jnp.stack(new_kb, axis=1),
            jnp.stack(new_vb, axis=1), jnp.stack(new_st, axis=1))
```

```python
import collections
import functools

import jax
import jax.numpy as jnp
import numpy as np
from jax import lax
from jax.experimental import pallas as pl
from jax.experimental.pallas import tpu as pltpu

D_MODEL = 1024
BATCH = 16
SEQ = 256
DEPTH = 2
DEC_BATCH = 8
DEC_SEQ = 1024
PAST_LEN = 256
GRID_W = 64
GRID_ROWS = DEC_SEQ // GRID_W
HEAD_DIM = 64
SCALE = HEAD_DIM ** -0.5
A_HEADS = 8
A_KV_HEADS = 2
A_GROUP = A_HEADS // A_KV_HEADS
A_WINDOW = 128
A_BLOCK = 128
A_Q = A_HEADS * HEAD_DIM
A_KV = A_KV_HEADS * HEAD_DIM
A_COLS = A_Q + 2 * A_KV
B_HEADS = 8
B_W = B_HEADS * HEAD_DIM
B_COLS = 3 * B_W
NA_ROWS = 8
NA_COLS = 16
C_HEADS = 8
C_WIDTH = C_HEADS * HEAD_DIM
DECAY_RANK = 64
ICLR_RANK = 64
GATE_RANK = 128
C_COLS = 3 * C_WIDTH + 2 * DECAY_RANK + 2 * ICLR_RANK + GATE_RANK
GN_EPS = 64e-5
GATE_COLS = 3 * D_MODEL
IN_COLS = A_COLS + B_COLS + C_COLS + GATE_COLS
D_FF = 2816
N_MOD = 9
ROPE_BASE = 10000.0
LN_EPS = 1e-5
NEG_INF = -1e30
ALPHA = (2 * DEPTH) ** 0.25

LANES = 128
SUBLANES = 8
TM = 256
MOD_ROWS = 16
SCAN_TB = 16
SCAN_NB = LANES // (2 * C_HEADS)
VMEM_LIMIT = 56 * 1024 * 1024

F32 = jnp.float32
BF16 = jnp.bfloat16

Stream = collections.namedtuple("Stream", "name nb seq n_tok tiles tiles_per_seq mod_row")


def _make_stream(name, nb, seq, mod_row):
    return Stream(name, nb, seq, nb * seq, nb * seq // TM, seq // TM, mod_row)


CTX = _make_stream("ctx", BATCH, SEQ, lambda i: DEC_BATCH)
LAT = _make_stream("lat", DEC_BATCH, DEC_SEQ, lambda i: i // (DEC_SEQ // TM))


def _sigmoid(x):
    return 1.0 / (1.0 + jnp.exp(-x))


def _layer_norm(y, g, b):
    mu = jnp.mean(y, axis=-1, keepdims=True)
    d = y - mu
    var = jnp.mean(d * d, axis=-1, keepdims=True)
    return d * lax.rsqrt(var + LN_EPS) * g + b


def _dot(a, b):
    return jnp.dot(a, b, preferred_element_type=F32)


def _dot_nt(a, b):
    return lax.dot_general(a, b, (((1,), (1,)), ((), ())), preferred_element_type=F32)


def _dot_f32(a, b):
    return jnp.dot(a, b, preferred_element_type=F32, precision=lax.Precision.HIGHEST)


def _resident(shape):
    zeros = (0,) * len(shape)
    return pl.BlockSpec(shape, lambda *_: zeros, pipeline_mode=pl.Buffered(1))


def _tile(width, col=0):
    return pl.BlockSpec((TM, width), lambda i: (i, col))


def _mod_spec(st):
    return pl.BlockSpec((1, N_MOD, D_MODEL), lambda i: (st.mod_row(i), 0, 0))


def _params(*sem):
    return pltpu.CompilerParams(dimension_semantics=sem, vmem_limit_bytes=VMEM_LIMIT)


def _ada_kernel(c_ref, w_ref, b_ref, o_ref):
    c = c_ref[...]
    s = (c * _sigmoid(c)).astype(BF16)
    o_ref[0] = _dot(s, w_ref[0].astype(BF16)) + b_ref[0]


def _ada_mods(c_rows, w_ada, b_ada):
    out = pl.pallas_call(
        _ada_kernel,
        grid=(DEPTH, N_MOD),
        in_specs=[
            pl.BlockSpec((MOD_ROWS, D_MODEL), lambda l, j: (0, 0)),
            pl.BlockSpec((1, D_MODEL, D_MODEL), lambda l, j: (l, 0, j)),
            pl.BlockSpec((1, 1, D_MODEL), lambda l, j: (l, 0, j)),
        ],
        out_specs=pl.BlockSpec((1, MOD_ROWS, D_MODEL), lambda l, j: (l, 0, j)),
        out_shape=jax.ShapeDtypeStruct((DEPTH, MOD_ROWS, N_MOD * D_MODEL), F32),
        compiler_params=_params("parallel", "parallel"),
        name="ada_mods",
    )(c_rows, w_ada, b_ada.reshape(DEPTH, 1, N_MOD * D_MODEL))
    return out.reshape(DEPTH, MOD_ROWS, N_MOD, D_MODEL)


def _ffn_kernel(x_ref, mod_ref, w1_ref, w2_ref, g_ref, b_ref, o_ref, *, base):
    x = x_ref[...]
    shift = mod_ref[0, base:base + 1, :]
    scale = mod_ref[0, base + 1:base + 2, :]
    gate = mod_ref[0, base + 2:base + 3, :]
    h = (x * (1.0 + scale) + shift).astype(BF16)
    a = _dot(h, w1_ref[:, :D_FF])
    g = _dot(h, w1_ref[:, D_FF:])
    u = (a * _sigmoid(a) * g).astype(BF16)
    f = _dot(u, w2_ref[...])
    o_ref[...] = _layer_norm(ALPHA * x + 0.5 * gate * f, g_ref[...], b_ref[...])


def _ffn(st, x, mods, w1, w2, ln_g, ln_b, base):
    return pl.pallas_call(
        functools.partial(_ffn_kernel, base=base),
        grid=(st.tiles,),
        in_specs=[
            _tile(D_MODEL), _mod_spec(st),
            _resident((D_MODEL, 2 * D_FF)), _resident((D_FF, D_MODEL)),
            _resident((1, D_MODEL)), _resident((1, D_MODEL)),
        ],
        out_specs=_tile(D_MODEL),
        out_shape=jax.ShapeDtypeStruct((st.n_tok, D_MODEL), F32),
        compiler_params=_params("parallel"),
        name=f"ffn{base // 6 + 1}_{st.name}",
    )(x, mods, w1, w2, ln_g.reshape(1, D_MODEL), ln_b.reshape(1, D_MODEL))


_COL_A = 0
_COL_B = A_COLS
_COL_C = A_COLS + B_COLS
_COL_G = A_COLS + B_COLS + C_COLS


def _inproj_kernel(x_ref, mod_ref, w_ref, za_ref, zb_ref, zc_ref, sg_ref):
    x = x_ref[...]
    h = (x * (1.0 + mod_ref[0, 4:5, :]) + mod_ref[0, 3:4, :]).astype(BF16)
    za_ref[...] = _dot(h, w_ref[:, _COL_A:_COL_B])
    zb_ref[...] = _dot(h, w_ref[:, _COL_B:_COL_C])
    zc_ref[...] = _dot(h, w_ref[:, _COL_C:_COL_G])
    sg_ref[...] = _sigmoid(_dot(h, w_ref[:, _COL_G:]))


def _inproj(st, x, mods, w_in):
    widths = (A_COLS, B_COLS, C_COLS, GATE_COLS)
    return pl.pallas_call(
        _inproj_kernel,
        grid=(st.tiles,),
        in_specs=[_tile(D_MODEL), _mod_spec(st), _resident((D_MODEL, IN_COLS))],
        out_specs=[_tile(w) for w in widths],
        out_shape=[jax.ShapeDtypeStruct((st.n_tok, w), F32) for w in widths],
        compiler_params=_params("parallel"),
        name=f"inproj_{st.name}",
    )(x, mods, w_in)


def _softmax_pv(scores, values, sink=None):
    m = scores[0].max(axis=-1, keepdims=True)
    for s in scores[1:]:
        m = jnp.maximum(m, s.max(axis=-1, keepdims=True))
    if sink is not None:
        m = jnp.maximum(m, sink)
    den = jnp.exp(sink - m) if sink is not None else 0.0
    acc = None
    for s, v in zip(scores, values):
        p = jnp.exp(s - m)
        den = den + p.sum(axis=-1, keepdims=True)
        pv = _dot(p.astype(BF16), v)
        acc = pv if acc is None else acc + pv
    return acc / den


def _head(x, h):
    return x[:, h * HEAD_DIM:(h + 1) * HEAD_DIM]


def _ctx_attn_kernel(sink_ref, za_ref, zb_ref, o_ref):
    za = za_ref[...]
    zb = zb_ref[...]
    outs = []
    row = lax.broadcasted_iota(jnp.int32, (A_GROUP * SEQ, 1), 0) // SEQ
    for g in range(A_KV_HEADS):
        q4 = jnp.concatenate([_head(za, A_GROUP * g + j) for j in range(A_GROUP)], axis=0).astype(BF16)
        k = _head(za, A_HEADS + g).astype(BF16)
        v = _head(za, A_HEADS + A_KV_HEADS + g).astype(BF16)
        sink = jnp.zeros((A_GROUP * SEQ, 1), F32)
        for j in range(A_GROUP):
            sink = jnp.where(row == j, sink_ref[A_GROUP * g + j], sink)
        o4 = _softmax_pv([_dot_nt(q4, k) * SCALE], [v], sink)
        outs.extend(o4[j * SEQ:(j + 1) * SEQ] for j in range(A_GROUP))
    for h in range(B_HEADS):
        q = _head(zb, h).astype(BF16)
        k = _head(zb, B_HEADS + h).astype(BF16)
        v = _head(zb, 2 * B_HEADS + h).astype(BF16)
        outs.append(_softmax_pv([_dot_nt(q, k) * SCALE], [v]))
    o_ref[...] = jnp.concatenate(outs, axis=-1)


def _ctx_attn(za, zb, sink):
    return pl.pallas_call(
        _ctx_attn_kernel,
        grid=(BATCH,),
        in_specs=[
            pl.BlockSpec(memory_space=pltpu.SMEM),
            pl.BlockSpec((SEQ, A_COLS), lambda b: (b, 0)),
            pl.BlockSpec((SEQ, B_COLS), lambda b: (b, 0)),
        ],
        out_specs=pl.BlockSpec((SEQ, A_Q + B_W), lambda b: (b, 0)),
        out_shape=jax.ShapeDtypeStruct((CTX.n_tok, A_Q + B_W), F32),
        compiler_params=_params("parallel"),
        name="ctx_attn",
    )(sink, za, zb)


def _rope(x, cos, sin_signed):
    w = x.shape[-1]
    half = HEAD_DIM // 2
    lane = lax.broadcasted_iota(jnp.int32, x.shape, 1)
    partner = jnp.where(lane % HEAD_DIM < half, pltpu.roll(x, w - half, 1), pltpu.roll(x, half, 1))
    return x * cos + partner * sin_signed


def _win_attn_kernel(sink_ref, za_ref, ck_ref, cv_ref, cos_ref, sin_ref, o_ref):
    n = pl.program_id(1)
    nb = DEC_SEQ // A_BLOCK
    q0 = pl.multiple_of(n * A_BLOCK, A_BLOCK)
    q = _rope(za_ref[pl.ds(q0, A_BLOCK), :A_Q], cos_ref[pl.ds(q0, A_BLOCK), :], sin_ref[pl.ds(q0, A_BLOCK), :])
    ks, vs = [], []
    for j in range(3):
        kb = jnp.clip(n - 1 + j, 0, nb - 1)
        k0 = pl.multiple_of(kb * A_BLOCK, A_BLOCK)
        kblk = za_ref[pl.ds(k0, A_BLOCK), A_Q:A_Q + A_KV]
        ks.append(_rope(kblk, cos_ref[pl.ds(k0, A_BLOCK), :A_KV], sin_ref[pl.ds(k0, A_BLOCK), :A_KV]))
        vs.append(za_ref[pl.ds(k0, A_BLOCK), A_Q + A_KV:])
    kband = jnp.concatenate(ks, axis=0).astype(BF16)
    vband = jnp.concatenate(vs, axis=0).astype(BF16)
    ck = ck_ref[0].astype(BF16)
    cv = cv_ref[0].astype(BF16)
    m4 = A_GROUP * A_BLOCK
    qi = lax.broadcasted_iota(jnp.int32, (m4, 3 * A_BLOCK), 0) % A_BLOCK
    kj = lax.broadcasted_iota(jnp.int32, (m4, 3 * A_BLOCK), 1)
    rel = kj - A_BLOCK - qi
    kpos = (n - 1) * A_BLOCK + kj
    valid = (jnp.abs(rel) <= A_WINDOW) & (kpos >= 0) & (kpos < DEC_SEQ)
    row = lax.broadcasted_iota(jnp.int32, (m4, 1), 0) // A_BLOCK
    outs = []
    for g in range(A_KV_HEADS):
        q4 = jnp.concatenate([_head(q, A_GROUP * g + j) for j in range(A_GROUP)], axis=0).astype(BF16)
        s_loc = jnp.where(valid, _dot_nt(q4, _head(kband, g)) * SCALE, NEG_INF)
        s_ctx = _dot_nt(q4, _head(ck, g)) * SCALE
        sink = jnp.zeros((m4, 1), F32)
        for j in range(A_GROUP):
            sink = jnp.where(row == j, sink_ref[A_GROUP * g + j], sink)
        o4 = _softmax_pv([s_loc, s_ctx], [_head(vband, g), _head(cv, g)], sink)
        outs.extend(o4[j * A_BLOCK:(j + 1) * A_BLOCK] for j in range(A_GROUP))
    o_ref[...] = jnp.concatenate(outs, axis=-1)


def _rope_tables():
    t = np.arange(DEC_SEQ)
    n_freq = HEAD_DIM // 4
    inv = ROPE_BASE ** (-jnp.arange(n_freq, dtype=F32) / n_freq)
    rows = jnp.asarray(t // GRID_W, F32)
    cols = jnp.asarray(t % GRID_W, F32)
    ang = jnp.concatenate([rows[:, None] * inv, cols[:, None] * inv], axis=-1)
    cos, sin = jnp.cos(ang), jnp.sin(ang)
    cos_h = jnp.concatenate([cos, cos], axis=-1)
    sin_h = jnp.concatenate([-sin, sin], axis=-1)
    return jnp.tile(cos_h, (1, A_HEADS)), jnp.tile(sin_h, (1, A_HEADS))


def _win_attn(za, ck, cv, sink, cos_t, sin_t):
    nb = DEC_SEQ // A_BLOCK
    return pl.pallas_call(
        _win_attn_kernel,
        grid=(DEC_BATCH, nb),
        in_specs=[
            pl.BlockSpec(memory_space=pltpu.SMEM),
            pl.BlockSpec((DEC_SEQ, A_COLS), lambda b, n: (b, 0)),
            pl.BlockSpec((1, PAST_LEN, A_KV), lambda b, n: (b, 0, 0)),
            pl.BlockSpec((1, PAST_LEN, A_KV), lambda b, n: (b, 0, 0)),
            pl.BlockSpec((DEC_SEQ, A_Q), lambda b, n: (0, 0)),
            pl.BlockSpec((DEC_SEQ, A_Q), lambda b, n: (0, 0)),
        ],
        out_specs=pl.BlockSpec((A_BLOCK, A_Q), lambda b, n: (b * nb + n, 0)),
        out_shape=jax.ShapeDtypeStruct((LAT.n_tok, A_Q), F32),
        compiler_params=_params("parallel", "arbitrary"),
        name="win_attn",
    )(sink, za, ck, cv, cos_t, sin_t)


_NA_QBLK = 256
_RPB_I = 2 * NA_ROWS - 1
_RPB_J = 2 * NA_COLS - 1


def _na_build_bias(rpb_ref, head, bias_ref, slot):
    shape = (GRID_W, 2 * GRID_W)
    qc = lax.broadcasted_iota(jnp.int32, shape, 0)
    lane = lax.broadcasted_iota(jnp.int32, shape, 1)
    kc = lane % GRID_W
    left = lane < GRID_W
    cs = jnp.clip(qc - NA_COLS // 2, 0, GRID_W - NA_COLS)
    idx = jnp.where((kc >= cs) & (kc < cs + NA_COLS), jnp.clip(kc - qc + NA_COLS - 1, 0, _RPB_J - 1), -1)
    base = head * (_RPB_I * _RPB_J)
    tables = []
    for i in range(_RPB_I):
        t = jnp.full(shape, NEG_INF, F32)
        for j in range(_RPB_J):
            t = jnp.where(idx == j, rpb_ref[base + i * _RPB_J + j], t)
        tables.append(t)
    neg = jnp.full(shape, NEG_INF, F32)
    half = NA_ROWS // 2
    for qr in range(GRID_ROWS):
        rs = min(max(qr - half, 0), GRID_ROWS - NA_ROWS)
        for m in range(GRID_ROWS // 2):
            kr0, kr1 = 2 * m, 2 * m + 1
            v0 = rs <= kr0 < rs + NA_ROWS
            v1 = rs <= kr1 < rs + NA_ROWS
            i0 = kr0 - qr + NA_ROWS - 1
            if v0 and v1:
                piece = jnp.where(left, tables[i0], tables[i0 + 1])
            elif v0:
                piece = jnp.where(left, tables[i0], NEG_INF)
            elif v1:
                piece = jnp.where(left, NEG_INF, tables[i0 + 1])
            else:
                piece = neg
            bias_ref[slot, qr * GRID_W:(qr + 1) * GRID_W, m * 2 * GRID_W:(m + 1) * 2 * GRID_W] = piece


def _na_attn_kernel(rpb_ref, q_ref, k_ref, v_ref, ck_ref, cv_ref, o_ref, bias_ref):
    hp = pl.program_id(0)

    @pl.when(pl.program_id(1) == 0)
    def _():
        for half in range(2):
            _na_build_bias(rpb_ref, 2 * hp + half, bias_ref, half)

    k2 = k_ref[...].astype(BF16)
    v2 = v_ref[...].astype(BF16)
    ck2 = ck_ref[0].astype(BF16)
    cv2 = cv_ref[0].astype(BF16)
    for qb in range(DEC_SEQ // _NA_QBLK):
        rows = slice(qb * _NA_QBLK, (qb + 1) * _NA_QBLK)
        q2 = q_ref[rows, :].astype(BF16)
        outs = []
        for half in range(2):
            q = _head(q2, half)
            s_loc = _dot_nt(q, _head(k2, half)) * SCALE + bias_ref[half, rows, :]
            s_ctx = _dot_nt(q, _head(ck2, half)) * SCALE
            outs.append(_softmax_pv([s_loc, s_ctx], [_head(v2, half), _head(cv2, half)]))
        o_ref[rows, :] = jnp.concatenate(outs, axis=-1)


def _na_attn(zb, ck, cv, rpb_flat):
    pairs = B_HEADS // 2
    pw = 2 * HEAD_DIM
    return pl.pallas_call(
        _na_attn_kernel,
        grid=(pairs, DEC_BATCH),
        in_specs=[
            pl.BlockSpec(memory_space=pltpu.SMEM),
            pl.BlockSpec((DEC_SEQ, pw), lambda hp, b: (b, hp)),
            pl.BlockSpec((DEC_SEQ, pw), lambda hp, b: (b, pairs + hp)),
            pl.BlockSpec((DEC_SEQ, pw), lambda hp, b: (b, 2 * pairs + hp)),
            pl.BlockSpec((1, PAST_LEN, pw), lambda hp, b: (b, 0, hp)),
            pl.BlockSpec((1, PAST_LEN, pw), lambda hp, b: (b, 0, hp)),
        ],
        out_specs=pl.BlockSpec((DEC_SEQ, pw), lambda hp, b: (b, hp)),
        out_shape=jax.ShapeDtypeStruct((LAT.n_tok, B_W), F32),
        scratch_shapes=[pltpu.VMEM((2, DEC_SEQ, DEC_SEQ), F32)],
        compiler_params=_params("arbitrary", "arbitrary"),
        name="na_attn",
    )(rpb_flat, zb, zb, zb, ck, cv)


_C_R, _C_K, _C_V = 0, C_WIDTH, 2 * C_WIDTH
_C_WLO = 3 * C_WIDTH
_C_ALO = _C_WLO + 2 * DECAY_RANK
_C_GLO = _C_ALO + 2 * ICLR_RANK
_HALO = SUBLANES


def _rwkv_prep_kernel(z_ref, zp_ref, zn_ref, wsh_ref, dup_ref, aup_ref, gup_ref, w0_ref, a0_ref, kk_ref, ka_ref,
                      rk_ref, ones_ref,
                      r_ref, v_ref, nkk_ref, w_ref, b_ref, k_ref, bonus_ref, g_ref, *, tiles_per_seq):
    j = pl.program_id(0) % tiles_per_seq
    has_prev = jnp.where(j != 0, 1.0, 0.0)
    has_next = jnp.where(j != tiles_per_seq - 1, 1.0, 0.0)
    z = z_ref[...]
    row = lax.broadcasted_iota(jnp.int32, z.shape, 0)
    z_prev = jnp.where(row == 0, zp_ref[_HALO - 1:_HALO, :] * has_prev, pltpu.roll(z, 1, 0))
    z_next = jnp.where(row == TM - 1, zn_ref[0:1, :] * has_next, pltpu.roll(z, TM - 1, 0))
    zc = z_prev * wsh_ref[0:1, :] + z * wsh_ref[1:2, :] + z_next * wsh_ref[2:3, :]

    r = zc[:, _C_R:_C_R + C_WIDTH]
    k = zc[:, _C_K:_C_K + C_WIDTH]
    v = zc[:, _C_V:_C_V + C_WIDTH]
    wlo = zc[:, _C_WLO:_C_WLO + 2 * DECAY_RANK]
    alo = zc[:, _C_ALO:_C_ALO + 2 * ICLR_RANK]
    glo = zc[:, _C_GLO:_C_GLO + GATE_RANK]
    ones = ones_ref[...]

    g_ref[...] = _dot(_sigmoid(glo).astype(BF16), gup_ref[...])
    kk = k * kk_ref[...]
    norm = jnp.sqrt(_dot_f32(kk * kk, ones))
    kk = kk / jnp.maximum(norm, 1e-12)
    x = w0_ref[...] + _dot(jnp.tanh(wlo).astype(BF16), dup_ref[...])
    logw = -(jnp.maximum(-x, 0.0) + jnp.log(1.0 + jnp.exp(-jnp.abs(x)))) - 0.5
    decay = jnp.exp(-jnp.exp(logw))
    a = _sigmoid(a0_ref[...] + _dot(alo.astype(BF16), aup_ref[...]))
    ka = ka_ref[...]
    kd_sum = None
    for d in range(2):
        a_d = a[:, d * C_WIDTH:(d + 1) * C_WIDTH]
        k_d = k * (1.0 + (a_d - 1.0) * ka)
        w_ref[d] = decay[:, d * C_WIDTH:(d + 1) * C_WIDTH]
        b_ref[d] = kk * a_d
        k_ref[d] = k_d
        kd_sum = k_d if kd_sum is None else kd_sum + k_d
    bonus_ref[...] = _dot_f32(r * kd_sum * rk_ref[...], ones) * v
    r_ref[0] = r
    v_ref[0] = v
    nkk_ref[0] = -kk


def _rwkv_prep(st, zc_raw, w_shift, dec_up, iclr_up, gate_up, w0, a0, k_k, k_a, r_k, ones_bd):
    tiles_per_halo = TM // _HALO
    n_halo = st.n_tok // _HALO
    row = lambda n: _resident((1, n))
    stacked = lambda d: (pl.BlockSpec((d, TM, C_WIDTH), lambda i: (0, i, 0)),
                         jax.ShapeDtypeStruct((d, st.n_tok, C_WIDTH), F32))
    flat = (_tile(C_WIDTH), jax.ShapeDtypeStruct((st.n_tok, C_WIDTH), F32))
    outs = [stacked(1)] * 3 + [stacked(2)] * 3 + [flat] * 2
    return pl.pallas_call(
        functools.partial(_rwkv_prep_kernel, tiles_per_seq=st.tiles_per_seq),
        grid=(st.tiles,),
        in_specs=[
            _tile(C_COLS),
            pl.BlockSpec((_HALO, C_COLS), lambda i: (jnp.maximum(i * tiles_per_halo - 1, 0), 0)),
            pl.BlockSpec((_HALO, C_COLS), lambda i: (jnp.minimum((i + 1) * tiles_per_halo, n_halo - 1), 0)),
            _resident((3, C_COLS)),
            _resident((2 * DECAY_RANK, 2 * C_WIDTH)),
            _resident((2 * ICLR_RANK, 2 * C_WIDTH)),
            _resident((GATE_RANK, C_WIDTH)),
            row(2 * C_WIDTH), row(2 * C_WIDTH), row(C_WIDTH), row(C_WIDTH), row(C_WIDTH),
            _resident((C_WIDTH, C_WIDTH)),
        ],
        out_specs=[o[0] for o in outs],
        out_shape=[o[1] for o in outs],
        compiler_params=_params("parallel"),
        name=f"rwkv_prep_{st.name}",
    )(zc_raw, zc_raw, zc_raw, w_shift, dec_up, iclr_up, gate_up, w0, a0, k_k, k_a, r_k, ones_bd)


_PB = 2 * SCAN_NB


def _scan_kernel(wf, wb, af, ab, bf, bb, kf, kb, rf, rb, vf, vb, s0_ref, yf_ref, yb_ref, sT_ref,
                 state, ops_a, ops_b, y_a, y_b):
    tb = pl.program_id(1)

    @pl.when(tb == 0)
    def _():
        state[...] = s0_ref[0]
        y_a[...] = jnp.zeros_like(y_a)
        y_b[...] = jnp.zeros_like(y_b)

    left = lax.broadcasted_iota(jnp.int32, (_PB, LANES), 1) < HEAD_DIM

    def load_ops(i, ops):
        ir = SCAN_TB - 1 - i
        for p, (pf, pb, qf, qb) in enumerate(((wf, wb, af, ab), (bf, bb, kf, kb), (rf, rb, vf, vb))):
            xp = jnp.concatenate([pf[0, :, i, :], pb[0, :, ir, :]], axis=0)
            xq = jnp.concatenate([qf[0, :, i, :], qb[0, :, ir, :]], axis=0)
            blocks = []
            for j in range(C_HEADS // 2):
                pv = xp[:, j * LANES:(j + 1) * LANES]
                qv = xq[:, j * LANES:(j + 1) * LANES]
                blocks.append(jnp.where(left, pv, pltpu.roll(qv, HEAD_DIM, 1)))
                blocks.append(jnp.where(left, pltpu.roll(pv, HEAD_DIM, 1), qv))
            ops[p] = jnp.concatenate(blocks, axis=0).T

    def compute(ops, ybuf):
        w = ops[0, 0:HEAD_DIM]
        a = ops[0, HEAD_DIM:]
        b = ops[1, 0:HEAD_DIM]
        k = ops[1, HEAD_DIM:]
        r = ops[2, 0:HEAD_DIM]
        for vg in range(HEAD_DIM // SUBLANES):
            ys = []
            for j in range(SUBLANES):
                vi = vg * SUBLANES + j
                s = state[vi]
                u = jnp.sum(s * a, axis=0, keepdims=True)
                s = s * w + u * b + ops[2, HEAD_DIM + vi:HEAD_DIM + vi + 1] * k
                state[vi] = s
                ys.append(jnp.sum(s * r, axis=0, keepdims=True))
            ybuf[vg * SUBLANES:(vg + 1) * SUBLANES] = jnp.concatenate(ys, axis=0)

    def emit(i, ybuf):
        y = ybuf[...]
        t = jnp.concatenate([y, y], axis=0).T
        cols = [jnp.where(left, t[(2 * j) * _PB:(2 * j + 1) * _PB], t[(2 * j + 1) * _PB:(2 * j + 2) * _PB])
                for j in range(C_HEADS // 2)]
        out = jnp.concatenate(cols, axis=1)
        yf_ref[:, i, :] = out[:SCAN_NB]
        yb_ref[:, SCAN_TB - 1 - i, :] = out[SCAN_NB:]

    load_ops(0, ops_a)

    def body(m, carry):
        i0 = 2 * m
        load_ops(i0 + 1, ops_b)
        compute(ops_a, y_a)
        emit(jnp.maximum(i0 - 1, 0), y_b)
        load_ops(jnp.minimum(i0 + 2, SCAN_TB - 1), ops_a)
        compute(ops_b, y_b)
        emit(i0, y_a)
        return carry

    lax.fori_loop(0, SCAN_TB // 2, body, 0)
    emit(SCAN_TB - 1, y_b)

    @pl.when(tb == pl.num_programs(1) - 1)
    def _():
        sT_ref[0] = state[...]


def _scan(st, w, b, k, nkk, r, v, s0):
    groups = st.nb // SCAN_NB
    n_t = st.seq // SCAN_TB
    view = lambda x: x.reshape(x.shape[0], st.nb, st.seq, C_WIDTH)
    blk = (1, SCAN_NB, SCAN_TB, C_WIDTH)
    fwd = lambda d: pl.BlockSpec(blk, lambda g, t: (d, g, t, 0))
    bwd = lambda d: pl.BlockSpec(blk, lambda g, t: (d, g, n_t - 1 - t, 0))
    st_spec = pl.BlockSpec((1, HEAD_DIM, HEAD_DIM, LANES), lambda g, t: (g, 0, 0, 0))
    w, b, k, nkk, r, v = (view(x) for x in (w, b, k, nkk, r, v))
    tile = pltpu.VMEM((3, LANES, LANES), F32)
    ybuf = pltpu.VMEM((HEAD_DIM, LANES), F32)
    yblk = (SCAN_NB, SCAN_TB, C_WIDTH)
    y_shape = jax.ShapeDtypeStruct((st.nb, st.seq, C_WIDTH), F32)
    yf, yb, s_t = pl.pallas_call(
        _scan_kernel,
        grid=(groups, n_t),
        in_specs=[fwd(0), bwd(1), fwd(0), bwd(0), fwd(0), bwd(1), fwd(0), bwd(1), fwd(0), bwd(0), fwd(0), bwd(0),
                  st_spec],
        out_specs=[pl.BlockSpec(yblk, lambda g, t: (g, t, 0)), pl.BlockSpec(yblk, lambda g, t: (g, n_t - 1 - t, 0)),
                   st_spec],
        out_shape=[y_shape, y_shape, jax.ShapeDtypeStruct((groups, HEAD_DIM, HEAD_DIM, LANES), F32)],
        scratch_shapes=[pltpu.VMEM((HEAD_DIM, HEAD_DIM, LANES), F32), tile, tile, ybuf, ybuf],
        compiler_params=_params("parallel", "arbitrary"),
        name=f"rwkv_scan_{st.name}",
    )(w, w, nkk, nkk, b, b, k, k, r, r, v, v, s0)
    return yf.reshape(st.n_tok, C_WIDTH), yb.reshape(st.n_tok, C_WIDTH), s_t


def _merge_kernel(x_ref, mod_ref, oa_ref, ob_ref, yf_ref, yb_ref, bonus_ref, g_ref, sg_ref, pa_ref, pb_ref, pc_ref,
                  wo_ref, avg_ref, gng_ref, gnb_ref, lng_ref, lnb_ref, o_ref):
    y = yf_ref[...] + yb_ref[...]
    avg = avg_ref[...]
    mu = _dot_f32(y, avg)
    d = y - mu
    var = _dot_f32(d * d, avg)
    yn = d * lax.rsqrt(var + GN_EPS) * gng_ref[...] + gnb_ref[...]
    oc = ((yn + bonus_ref[...]) * g_ref[...]).astype(BF16)
    merged = (sg_ref[:, :D_MODEL] * _dot(oa_ref[...].astype(BF16), pa_ref[...])
              + sg_ref[:, D_MODEL:2 * D_MODEL] * _dot(ob_ref[...].astype(BF16), pb_ref[...])
              + sg_ref[:, 2 * D_MODEL:] * _dot(oc, pc_ref[...]))
    o = _dot(merged.astype(BF16), wo_ref[...])
    o_ref[...] = _layer_norm(ALPHA * x_ref[...] + mod_ref[0, 5:6, :] * o, lng_ref[...], lnb_ref[...])


def _merge(st, x, mods, oa, oa_col, ob, ob_col, yf, yb, bonus, g, sg, pa, pb, pc, wo, avg, gn_g, gn_b, ln_g, ln_b):
    return pl.pallas_call(
        _merge_kernel,
        grid=(st.tiles,),
        in_specs=[
            _tile(D_MODEL), _mod_spec(st), _tile(A_Q, oa_col), _tile(B_W, ob_col), _tile(C_WIDTH), _tile(C_WIDTH),
            _tile(C_WIDTH), _tile(C_WIDTH), _tile(GATE_COLS),
            _resident((A_Q, D_MODEL)), _resident((B_W, D_MODEL)), _resident((C_WIDTH, D_MODEL)),
            _resident((D_MODEL, D_MODEL)), _resident((C_WIDTH, C_WIDTH)),
            _resident((1, C_WIDTH)), _resident((1, C_WIDTH)), _resident((1, D_MODEL)), _resident((1, D_MODEL)),
        ],
        out_specs=_tile(D_MODEL),
        out_shape=jax.ShapeDtypeStruct((st.n_tok, D_MODEL), F32),
        compiler_params=_params("parallel"),
        name=f"merge_{st.name}",
    )(x, mods, oa, ob, yf, yb, bonus, g, sg,
      pa, pb, pc, wo, avg,
      gn_g.reshape(1, C_WIDTH), gn_b.reshape(1, C_WIDTH), ln_g.reshape(1, D_MODEL), ln_b.reshape(1, D_MODEL))


def _block_diag2(a, b):
    za = jnp.zeros((a.shape[0], b.shape[1]), a.dtype)
    zb = jnp.zeros((b.shape[0], a.shape[1]), b.dtype)
    return jnp.concatenate([jnp.concatenate([a, za], axis=1), jnp.concatenate([zb, b], axis=1)], axis=0)


def kernel(x_prompt, x_sample, cache_attn_k, cache_attn_v, cache_na_k, cache_na_v, state_rwkv, c, c_ctx, w_ada, b_ada, ffn1_w_in, ffn1_w_out, ffn2_w_in, ffn2_w_out, w_in, w_shift, attn_sink, na_rpb, decay_w0, decay_up, iclr_a0, iclr_up, gate_up, k_k, k_a, r_k, gn_g, gn_b, proj_a, proj_b, proj_c, w_out, ln_g, ln_b):
    xs = {CTX: x_prompt.reshape(CTX.n_tok, D_MODEL), LAT: x_sample.reshape(LAT.n_tok, D_MODEL)}
    c_rows = jnp.zeros((MOD_ROWS, D_MODEL), F32).at[:DEC_BATCH].set(c).at[DEC_BATCH].set(c_ctx)
    mods_all = _ada_mods(c_rows, w_ada, b_ada)
    cos_t, sin_t = _rope_tables()
    head_of = np.arange(C_WIDTH) // HEAD_DIM
    ones_bd = jnp.asarray(head_of[:, None] == head_of[None, :], F32)
    avg_bd = ones_bd / HEAD_DIM
    ctx_groups = BATCH // SCAN_NB
    zero_state = jnp.zeros((ctx_groups, HEAD_DIM, HEAD_DIM, LANES), F32)

    new_ka, new_va, new_kb, new_vb, new_st = [], [], [], [], []
    for l in range(DEPTH):
        mods = mods_all[l]
        bf = lambda w: w[l].astype(BF16)
        ffn1 = (bf(ffn1_w_in), bf(ffn1_w_out), ln_g[l, 0], ln_b[l, 0], 0)
        ffn2 = (bf(ffn2_w_in), bf(ffn2_w_out), ln_g[l, 2], ln_b[l, 2], 6)
        w_in_l = bf(w_in)
        prep_w = (w_shift[l],
                  _block_diag2(decay_up[l, 0], decay_up[l, 1]).astype(BF16),
                  _block_diag2(iclr_up[l, 0], iclr_up[l, 1]).astype(BF16),
                  bf(gate_up),
                  decay_w0[l].reshape(1, 2 * C_WIDTH), iclr_a0[l].reshape(1, 2 * C_WIDTH),
                  k_k[l].reshape(1, C_WIDTH), k_a[l].reshape(1, C_WIDTH), r_k[l].reshape(1, C_WIDTH), ones_bd)
        merge_w = (bf(proj_a), bf(proj_b), bf(proj_c), bf(w_out), avg_bd, gn_g[l], gn_b[l], ln_g[l, 1], ln_b[l, 1])
        s0_lat = state_rwkv[:, l].transpose(3, 4, 2, 1, 0).reshape(1, HEAD_DIM, HEAD_DIM, LANES)

        for st in (CTX, LAT):
            x = _ffn(st, xs[st], mods, *ffn1)
            za, zb, zc_raw, sg = _inproj(st, x, mods, w_in_l)
            if st is CTX:
                o_ab = _ctx_attn(za, zb, attn_sink[l])
                attn = (o_ab, 0, o_ab, 1)
                zac = za.reshape(BATCH, SEQ, A_COLS)
                zbc = zb.reshape(BATCH, SEQ, B_COLS)
                new_ka.append(zac[..., A_Q:A_Q + A_KV].reshape(BATCH, SEQ, A_KV_HEADS, HEAD_DIM))
                new_va.append(zac[..., A_Q + A_KV:].reshape(BATCH, SEQ, A_KV_HEADS, HEAD_DIM))
                new_kb.append(zbc[..., B_W:2 * B_W].reshape(BATCH, SEQ, B_HEADS, HEAD_DIM))
                new_vb.append(zbc[..., 2 * B_W:].reshape(BATCH, SEQ, B_HEADS, HEAD_DIM))
            else:
                oa = _win_attn(za, cache_attn_k[:, l].reshape(DEC_BATCH, PAST_LEN, A_KV),
                               cache_attn_v[:, l].reshape(DEC_BATCH, PAST_LEN, A_KV), attn_sink[l], cos_t, sin_t)
                ob = _na_attn(zb, cache_na_k[:, l].reshape(DEC_BATCH, PAST_LEN, B_W),
                              cache_na_v[:, l].reshape(DEC_BATCH, PAST_LEN, B_W), na_rpb[l].reshape(-1))
                attn = (oa, 0, ob, 0)
            r, v, nkk, w, b, k, bonus, g = _rwkv_prep(st, zc_raw, *prep_w)
            yf, yb, s_t = _scan(st, w, b, k, nkk, r, v, zero_state if st is CTX else s0_lat)
            if st is CTX:
                s_t = s_t.reshape(ctx_groups, HEAD_DIM, HEAD_DIM, C_HEADS, 2, SCAN_NB)
                new_st.append(s_t.transpose(0, 5, 4, 3, 1, 2).reshape(BATCH, 2, C_HEADS, HEAD_DIM, HEAD_DIM))
            x = _merge(st, x, mods, *attn, yf, yb, bonus, g, sg, *merge_w)
            xs[st] = _ffn(st, x, mods, *ffn2)

    y_p = xs[CTX].reshape(BATCH, SEQ, D_MODEL)
    y_s = xs[LAT].reshape(DEC_BATCH, DEC_SEQ, D_MODEL)
    return (y_p, y_s, jnp.stack(new_ka, axis=1), jnp.stack(new_va, axis=1), jnp.stack(new_kb, axis=1),
            jnp.stack(new_vb, axis=1), jnp.stack(new_st, axis=1))
```

```python
import collections
import functools

import jax
import jax.numpy as jnp
import numpy as np
from jax import lax
from jax.experimental import pallas as pl
from jax.experimental.pallas import tpu as pltpu

D_MODEL = 1024
BATCH = 16
SEQ = 256
DEPTH = 2
DEC_BATCH = 8
DEC_SEQ = 1024
PAST_LEN = 256
GRID_W = 64
GRID_ROWS = DEC_SEQ // GRID_W
HEAD_DIM = 64
SCALE = HEAD_DIM ** -0.5
A_HEADS = 8
A_KV_HEADS = 2
A_GROUP = A_HEADS // A_KV_HEADS
A_WINDOW = 128
A_BLOCK = 128
A_Q = A_HEADS * HEAD_DIM
A_KV = A_KV_HEADS * HEAD_DIM
A_COLS = A_Q + 2 * A_KV
B_HEADS = 8
B_W = B_HEADS * HEAD_DIM
B_COLS = 3 * B_W
NA_ROWS = 8
NA_COLS = 16
C_HEADS = 8
C_WIDTH = C_HEADS * HEAD_DIM
DECAY_RANK = 64
ICLR_RANK = 64
GATE_RANK = 128
C_COLS = 3 * C_WIDTH + 2 * DECAY_RANK + 2 * ICLR_RANK + GATE_RANK
GN_EPS = 64e-5
GATE_COLS = 3 * D_MODEL
IN_COLS = A_COLS + B_COLS + C_COLS + GATE_COLS
D_FF = 2816
N_MOD = 9
ROPE_BASE = 10000.0
LN_EPS = 1e-5
NEG_INF = -1e30
ALPHA = (2 * DEPTH) ** 0.25

LANES = 128
SUBLANES = 8
TM = 256
MOD_ROWS = 16
SCAN_TB = 32
SCAN_NB = LANES // (2 * C_HEADS)
VMEM_LIMIT = 56 * 1024 * 1024

F32 = jnp.float32
BF16 = jnp.bfloat16

Stream = collections.namedtuple("Stream", "name nb seq n_tok tiles tiles_per_seq mod_row")


def _make_stream(name, nb, seq, mod_row):
    return Stream(name, nb, seq, nb * seq, nb * seq // TM, seq // TM, mod_row)


CTX = _make_stream("ctx", BATCH, SEQ, lambda i: DEC_BATCH)
LAT = _make_stream("lat", DEC_BATCH, DEC_SEQ, lambda i: i // (DEC_SEQ // TM))


def _sigmoid(x):
    return 1.0 / (1.0 + jnp.exp(-x))


def _layer_norm(y, g, b):
    mu = jnp.mean(y, axis=-1, keepdims=True)
    d = y - mu
    var = jnp.mean(d * d, axis=-1, keepdims=True)
    return d * lax.rsqrt(var + LN_EPS) * g + b


def _dot(a, b):
    return jnp.dot(a, b, preferred_element_type=F32)


def _dot_nt(a, b):
    return lax.dot_general(a, b, (((1,), (1,)), ((), ())), preferred_element_type=F32)


def _dot_f32(a, b):
    return jnp.dot(a, b, preferred_element_type=F32, precision=lax.Precision.HIGHEST)


def _resident(shape):
    zeros = (0,) * len(shape)
    return pl.BlockSpec(shape, lambda *_: zeros, pipeline_mode=pl.Buffered(1))


def _tile(width, col=0):
    return pl.BlockSpec((TM, width), lambda i: (i, col))


def _mod_spec(st):
    return pl.BlockSpec((1, N_MOD, D_MODEL), lambda i: (st.mod_row(i), 0, 0))


def _params(*sem):
    return pltpu.CompilerParams(dimension_semantics=sem, vmem_limit_bytes=VMEM_LIMIT)


def _ada_kernel(c_ref, w_ref, b_ref, o_ref):
    c = c_ref[...]
    s = (c * _sigmoid(c)).astype(BF16)
    o_ref[0] = _dot(s, w_ref[0].astype(BF16)) + b_ref[0]


def _ada_mods(c_rows, w_ada, b_ada):
    out = pl.pallas_call(
        _ada_kernel,
        grid=(DEPTH, N_MOD),
        in_specs=[
            pl.BlockSpec((MOD_ROWS, D_MODEL), lambda l, j: (0, 0)),
            pl.BlockSpec((1, D_MODEL, D_MODEL), lambda l, j: (l, 0, j)),
            pl.BlockSpec((1, 1, D_MODEL), lambda l, j: (l, 0, j)),
        ],
        out_specs=pl.BlockSpec((1, MOD_ROWS, D_MODEL), lambda l, j: (l, 0, j)),
        out_shape=jax.ShapeDtypeStruct((DEPTH, MOD_ROWS, N_MOD * D_MODEL), F32),
        compiler_params=_params("parallel", "parallel"),
        name="ada_mods",
    )(c_rows, w_ada, b_ada.reshape(DEPTH, 1, N_MOD * D_MODEL))
    return out.reshape(DEPTH, MOD_ROWS, N_MOD, D_MODEL)


def _ffn_kernel(x_ref, mod_ref, w1_ref, w2_ref, g_ref, b_ref, o_ref, *, base):
    x = x_ref[...]
    shift = mod_ref[0, base:base + 1, :]
    scale = mod_ref[0, base + 1:base + 2, :]
    gate = mod_ref[0, base + 2:base + 3, :]
    h = (x * (1.0 + scale) + shift).astype(BF16)
    a = _dot(h, w1_ref[:, :D_FF])
    g = _dot(h, w1_ref[:, D_FF:])
    u = (a * _sigmoid(a) * g).astype(BF16)
    f = _dot(u, w2_ref[...])
    o_ref[...] = _layer_norm(ALPHA * x + 0.5 * gate * f, g_ref[...], b_ref[...])


def _ffn(st, x, mods, w1, w2, ln_g, ln_b, base):
    return pl.pallas_call(
        functools.partial(_ffn_kernel, base=base),
        grid=(st.tiles,),
        in_specs=[
            _tile(D_MODEL), _mod_spec(st),
            _resident((D_MODEL, 2 * D_FF)), _resident((D_FF, D_MODEL)),
            _resident((1, D_MODEL)), _resident((1, D_MODEL)),
        ],
        out_specs=_tile(D_MODEL),
        out_shape=jax.ShapeDtypeStruct((st.n_tok, D_MODEL), F32),
        compiler_params=_params("parallel"),
        name=f"ffn{base // 6 + 1}_{st.name}",
    )(x, mods, w1, w2, ln_g.reshape(1, D_MODEL), ln_b.reshape(1, D_MODEL))


_COL_A = 0
_COL_B = A_COLS
_COL_C = A_COLS + B_COLS
_COL_G = A_COLS + B_COLS + C_COLS


def _inproj_kernel(x_ref, mod_ref, w_ref, za_ref, zb_ref, zc_ref, sg_ref):
    x = x_ref[...]
    h = (x * (1.0 + mod_ref[0, 4:5, :]) + mod_ref[0, 3:4, :]).astype(BF16)
    za_ref[...] = _dot(h, w_ref[:, _COL_A:_COL_B])
    zb_ref[...] = _dot(h, w_ref[:, _COL_B:_COL_C])
    zc_ref[...] = _dot(h, w_ref[:, _COL_C:_COL_G])
    sg_ref[...] = _sigmoid(_dot(h, w_ref[:, _COL_G:]))


def _inproj(st, x, mods, w_in):
    widths = (A_COLS, B_COLS, C_COLS, GATE_COLS)
    return pl.pallas_call(
        _inproj_kernel,
        grid=(st.tiles,),
        in_specs=[_tile(D_MODEL), _mod_spec(st), _resident((D_MODEL, IN_COLS))],
        out_specs=[_tile(w) for w in widths],
        out_shape=[jax.ShapeDtypeStruct((st.n_tok, w), F32) for w in widths],
        compiler_params=_params("parallel"),
        name=f"inproj_{st.name}",
    )(x, mods, w_in)


def _softmax_pv(scores, values, sink=None):
    m = scores[0].max(axis=-1, keepdims=True)
    for s in scores[1:]:
        m = jnp.maximum(m, s.max(axis=-1, keepdims=True))
    if sink is not None:
        m = jnp.maximum(m, sink)
    den = jnp.exp(sink - m) if sink is not None else 0.0
    acc = None
    for s, v in zip(scores, values):
        p = jnp.exp(s - m)
        den = den + p.sum(axis=-1, keepdims=True)
        pv = _dot(p.astype(BF16), v)
        acc = pv if acc is None else acc + pv
    return acc / den


def _head(x, h):
    return x[:, h * HEAD_DIM:(h + 1) * HEAD_DIM]


def _ctx_attn_kernel(sink_ref, za_ref, zb_ref, o_ref):
    za = za_ref[...]
    zb = zb_ref[...]
    outs = []
    row = lax.broadcasted_iota(jnp.int32, (A_GROUP * SEQ, 1), 0) // SEQ
    for g in range(A_KV_HEADS):
        q4 = jnp.concatenate([_head(za, A_GROUP * g + j) for j in range(A_GROUP)], axis=0).astype(BF16)
        k = _head(za, A_HEADS + g).astype(BF16)
        v = _head(za, A_HEADS + A_KV_HEADS + g).astype(BF16)
        sink = jnp.zeros((A_GROUP * SEQ, 1), F32)
        for j in range(A_GROUP):
            sink = jnp.where(row == j, sink_ref[A_GROUP * g + j], sink)
        o4 = _softmax_pv([_dot_nt(q4, k) * SCALE], [v], sink)
        outs.extend(o4[j * SEQ:(j + 1) * SEQ] for j in range(A_GROUP))
    for h in range(B_HEADS):
        q = _head(zb, h).astype(BF16)
        k = _head(zb, B_HEADS + h).astype(BF16)
        v = _head(zb, 2 * B_HEADS + h).astype(BF16)
        outs.append(_softmax_pv([_dot_nt(q, k) * SCALE], [v]))
    o_ref[...] = jnp.concatenate(outs, axis=-1)


def _ctx_attn(za, zb, sink):
    return pl.pallas_call(
        _ctx_attn_kernel,
        grid=(BATCH,),
        in_specs=[
            pl.BlockSpec(memory_space=pltpu.SMEM),
            pl.BlockSpec((SEQ, A_COLS), lambda b: (b, 0)),
            pl.BlockSpec((SEQ, B_COLS), lambda b: (b, 0)),
        ],
        out_specs=pl.BlockSpec((SEQ, A_Q + B_W), lambda b: (b, 0)),
        out_shape=jax.ShapeDtypeStruct((CTX.n_tok, A_Q + B_W), F32),
        compiler_params=_params("parallel"),
        name="ctx_attn",
    )(sink, za, zb)


def _rope(x, cos, sin_signed):
    w = x.shape[-1]
    half = HEAD_DIM // 2
    lane = lax.broadcasted_iota(jnp.int32, x.shape, 1)
    partner = jnp.where(lane % HEAD_DIM < half, pltpu.roll(x, w - half, 1), pltpu.roll(x, half, 1))
    return x * cos + partner * sin_signed


def _win_attn_kernel(sink_ref, za_ref, ck_ref, cv_ref, cos_ref, sin_ref, o_ref):
    n = pl.program_id(1)
    nb = DEC_SEQ // A_BLOCK
    q0 = pl.multiple_of(n * A_BLOCK, A_BLOCK)
    q = _rope(za_ref[pl.ds(q0, A_BLOCK), :A_Q], cos_ref[pl.ds(q0, A_BLOCK), :], sin_ref[pl.ds(q0, A_BLOCK), :])
    ks, vs = [], []
    for j in range(3):
        kb = jnp.clip(n - 1 + j, 0, nb - 1)
        k0 = pl.multiple_of(kb * A_BLOCK, A_BLOCK)
        kblk = za_ref[pl.ds(k0, A_BLOCK), A_Q:A_Q + A_KV]
        ks.append(_rope(kblk, cos_ref[pl.ds(k0, A_BLOCK), :A_KV], sin_ref[pl.ds(k0, A_BLOCK), :A_KV]))
        vs.append(za_ref[pl.ds(k0, A_BLOCK), A_Q + A_KV:])
    kband = jnp.concatenate(ks, axis=0).astype(BF16)
    vband = jnp.concatenate(vs, axis=0).astype(BF16)
    ck = ck_ref[0].astype(BF16)
    cv = cv_ref[0].astype(BF16)
    m4 = A_GROUP * A_BLOCK
    qi = lax.broadcasted_iota(jnp.int32, (m4, 3 * A_BLOCK), 0) % A_BLOCK
    kj = lax.broadcasted_iota(jnp.int32, (m4, 3 * A_BLOCK), 1)
    rel = kj - A_BLOCK - qi
    kpos = (n - 1) * A_BLOCK + kj
    valid = (jnp.abs(rel) <= A_WINDOW) & (kpos >= 0) & (kpos < DEC_SEQ)
    row = lax.broadcasted_iota(jnp.int32, (m4, 1), 0) // A_BLOCK
    outs = []
    for g in range(A_KV_HEADS):
        q4 = jnp.concatenate([_head(q, A_GROUP * g + j) for j in range(A_GROUP)], axis=0).astype(BF16)
        s_loc = jnp.where(valid, _dot_nt(q4, _head(kband, g)) * SCALE, NEG_INF)
        s_ctx = _dot_nt(q4, _head(ck, g)) * SCALE
        sink = jnp.zeros((m4, 1), F32)
        for j in range(A_GROUP):
            sink = jnp.where(row == j, sink_ref[A_GROUP * g + j], sink)
        o4 = _softmax_pv([s_loc, s_ctx], [_head(vband, g), _head(cv, g)], sink)
        outs.extend(o4[j * A_BLOCK:(j + 1) * A_BLOCK] for j in range(A_GROUP))
    o_ref[...] = jnp.concatenate(outs, axis=-1)


def _rope_tables():
    t = np.arange(DEC_SEQ)
    n_freq = HEAD_DIM // 4
    inv = ROPE_BASE ** (-jnp.arange(n_freq, dtype=F32) / n_freq)
    rows = jnp.asarray(t // GRID_W, F32)
    cols = jnp.asarray(t % GRID_W, F32)
    ang = jnp.concatenate([rows[:, None] * inv, cols[:, None] * inv], axis=-1)
    cos, sin = jnp.cos(ang), jnp.sin(ang)
    cos_h = jnp.concatenate([cos, cos], axis=-1)
    sin_h = jnp.concatenate([-sin, sin], axis=-1)
    return jnp.tile(cos_h, (1, A_HEADS)), jnp.tile(sin_h, (1, A_HEADS))


def _win_attn(za, ck, cv, sink, cos_t, sin_t):
    nb = DEC_SEQ // A_BLOCK
    return pl.pallas_call(
        _win_attn_kernel,
        grid=(DEC_BATCH, nb),
        in_specs=[
            pl.BlockSpec(memory_space=pltpu.SMEM),
            pl.BlockSpec((DEC_SEQ, A_COLS), lambda b, n: (b, 0)),
            pl.BlockSpec((1, PAST_LEN, A_KV), lambda b, n: (b, 0, 0)),
            pl.BlockSpec((1, PAST_LEN, A_KV), lambda b, n: (b, 0, 0)),
            pl.BlockSpec((DEC_SEQ, A_Q), lambda b, n: (0, 0)),
            pl.BlockSpec((DEC_SEQ, A_Q), lambda b, n: (0, 0)),
        ],
        out_specs=pl.BlockSpec((A_BLOCK, A_Q), lambda b, n: (b * nb + n, 0)),
        out_shape=jax.ShapeDtypeStruct((LAT.n_tok, A_Q), F32),
        compiler_params=_params("parallel", "arbitrary"),
        name="win_attn",
    )(sink, za, ck, cv, cos_t, sin_t)


_NA_QBLK = 256
_RPB_I = 2 * NA_ROWS - 1
_RPB_J = 2 * NA_COLS - 1


def _na_build_bias(rpb_ref, head, bias_ref, slot):
    shape = (GRID_W, 2 * GRID_W)
    qc = lax.broadcasted_iota(jnp.int32, shape, 0)
    lane = lax.broadcasted_iota(jnp.int32, shape, 1)
    kc = lane % GRID_W
    left = lane < GRID_W
    cs = jnp.clip(qc - NA_COLS // 2, 0, GRID_W - NA_COLS)
    idx = jnp.where((kc >= cs) & (kc < cs + NA_COLS), jnp.clip(kc - qc + NA_COLS - 1, 0, _RPB_J - 1), -1)
    base = head * (_RPB_I * _RPB_J)
    tables = []
    for i in range(_RPB_I):
        t = jnp.full(shape, NEG_INF, F32)
        for j in range(_RPB_J):
            t = jnp.where(idx == j, rpb_ref[base + i * _RPB_J + j], t)
        tables.append(t)
    neg = jnp.full(shape, NEG_INF, F32)
    half = NA_ROWS // 2
    for qr in range(GRID_ROWS):
        rs = min(max(qr - half, 0), GRID_ROWS - NA_ROWS)
        for m in range(GRID_ROWS // 2):
            kr0, kr1 = 2 * m, 2 * m + 1
            v0 = rs <= kr0 < rs + NA_ROWS
            v1 = rs <= kr1 < rs + NA_ROWS
            i0 = kr0 - qr + NA_ROWS - 1
            if v0 and v1:
                piece = jnp.where(left, tables[i0], tables[i0 + 1])
            elif v0:
                piece = jnp.where(left, tables[i0], NEG_INF)
            elif v1:
                piece = jnp.where(left, NEG_INF, tables[i0 + 1])
            else:
                piece = neg
            bias_ref[slot, qr * GRID_W:(qr + 1) * GRID_W, m * 2 * GRID_W:(m + 1) * 2 * GRID_W] = piece


def _na_attn_kernel(rpb_ref, q_ref, k_ref, v_ref, ck_ref, cv_ref, o_ref, bias_ref):
    hp = pl.program_id(0)

    @pl.when(pl.program_id(1) == 0)
    def _():
        for half in range(2):
            _na_build_bias(rpb_ref, 2 * hp + half, bias_ref, half)

    ck2 = ck_ref[0].astype(BF16)
    cv2 = cv_ref[0].astype(BF16)
    rows_per_blk = _NA_QBLK // GRID_W
    for qb in range(DEC_SEQ // _NA_QBLK):
        rows = slice(qb * _NA_QBLK, (qb + 1) * _NA_QBLK)
        r_lo = min(max(qb * rows_per_blk - NA_ROWS // 2, 0), GRID_ROWS - NA_ROWS)
        r_hi = min(max((qb + 1) * rows_per_blk - 1 - NA_ROWS // 2, 0), GRID_ROWS - NA_ROWS) + NA_ROWS
        keys = slice(r_lo // 2 * 2 * GRID_W, -(-r_hi // 2) * 2 * GRID_W)
        k2 = k_ref[keys, :].astype(BF16)
        v2 = v_ref[keys, :].astype(BF16)
        q2 = q_ref[rows, :].astype(BF16)
        outs = []
        for half in range(2):
            q = _head(q2, half)
            s_loc = _dot_nt(q, _head(k2, half)) * SCALE + bias_ref[half, rows, keys]
            s_ctx = _dot_nt(q, _head(ck2, half)) * SCALE
            outs.append(_softmax_pv([s_loc, s_ctx], [_head(v2, half), _head(cv2, half)]))
        o_ref[rows, :] = jnp.concatenate(outs, axis=-1)


def _na_attn(zb, ck, cv, rpb_flat):
    pairs = B_HEADS // 2
    pw = 2 * HEAD_DIM
    return pl.pallas_call(
        _na_attn_kernel,
        grid=(pairs, DEC_BATCH),
        in_specs=[
            pl.BlockSpec(memory_space=pltpu.SMEM),
            pl.BlockSpec((DEC_SEQ, pw), lambda hp, b: (b, hp)),
            pl.BlockSpec((DEC_SEQ, pw), lambda hp, b: (b, pairs + hp)),
            pl.BlockSpec((DEC_SEQ, pw), lambda hp, b: (b, 2 * pairs + hp)),
            pl.BlockSpec((1, PAST_LEN, pw), lambda hp, b: (b, 0, hp)),
            pl.BlockSpec((1, PAST_LEN, pw), lambda hp, b: (b, 0, hp)),
        ],
        out_specs=pl.BlockSpec((DEC_SEQ, pw), lambda hp, b: (b, hp)),
        out_shape=jax.ShapeDtypeStruct((LAT.n_tok, B_W), F32),
        scratch_shapes=[pltpu.VMEM((2, DEC_SEQ, DEC_SEQ), F32)],
        compiler_params=_params("arbitrary", "arbitrary"),
        name="na_attn",
    )(rpb_flat, zb, zb, zb, ck, cv)


_C_R, _C_K, _C_V = 0, C_WIDTH, 2 * C_WIDTH
_C_WLO = 3 * C_WIDTH
_C_ALO = _C_WLO + 2 * DECAY_RANK
_C_GLO = _C_ALO + 2 * ICLR_RANK
_HALO = SUBLANES


def _rwkv_prep_kernel(z_ref, zp_ref, zn_ref, wsh_ref, dup_ref, aup_ref, gup_ref, w0_ref, a0_ref, kk_ref, ka_ref,
                      rk_ref, ones_ref,
                      r_ref, v_ref, nkk_ref, w_ref, b_ref, k_ref, bonus_ref, g_ref, *, tiles_per_seq):
    j = pl.program_id(0) % tiles_per_seq
    has_prev = jnp.where(j != 0, 1.0, 0.0)
    has_next = jnp.where(j != tiles_per_seq - 1, 1.0, 0.0)
    z = z_ref[...]
    row = lax.broadcasted_iota(jnp.int32, z.shape, 0)
    z_prev = jnp.where(row == 0, zp_ref[_HALO - 1:_HALO, :] * has_prev, pltpu.roll(z, 1, 0))
    z_next = jnp.where(row == TM - 1, zn_ref[0:1, :] * has_next, pltpu.roll(z, TM - 1, 0))
    zc = z_prev * wsh_ref[0:1, :] + z * wsh_ref[1:2, :] + z_next * wsh_ref[2:3, :]

    r = zc[:, _C_R:_C_R + C_WIDTH]
    k = zc[:, _C_K:_C_K + C_WIDTH]
    v = zc[:, _C_V:_C_V + C_WIDTH]
    wlo = zc[:, _C_WLO:_C_WLO + 2 * DECAY_RANK]
    alo = zc[:, _C_ALO:_C_ALO + 2 * ICLR_RANK]
    glo = zc[:, _C_GLO:_C_GLO + GATE_RANK]
    ones = ones_ref[...]

    g_ref[...] = _dot(_sigmoid(glo).astype(BF16), gup_ref[...])
    kk = k * kk_ref[...]
    norm = jnp.sqrt(_dot_f32(kk * kk, ones))
    kk = kk / jnp.maximum(norm, 1e-12)
    x = w0_ref[...] + _dot(jnp.tanh(wlo).astype(BF16), dup_ref[...])
    logw = -(jnp.maximum(-x, 0.0) + jnp.log(1.0 + jnp.exp(-jnp.abs(x)))) - 0.5
    decay = jnp.exp(-jnp.exp(logw))
    a = _sigmoid(a0_ref[...] + _dot(alo.astype(BF16), aup_ref[...]))
    ka = ka_ref[...]
    kd_sum = None
    for d in range(2):
        a_d = a[:, d * C_WIDTH:(d + 1) * C_WIDTH]
        k_d = k * (1.0 + (a_d - 1.0) * ka)
        w_ref[d] = decay[:, d * C_WIDTH:(d + 1) * C_WIDTH]
        b_ref[d] = kk * a_d
        k_ref[d] = k_d
        kd_sum = k_d if kd_sum is None else kd_sum + k_d
    bonus_ref[...] = _dot_f32(r * kd_sum * rk_ref[...], ones) * v
    r_ref[0] = r
    v_ref[0] = v
    nkk_ref[0] = -kk


def _rwkv_prep(st, zc_raw, w_shift, dec_up, iclr_up, gate_up, w0, a0, k_k, k_a, r_k, ones_bd):
    tiles_per_halo = TM // _HALO
    n_halo = st.n_tok // _HALO
    row = lambda n: _resident((1, n))
    stacked = lambda d: (pl.BlockSpec((d, TM, C_WIDTH), lambda i: (0, i, 0)),
                         jax.ShapeDtypeStruct((d, st.n_tok, C_WIDTH), F32))
    flat = (_tile(C_WIDTH), jax.ShapeDtypeStruct((st.n_tok, C_WIDTH), F32))
    outs = [stacked(1)] * 3 + [stacked(2)] * 3 + [flat] * 2
    return pl.pallas_call(
        functools.partial(_rwkv_prep_kernel, tiles_per_seq=st.tiles_per_seq),
        grid=(st.tiles,),
        in_specs=[
            _tile(C_COLS),
            pl.BlockSpec((_HALO, C_COLS), lambda i: (jnp.maximum(i * tiles_per_halo - 1, 0), 0)),
            pl.BlockSpec((_HALO, C_COLS), lambda i: (jnp.minimum((i + 1) * tiles_per_halo, n_halo - 1), 0)),
            _resident((3, C_COLS)),
            _resident((2 * DECAY_RANK, 2 * C_WIDTH)),
            _resident((2 * ICLR_RANK, 2 * C_WIDTH)),
            _resident((GATE_RANK, C_WIDTH)),
            row(2 * C_WIDTH), row(2 * C_WIDTH), row(C_WIDTH), row(C_WIDTH), row(C_WIDTH),
            _resident((C_WIDTH, C_WIDTH)),
        ],
        out_specs=[o[0] for o in outs],
        out_shape=[o[1] for o in outs],
        compiler_params=_params("parallel"),
        name=f"rwkv_prep_{st.name}",
    )(zc_raw, zc_raw, zc_raw, w_shift, dec_up, iclr_up, gate_up, w0, a0, k_k, k_a, r_k, ones_bd)


_PB = 2 * SCAN_NB


def _scan_kernel(wf, wb, af, ab, bf, bb, kf, kb, rf, rb, vf, vb, s0_ref, yf_ref, yb_ref, state,
                 ops_a, ops_b, red, wcum, y_a, y_b):
    @pl.when(pl.program_id(1) == 0)
    def _():
        state[...] = s0_ref[...]
        y_b[...] = jnp.zeros_like(y_b)

    wcum[...] = jnp.ones_like(wcum)
    left = lax.broadcasted_iota(jnp.int32, (_PB, LANES), 1) < HEAD_DIM
    low = lax.broadcasted_iota(jnp.int32, (SUBLANES, LANES), 0) < SUBLANES // 2

    def load_ops(i, ops):
        ir = SCAN_TB - 1 - i
        for p, (pf, pb, qf, qb) in enumerate(((wf, wb, af, ab), (bf, bb, kf, kb), (rf, rb, vf, vb))):
            xp = jnp.concatenate([pf[0, :, i, :], pb[0, :, ir, :]], axis=0)
            xq = jnp.concatenate([qf[0, :, i, :], qb[0, :, ir, :]], axis=0)
            blocks = []
            for j in range(C_HEADS // 2):
                pv = xp[:, j * LANES:(j + 1) * LANES]
                qv = xq[:, j * LANES:(j + 1) * LANES]
                blocks.append(jnp.where(left, pv, pltpu.roll(qv, HEAD_DIM, 1)))
                blocks.append(jnp.where(left, pltpu.roll(pv, HEAD_DIM, 1), qv))
            ops[p] = jnp.concatenate(blocks, axis=0).T

    def fold(x):
        return x.reshape(HEAD_DIM // SUBLANES, SUBLANES, LANES).sum(axis=0)

    def compute(ops, ybuf):
        w_prev = wcum[...]
        w_now = w_prev * ops[0, 0:HEAD_DIM]
        wcum[...] = w_now
        w_inv = 1.0 / w_now
        a = ops[0, HEAD_DIM:] * w_prev
        b = ops[1, 0:HEAD_DIM]
        k = ops[1, HEAD_DIM:]
        r = ops[2, 0:HEAD_DIM]
        br = jnp.sum(b * r, axis=0, keepdims=True)
        kr = jnp.sum(k * r, axis=0, keepdims=True)
        b = b * w_inv
        k = k * w_inv
        r = r * w_now
        for vi in range(HEAD_DIM):
            z = state[0, vi]
            p1 = fold(z * a)
            p2 = fold(z * r)
            t = jnp.where(low, p1, p2) + pltpu.roll(jnp.where(low, p2, p1), SUBLANES // 2, 0)
            t = t + pltpu.roll(t, SUBLANES - 2, 0)
            red[vi] = t + pltpu.roll(t, SUBLANES - 1, 0)
        for vi in range(HEAD_DIM):
            state[0, vi] = state[0, vi] + red[vi, 0:1] * b + ops[2, HEAD_DIM + vi:HEAD_DIM + vi + 1] * k
        ybuf[...] = red[:, SUBLANES // 2, :] + red[:, 0, :] * br + ops[2, HEAD_DIM:] * kr

    def emit(i, ybuf):
        y = ybuf[...]
        t = jnp.concatenate([y, y], axis=0).T
        cols = [jnp.where(left, t[(2 * j) * _PB:(2 * j + 1) * _PB], t[(2 * j + 1) * _PB:(2 * j + 2) * _PB])
                for j in range(C_HEADS // 2)]
        out = jnp.concatenate(cols, axis=1)
        yf_ref[:, i, :] = out[:SCAN_NB]
        yb_ref[:, SCAN_TB - 1 - i, :] = out[SCAN_NB:]

    load_ops(0, ops_a)

    def body(m, carry):
        i0 = 2 * m
        load_ops(i0 + 1, ops_b)
        compute(ops_a, y_a)
        emit(jnp.maximum(i0 - 1, 0), y_b)
        load_ops(jnp.minimum(i0 + 2, SCAN_TB - 1), ops_a)
        compute(ops_b, y_b)
        emit(i0, y_a)
        return carry

    lax.fori_loop(0, SCAN_TB // 2, body, 0)
    emit(SCAN_TB - 1, y_b)
    w_end = wcum[...]
    for vi in range(HEAD_DIM):
        state[0, vi] = state[0, vi] * w_end


def _scan(st, w, b, k, nkk, r, v, s0):
    groups = st.nb // SCAN_NB
    n_t = st.seq // SCAN_TB
    view = lambda x: x.reshape(x.shape[0], st.nb, st.seq, C_WIDTH)
    blk = (1, SCAN_NB, SCAN_TB, C_WIDTH)
    fwd = lambda d: pl.BlockSpec(blk, lambda g, t: (d, g, t, 0))
    bwd = lambda d: pl.BlockSpec(blk, lambda g, t: (d, g, n_t - 1 - t, 0))
    st_blk = (1, HEAD_DIM, HEAD_DIM, LANES)
    w, b, k, nkk, r, v = (view(x) for x in (w, b, k, nkk, r, v))
    tile = pltpu.VMEM((3, LANES, LANES), F32)
    ybuf = pltpu.VMEM((HEAD_DIM, LANES), F32)
    yblk = (SCAN_NB, SCAN_TB, C_WIDTH)
    y_shape = jax.ShapeDtypeStruct((st.nb, st.seq, C_WIDTH), F32)
    yf, yb, s_t = pl.pallas_call(
        _scan_kernel,
        grid=(groups, n_t),
        in_specs=[fwd(0), bwd(1), fwd(0), bwd(0), fwd(0), bwd(1), fwd(0), bwd(1), fwd(0), bwd(0), fwd(0), bwd(0),
                  pl.BlockSpec(st_blk, lambda g, t: (g, 0, 0, 0), pipeline_mode=pl.Buffered(1))],
        out_specs=[pl.BlockSpec(yblk, lambda g, t: (g, t, 0)), pl.BlockSpec(yblk, lambda g, t: (g, n_t - 1 - t, 0)),
                   pl.BlockSpec(st_blk, lambda g, t: (g, 0, 0, 0))],
        out_shape=[y_shape, y_shape, jax.ShapeDtypeStruct((groups,) + st_blk[1:], F32)],
        scratch_shapes=[tile, tile, pltpu.VMEM((HEAD_DIM, SUBLANES, LANES), F32), ybuf, ybuf, ybuf],
        compiler_params=_params("arbitrary", "arbitrary"),
        name=f"rwkv_scan_{st.name}",
    )(w, w, nkk, nkk, b, b, k, k, r, r, v, v, s0)
    return yf.reshape(st.n_tok, C_WIDTH), yb.reshape(st.n_tok, C_WIDTH), s_t


def _merge_kernel(x_ref, mod_ref, oa_ref, ob_ref, yf_ref, yb_ref, bonus_ref, g_ref, sg_ref, pa_ref, pb_ref, pc_ref,
                  wo_ref, avg_ref, gng_ref, gnb_ref, lng_ref, lnb_ref, o_ref):
    y = yf_ref[...] + yb_ref[...]
    avg = avg_ref[...]
    mu = _dot_f32(y, avg)
    d = y - mu
    var = _dot_f32(d * d, avg)
    yn = d * lax.rsqrt(var + GN_EPS) * gng_ref[...] + gnb_ref[...]
    oc = ((yn + bonus_ref[...]) * g_ref[...]).astype(BF16)
    merged = (sg_ref[:, :D_MODEL] * _dot(oa_ref[...].astype(BF16), pa_ref[...])
              + sg_ref[:, D_MODEL:2 * D_MODEL] * _dot(ob_ref[...].astype(BF16), pb_ref[...])
              + sg_ref[:, 2 * D_MODEL:] * _dot(oc, pc_ref[...]))
    o = _dot(merged.astype(BF16), wo_ref[...])
    o_ref[...] = _layer_norm(ALPHA * x_ref[...] + mod_ref[0, 5:6, :] * o, lng_ref[...], lnb_ref[...])


def _merge(st, x, mods, oa, oa_col, ob, ob_col, yf, yb, bonus, g, sg, pa, pb, pc, wo, avg, gn_g, gn_b, ln_g, ln_b):
    return pl.pallas_call(
        _merge_kernel,
        grid=(st.tiles,),
        in_specs=[
            _tile(D_MODEL), _mod_spec(st), _tile(A_Q, oa_col), _tile(B_W, ob_col), _tile(C_WIDTH), _tile(C_WIDTH),
            _tile(C_WIDTH), _tile(C_WIDTH), _tile(GATE_COLS),
            _resident((A_Q, D_MODEL)), _resident((B_W, D_MODEL)), _resident((C_WIDTH, D_MODEL)),
            _resident((D_MODEL, D_MODEL)), _resident((C_WIDTH, C_WIDTH)),
            _resident((1, C_WIDTH)), _resident((1, C_WIDTH)), _resident((1, D_MODEL)), _resident((1, D_MODEL)),
        ],
        out_specs=_tile(D_MODEL),
        out_shape=jax.ShapeDtypeStruct((st.n_tok, D_MODEL), F32),
        compiler_params=_params("parallel"),
        name=f"merge_{st.name}",
    )(x, mods, oa, ob, yf, yb, bonus, g, sg,
      pa, pb, pc, wo, avg,
      gn_g.reshape(1, C_WIDTH), gn_b.reshape(1, C_WIDTH), ln_g.reshape(1, D_MODEL), ln_b.reshape(1, D_MODEL))


def _block_diag2(a, b):
    za = jnp.zeros((a.shape[0], b.shape[1]), a.dtype)
    zb = jnp.zeros((b.shape[0], a.shape[1]), b.dtype)
    return jnp.concatenate([jnp.concatenate([a, za], axis=1), jnp.concatenate([zb, b], axis=1)], axis=0)


def kernel(x_prompt, x_sample, cache_attn_k, cache_attn_v, cache_na_k, cache_na_v, state_rwkv, c, c_ctx, w_ada, b_ada, ffn1_w_in, ffn1_w_out, ffn2_w_in, ffn2_w_out, w_in, w_shift, attn_sink, na_rpb, decay_w0, decay_up, iclr_a0, iclr_up, gate_up, k_k, k_a, r_k, gn_g, gn_b, proj_a, proj_b, proj_c, w_out, ln_g, ln_b):
    xs = {CTX: x_prompt.reshape(CTX.n_tok, D_MODEL), LAT: x_sample.reshape(LAT.n_tok, D_MODEL)}
    c_rows = jnp.zeros((MOD_ROWS, D_MODEL), F32).at[:DEC_BATCH].set(c).at[DEC_BATCH].set(c_ctx)
    mods_all = _ada_mods(c_rows, w_ada, b_ada)
    cos_t, sin_t = _rope_tables()
    head_of = np.arange(C_WIDTH) // HEAD_DIM
    ones_bd = jnp.asarray(head_of[:, None] == head_of[None, :], F32)
    avg_bd = ones_bd / HEAD_DIM
    ctx_groups = BATCH // SCAN_NB
    zero_state = jnp.zeros((ctx_groups, HEAD_DIM, HEAD_DIM, LANES), F32)

    new_ka, new_va, new_kb, new_vb, new_st = [], [], [], [], []
    for l in range(DEPTH):
        mods = mods_all[l]
        bf = lambda w: w[l].astype(BF16)
        ffn1 = (bf(ffn1_w_in), bf(ffn1_w_out), ln_g[l, 0], ln_b[l, 0], 0)
        ffn2 = (bf(ffn2_w_in), bf(ffn2_w_out), ln_g[l, 2], ln_b[l, 2], 6)
        w_in_l = bf(w_in)
        prep_w = (w_shift[l],
                  _block_diag2(decay_up[l, 0], decay_up[l, 1]).astype(BF16),
                  _block_diag2(iclr_up[l, 0], iclr_up[l, 1]).astype(BF16),
                  bf(gate_up),
                  decay_w0[l].reshape(1, 2 * C_WIDTH), iclr_a0[l].reshape(1, 2 * C_WIDTH),
                  k_k[l].reshape(1, C_WIDTH), k_a[l].reshape(1, C_WIDTH), r_k[l].reshape(1, C_WIDTH), ones_bd)
        merge_w = (bf(proj_a), bf(proj_b), bf(proj_c), bf(w_out), avg_bd, gn_g[l], gn_b[l], ln_g[l, 1], ln_b[l, 1])
        s0_lat = state_rwkv[:, l].transpose(3, 4, 2, 1, 0).reshape(1, HEAD_DIM, HEAD_DIM, LANES)

        for st in (CTX, LAT):
            x = _ffn(st, xs[st], mods, *ffn1)
            za, zb, zc_raw, sg = _inproj(st, x, mods, w_in_l)
            if st is CTX:
                o_ab = _ctx_attn(za, zb, attn_sink[l])
                attn = (o_ab, 0, o_ab, 1)
                zac = za.reshape(BATCH, SEQ, A_COLS)
                zbc = zb.reshape(BATCH, SEQ, B_COLS)
                new_ka.append(zac[..., A_Q:A_Q + A_KV].reshape(BATCH, SEQ, A_KV_HEADS, HEAD_DIM))
                new_va.append(zac[..., A_Q + A_KV:].reshape(BATCH, SEQ, A_KV_HEADS, HEAD_DIM))
                new_kb.append(zbc[..., B_W:2 * B_W].reshape(BATCH, SEQ, B_HEADS, HEAD_DIM))
                new_vb.append(zbc[..., 2 * B_W:].reshape(BATCH, SEQ, B_HEADS, HEAD_DIM))
            else:
                oa = _win_attn(za, cache_attn_k[:, l].reshape(DEC_BATCH, PAST_LEN, A_KV),
                               cache_attn_v[:, l].reshape(DEC_BATCH, PAST_LEN, A_KV), attn_sink[l], cos_t, sin_t)
                ob = _na_attn(zb, cache_na_k[:, l].reshape(DEC_BATCH, PAST_LEN, B_W),
                              cache_na_v[:, l].reshape(DEC_BATCH, PAST_LEN, B_W), na_rpb[l].reshape(-1))
                attn = (oa, 0, ob, 0)
            r, v, nkk, w, b, k, bonus, g = _rwkv_prep(st, zc_raw, *prep_w)
            yf, yb, s_t = _scan(st, w, b, k, nkk, r, v, zero_state if st is CTX else s0_lat)
            if st is CTX:
                s_t = s_t.reshape(ctx_groups, HEAD_DIM, HEAD_DIM, C_HEADS, 2, SCAN_NB)
                new_st.append(s_t.transpose(0, 5, 4, 3, 1, 2).reshape(BATCH, 2, C_HEADS, HEAD_DIM, HEAD_DIM))
            x = _merge(st, x, mods, *attn, yf, yb, bonus, g, sg, *merge_w)
            xs[st] = _ffn(st, x, mods, *ffn2)

    y_p = xs[CTX].reshape(BATCH, SEQ, D_MODEL)
    y_s = xs[LAT].reshape(DEC_BATCH, DEC_SEQ, D_MODEL)
    return (y_p, y_s, jnp.stack(new_ka, axis=1), jnp.stack(new_va, axis=1), jnp.stack(new_kb, axis=1),
            jnp.stack(new_vb, axis=1), jnp.stack(new_st, axis=1))
```

```python
import collections
import functools

import jax
import jax.numpy as jnp
import numpy as np
from jax import lax
from jax.experimental import pallas as pl
from jax.experimental.pallas import tpu as pltpu

D_MODEL = 1024
BATCH = 16
SEQ = 256
DEPTH = 2
DEC_BATCH = 8
DEC_SEQ = 1024
PAST_LEN = 256
GRID_W = 64
GRID_ROWS = DEC_SEQ // GRID_W
HEAD_DIM = 64
SCALE = HEAD_DIM ** -0.5
A_HEADS = 8
A_KV_HEADS = 2
A_GROUP = A_HEADS // A_KV_HEADS
A_WINDOW = 128
A_BLOCK = 128
A_Q = A_HEADS * HEAD_DIM
A_KV = A_KV_HEADS * HEAD_DIM
A_COLS = A_Q + 2 * A_KV
B_HEADS = 8
B_W = B_HEADS * HEAD_DIM
B_COLS = 3 * B_W
NA_ROWS = 8
NA_COLS = 16
C_HEADS = 8
C_WIDTH = C_HEADS * HEAD_DIM
DECAY_RANK = 64
ICLR_RANK = 64
GATE_RANK = 128
C_COLS = 3 * C_WIDTH + 2 * DECAY_RANK + 2 * ICLR_RANK + GATE_RANK
GN_EPS = 64e-5
GATE_COLS = 3 * D_MODEL
IN_COLS = A_COLS + B_COLS + C_COLS + GATE_COLS
D_FF = 2816
N_MOD = 9
ROPE_BASE = 10000.0
LN_EPS = 1e-5
NEG_INF = -1e30
ALPHA = (2 * DEPTH) ** 0.25

LANES = 128
SUBLANES = 8
TM = 256
MOD_ROWS = 16
SCAN_TB = 32
SCAN_NB = LANES // (2 * C_HEADS)
VMEM_LIMIT = 56 * 1024 * 1024

F32 = jnp.float32
BF16 = jnp.bfloat16

Stream = collections.namedtuple("Stream", "name nb seq n_tok tiles tiles_per_seq mod_row")


def _make_stream(name, nb, seq, mod_row):
    return Stream(name, nb, seq, nb * seq, nb * seq // TM, seq // TM, mod_row)


CTX = _make_stream("ctx", BATCH, SEQ, lambda i: DEC_BATCH)
LAT = _make_stream("lat", DEC_BATCH, DEC_SEQ, lambda i: i // (DEC_SEQ // TM))


def _sigmoid(x):
    return 1.0 / (1.0 + jnp.exp(-x))


def _layer_norm(y, g, b):
    mu = jnp.mean(y, axis=-1, keepdims=True)
    d = y - mu
    var = jnp.mean(d * d, axis=-1, keepdims=True)
    return d * lax.rsqrt(var + LN_EPS) * g + b


def _dot(a, b):
    return jnp.dot(a, b, preferred_element_type=F32)


def _dot_nt(a, b):
    return lax.dot_general(a, b, (((1,), (1,)), ((), ())), preferred_element_type=F32)


def _head_sum(x, same_head):
    hi = x.astype(BF16)
    lo = (x - hi.astype(F32)).astype(BF16)
    return _dot(hi, same_head) + _dot(lo, same_head)


def _resident(shape):
    zeros = (0,) * len(shape)
    return pl.BlockSpec(shape, lambda *_: zeros, pipeline_mode=pl.Buffered(1))


def _tile(width, col=0):
    return pl.BlockSpec((TM, width), lambda i: (i, col))


def _mod_spec(st):
    return pl.BlockSpec((1, N_MOD, D_MODEL), lambda i: (st.mod_row(i), 0, 0))


def _params(*sem):
    return pltpu.CompilerParams(dimension_semantics=sem, vmem_limit_bytes=VMEM_LIMIT)


def _ada_kernel(c_ref, w_ref, b_ref, o_ref):
    c = c_ref[...]
    s = (c * _sigmoid(c)).astype(BF16)
    o_ref[0] = _dot(s, w_ref[0].astype(BF16)) + b_ref[0]


def _ada_mods(c_rows, w_ada, b_ada):
    out = pl.pallas_call(
        _ada_kernel,
        grid=(DEPTH, N_MOD),
        in_specs=[
            pl.BlockSpec((MOD_ROWS, D_MODEL), lambda l, j: (0, 0)),
            pl.BlockSpec((1, D_MODEL, D_MODEL), lambda l, j: (l, 0, j)),
            pl.BlockSpec((1, 1, D_MODEL), lambda l, j: (l, 0, j)),
        ],
        out_specs=pl.BlockSpec((1, MOD_ROWS, D_MODEL), lambda l, j: (l, 0, j)),
        out_shape=jax.ShapeDtypeStruct((DEPTH, MOD_ROWS, N_MOD * D_MODEL), F32),
        compiler_params=_params("parallel", "parallel"),
        name="ada_mods",
    )(c_rows, w_ada, b_ada.reshape(DEPTH, 1, N_MOD * D_MODEL))
    return out.reshape(DEPTH, MOD_ROWS, N_MOD, D_MODEL)


def _ffn_kernel(x_ref, mod_ref, w1_ref, w2_ref, g_ref, b_ref, o_ref, *, base):
    x = x_ref[...]
    shift = mod_ref[0, base:base + 1, :]
    scale = mod_ref[0, base + 1:base + 2, :]
    gate = mod_ref[0, base + 2:base + 3, :]
    h = (x * (1.0 + scale) + shift).astype(BF16)
    a = _dot(h, w1_ref[:, :D_FF])
    g = _dot(h, w1_ref[:, D_FF:])
    u = (a * _sigmoid(a) * g).astype(BF16)
    f = _dot(u, w2_ref[...])
    o_ref[...] = _layer_norm(ALPHA * x + 0.5 * gate * f, g_ref[...], b_ref[...])


def _ffn(st, x, mods, w1, w2, ln_g, ln_b, base):
    return pl.pallas_call(
        functools.partial(_ffn_kernel, base=base),
        grid=(st.tiles,),
        in_specs=[
            _tile(D_MODEL), _mod_spec(st),
            _resident((D_MODEL, 2 * D_FF)), _resident((D_FF, D_MODEL)),
            _resident((1, D_MODEL)), _resident((1, D_MODEL)),
        ],
        out_specs=_tile(D_MODEL),
        out_shape=jax.ShapeDtypeStruct((st.n_tok, D_MODEL), F32),
        compiler_params=_params("parallel"),
        name=f"ffn{base // 6 + 1}_{st.name}",
    )(x, mods, w1, w2, ln_g.reshape(1, D_MODEL), ln_b.reshape(1, D_MODEL))


_COL_A = 0
_COL_B = A_COLS
_COL_C = A_COLS + B_COLS
_COL_G = A_COLS + B_COLS + C_COLS


def _inproj_kernel(x_ref, mod_ref, w_ref, za_ref, zb_ref, zc_ref, sg_ref):
    x = x_ref[...]
    h = (x * (1.0 + mod_ref[0, 4:5, :]) + mod_ref[0, 3:4, :]).astype(BF16)
    za_ref[...] = _dot(h, w_ref[:, _COL_A:_COL_B])
    zb_ref[...] = _dot(h, w_ref[:, _COL_B:_COL_C])
    zc_ref[...] = _dot(h, w_ref[:, _COL_C:_COL_G])
    sg_ref[...] = _sigmoid(_dot(h, w_ref[:, _COL_G:]))


def _inproj(st, x, mods, w_in):
    widths = (A_COLS, B_COLS, C_COLS, GATE_COLS)
    return pl.pallas_call(
        _inproj_kernel,
        grid=(st.tiles,),
        in_specs=[_tile(D_MODEL), _mod_spec(st), _resident((D_MODEL, IN_COLS))],
        out_specs=[_tile(w) for w in widths],
        out_shape=[jax.ShapeDtypeStruct((st.n_tok, w), F32) for w in widths],
        compiler_params=_params("parallel"),
        name=f"inproj_{st.name}",
    )(x, mods, w_in)


def _softmax_pv(scores, values, sink=None):
    m = scores[0].max(axis=-1, keepdims=True)
    for s in scores[1:]:
        m = jnp.maximum(m, s.max(axis=-1, keepdims=True))
    if sink is not None:
        m = jnp.maximum(m, sink)
    den = jnp.exp(sink - m) if sink is not None else 0.0
    acc = None
    for s, v in zip(scores, values):
        p = jnp.exp(s - m)
        den = den + p.sum(axis=-1, keepdims=True)
        pv = _dot(p.astype(BF16), v)
        acc = pv if acc is None else acc + pv
    return acc / den


def _head(x, h):
    return x[:, h * HEAD_DIM:(h + 1) * HEAD_DIM]


def _ctx_attn_kernel(sink_ref, za_ref, zb_ref, o_ref):
    za = za_ref[...]
    zb = zb_ref[...]
    outs = []
    row = lax.broadcasted_iota(jnp.int32, (A_GROUP * SEQ, 1), 0) // SEQ
    for g in range(A_KV_HEADS):
        q4 = jnp.concatenate([_head(za, A_GROUP * g + j) for j in range(A_GROUP)], axis=0).astype(BF16)
        k = _head(za, A_HEADS + g).astype(BF16)
        v = _head(za, A_HEADS + A_KV_HEADS + g).astype(BF16)
        sink = jnp.zeros((A_GROUP * SEQ, 1), F32)
        for j in range(A_GROUP):
            sink = jnp.where(row == j, sink_ref[A_GROUP * g + j], sink)
        o4 = _softmax_pv([_dot_nt(q4, k) * SCALE], [v], sink)
        outs.extend(o4[j * SEQ:(j + 1) * SEQ] for j in range(A_GROUP))
    for h in range(B_HEADS):
        q = _head(zb, h).astype(BF16)
        k = _head(zb, B_HEADS + h).astype(BF16)
        v = _head(zb, 2 * B_HEADS + h).astype(BF16)
        outs.append(_softmax_pv([_dot_nt(q, k) * SCALE], [v]))
    o_ref[...] = jnp.concatenate(outs, axis=-1)


def _ctx_attn(za, zb, sink):
    return pl.pallas_call(
        _ctx_attn_kernel,
        grid=(BATCH,),
        in_specs=[
            pl.BlockSpec(memory_space=pltpu.SMEM),
            pl.BlockSpec((SEQ, A_COLS), lambda b: (b, 0)),
            pl.BlockSpec((SEQ, B_COLS), lambda b: (b, 0)),
        ],
        out_specs=pl.BlockSpec((SEQ, A_Q + B_W), lambda b: (b, 0)),
        out_shape=jax.ShapeDtypeStruct((CTX.n_tok, A_Q + B_W), F32),
        compiler_params=_params("parallel"),
        name="ctx_attn",
    )(sink, za, zb)


def _rope(x, cos, sin_signed):
    w = x.shape[-1]
    half = HEAD_DIM // 2
    lane = lax.broadcasted_iota(jnp.int32, x.shape, 1)
    partner = jnp.where(lane % HEAD_DIM < half, pltpu.roll(x, w - half, 1), pltpu.roll(x, half, 1))
    return x * cos + partner * sin_signed


def _win_attn_kernel(sink_ref, za_ref, ck_ref, cv_ref, cos_ref, sin_ref, o_ref):
    n = pl.program_id(1)
    nb = DEC_SEQ // A_BLOCK
    q0 = pl.multiple_of(n * A_BLOCK, A_BLOCK)
    q = _rope(za_ref[pl.ds(q0, A_BLOCK), :A_Q], cos_ref[pl.ds(q0, A_BLOCK), :], sin_ref[pl.ds(q0, A_BLOCK), :])
    ks, vs = [], []
    for j in range(3):
        kb = jnp.clip(n - 1 + j, 0, nb - 1)
        k0 = pl.multiple_of(kb * A_BLOCK, A_BLOCK)
        kblk = za_ref[pl.ds(k0, A_BLOCK), A_Q:A_Q + A_KV]
        ks.append(_rope(kblk, cos_ref[pl.ds(k0, A_BLOCK), :A_KV], sin_ref[pl.ds(k0, A_BLOCK), :A_KV]))
        vs.append(za_ref[pl.ds(k0, A_BLOCK), A_Q + A_KV:])
    kband = jnp.concatenate(ks, axis=0).astype(BF16)
    vband = jnp.concatenate(vs, axis=0).astype(BF16)
    ck = ck_ref[0].astype(BF16)
    cv = cv_ref[0].astype(BF16)
    m4 = A_GROUP * A_BLOCK
    qi = lax.broadcasted_iota(jnp.int32, (m4, 3 * A_BLOCK), 0) % A_BLOCK
    kj = lax.broadcasted_iota(jnp.int32, (m4, 3 * A_BLOCK), 1)
    rel = kj - A_BLOCK - qi
    kpos = (n - 1) * A_BLOCK + kj
    valid = (jnp.abs(rel) <= A_WINDOW) & (kpos >= 0) & (kpos < DEC_SEQ)
    row = lax.broadcasted_iota(jnp.int32, (m4, 1), 0) // A_BLOCK
    outs = []
    for g in range(A_KV_HEADS):
        q4 = jnp.concatenate([_head(q, A_GROUP * g + j) for j in range(A_GROUP)], axis=0).astype(BF16)
        s_loc = jnp.where(valid, _dot_nt(q4, _head(kband, g)) * SCALE, NEG_INF)
        s_ctx = _dot_nt(q4, _head(ck, g)) * SCALE
        sink = jnp.zeros((m4, 1), F32)
        for j in range(A_GROUP):
            sink = jnp.where(row == j, sink_ref[A_GROUP * g + j], sink)
        o4 = _softmax_pv([s_loc, s_ctx], [_head(vband, g), _head(cv, g)], sink)
        outs.extend(o4[j * A_BLOCK:(j + 1) * A_BLOCK] for j in range(A_GROUP))
    o_ref[...] = jnp.concatenate(outs, axis=-1)


def _rope_tables():
    t = np.arange(DEC_SEQ)
    n_freq = HEAD_DIM // 4
    inv = ROPE_BASE ** (-jnp.arange(n_freq, dtype=F32) / n_freq)
    rows = jnp.asarray(t // GRID_W, F32)
    cols = jnp.asarray(t % GRID_W, F32)
    ang = jnp.concatenate([rows[:, None] * inv, cols[:, None] * inv], axis=-1)
    cos, sin = jnp.cos(ang), jnp.sin(ang)
    cos_h = jnp.concatenate([cos, cos], axis=-1)
    sin_h = jnp.concatenate([-sin, sin], axis=-1)
    return jnp.tile(cos_h, (1, A_HEADS)), jnp.tile(sin_h, (1, A_HEADS))


def _win_attn(za, ck, cv, sink, cos_t, sin_t):
    nb = DEC_SEQ // A_BLOCK
    return pl.pallas_call(
        _win_attn_kernel,
        grid=(DEC_BATCH, nb),
        in_specs=[
            pl.BlockSpec(memory_space=pltpu.SMEM),
            pl.BlockSpec((DEC_SEQ, A_COLS), lambda b, n: (b, 0)),
            pl.BlockSpec((1, PAST_LEN, A_KV), lambda b, n: (b, 0, 0)),
            pl.BlockSpec((1, PAST_LEN, A_KV), lambda b, n: (b, 0, 0)),
            pl.BlockSpec((DEC_SEQ, A_Q), lambda b, n: (0, 0)),
            pl.BlockSpec((DEC_SEQ, A_Q), lambda b, n: (0, 0)),
        ],
        out_specs=pl.BlockSpec((A_BLOCK, A_Q), lambda b, n: (b * nb + n, 0)),
        out_shape=jax.ShapeDtypeStruct((LAT.n_tok, A_Q), F32),
        compiler_params=_params("parallel", "arbitrary"),
        name="win_attn",
    )(sink, za, ck, cv, cos_t, sin_t)


_NA_QBLK = 256
_RPB_I = 2 * NA_ROWS - 1
_RPB_J = 2 * NA_COLS - 1


def _na_build_bias(rpb_ref, head, bias_ref, slot):
    shape = (GRID_W, 2 * GRID_W)
    qc = lax.broadcasted_iota(jnp.int32, shape, 0)
    lane = lax.broadcasted_iota(jnp.int32, shape, 1)
    kc = lane % GRID_W
    left = lane < GRID_W
    cs = jnp.clip(qc - NA_COLS // 2, 0, GRID_W - NA_COLS)
    idx = jnp.where((kc >= cs) & (kc < cs + NA_COLS), jnp.clip(kc - qc + NA_COLS - 1, 0, _RPB_J - 1), -1)
    base = head * (_RPB_I * _RPB_J)
    tables = []
    for i in range(_RPB_I):
        t = jnp.full(shape, NEG_INF, F32)
        for j in range(_RPB_J):
            t = jnp.where(idx == j, rpb_ref[base + i * _RPB_J + j], t)
        tables.append(t)
    neg = jnp.full(shape, NEG_INF, F32)
    half = NA_ROWS // 2
    for qr in range(GRID_ROWS):
        rs = min(max(qr - half, 0), GRID_ROWS - NA_ROWS)
        for m in range(GRID_ROWS // 2):
            kr0, kr1 = 2 * m, 2 * m + 1
            v0 = rs <= kr0 < rs + NA_ROWS
            v1 = rs <= kr1 < rs + NA_ROWS
            i0 = kr0 - qr + NA_ROWS - 1
            if v0 and v1:
                piece = jnp.where(left, tables[i0], tables[i0 + 1])
            elif v0:
                piece = jnp.where(left, tables[i0], NEG_INF)
            elif v1:
                piece = jnp.where(left, NEG_INF, tables[i0 + 1])
            else:
                piece = neg
            bias_ref[slot, qr * GRID_W:(qr + 1) * GRID_W, m * 2 * GRID_W:(m + 1) * 2 * GRID_W] = piece


def _na_attn_kernel(rpb_ref, q_ref, k_ref, v_ref, ck_ref, cv_ref, o_ref, bias_ref):
    hp = pl.program_id(0)

    @pl.when(pl.program_id(1) == 0)
    def _():
        for half in range(2):
            _na_build_bias(rpb_ref, 2 * hp + half, bias_ref, half)

    ck2 = ck_ref[0].astype(BF16)
    cv2 = cv_ref[0].astype(BF16)
    rows_per_blk = _NA_QBLK // GRID_W
    for qb in range(DEC_SEQ // _NA_QBLK):
        rows = slice(qb * _NA_QBLK, (qb + 1) * _NA_QBLK)
        r_lo = min(max(qb * rows_per_blk - NA_ROWS // 2, 0), GRID_ROWS - NA_ROWS)
        r_hi = min(max((qb + 1) * rows_per_blk - 1 - NA_ROWS // 2, 0), GRID_ROWS - NA_ROWS) + NA_ROWS
        keys = slice(r_lo // 2 * 2 * GRID_W, -(-r_hi // 2) * 2 * GRID_W)
        k2 = k_ref[keys, :].astype(BF16)
        v2 = v_ref[keys, :].astype(BF16)
        q2 = q_ref[rows, :].astype(BF16)
        outs = []
        for half in range(2):
            q = _head(q2, half)
            s_loc = _dot_nt(q, _head(k2, half)) * SCALE + bias_ref[half, rows, keys]
            s_ctx = _dot_nt(q, _head(ck2, half)) * SCALE
            outs.append(_softmax_pv([s_loc, s_ctx], [_head(v2, half), _head(cv2, half)]))
        o_ref[rows, :] = jnp.concatenate(outs, axis=-1)


def _na_attn(zb, ck, cv, rpb_flat):
    pairs = B_HEADS // 2
    pw = 2 * HEAD_DIM
    return pl.pallas_call(
        _na_attn_kernel,
        grid=(pairs, DEC_BATCH),
        in_specs=[
            pl.BlockSpec(memory_space=pltpu.SMEM),
            pl.BlockSpec((DEC_SEQ, pw), lambda hp, b: (b, hp)),
            pl.BlockSpec((DEC_SEQ, pw), lambda hp, b: (b, pairs + hp)),
            pl.BlockSpec((DEC_SEQ, pw), lambda hp, b: (b, 2 * pairs + hp)),
            pl.BlockSpec((1, PAST_LEN, pw), lambda hp, b: (b, 0, hp)),
            pl.BlockSpec((1, PAST_LEN, pw), lambda hp, b: (b, 0, hp)),
        ],
        out_specs=pl.BlockSpec((DEC_SEQ, pw), lambda hp, b: (b, hp)),
        out_shape=jax.ShapeDtypeStruct((LAT.n_tok, B_W), F32),
        scratch_shapes=[pltpu.VMEM((2, DEC_SEQ, DEC_SEQ), F32)],
        compiler_params=_params("arbitrary", "arbitrary"),
        name="na_attn",
    )(rpb_flat, zb, zb, zb, ck, cv)


_C_R, _C_K, _C_V = 0, C_WIDTH, 2 * C_WIDTH
_C_WLO = 3 * C_WIDTH
_C_ALO = _C_WLO + 2 * DECAY_RANK
_C_GLO = _C_ALO + 2 * ICLR_RANK
_HALO = SUBLANES


def _rwkv_prep_kernel(z_ref, zp_ref, zn_ref, wsh_ref, dup_ref, aup_ref, gup_ref, w0_ref, a0_ref, kk_ref, ka_ref,
                      rk_ref, ones_ref,
                      r_ref, v_ref, nkk_ref, w_ref, b_ref, k_ref, bonus_ref, g_ref, *, tiles_per_seq):
    j = pl.program_id(0) % tiles_per_seq
    has_prev = jnp.where(j != 0, 1.0, 0.0)
    has_next = jnp.where(j != tiles_per_seq - 1, 1.0, 0.0)
    z = z_ref[...]
    row = lax.broadcasted_iota(jnp.int32, z.shape, 0)
    z_prev = jnp.where(row == 0, zp_ref[_HALO - 1:_HALO, :] * has_prev, pltpu.roll(z, 1, 0))
    z_next = jnp.where(row == TM - 1, zn_ref[0:1, :] * has_next, pltpu.roll(z, TM - 1, 0))
    zc = z_prev * wsh_ref[0:1, :] + z * wsh_ref[1:2, :] + z_next * wsh_ref[2:3, :]

    r = zc[:, _C_R:_C_R + C_WIDTH]
    k = zc[:, _C_K:_C_K + C_WIDTH]
    v = zc[:, _C_V:_C_V + C_WIDTH]
    wlo = zc[:, _C_WLO:_C_WLO + 2 * DECAY_RANK]
    alo = zc[:, _C_ALO:_C_ALO + 2 * ICLR_RANK]
    glo = zc[:, _C_GLO:_C_GLO + GATE_RANK]
    ones = ones_ref[...]

    g_ref[...] = _dot(_sigmoid(glo).astype(BF16), gup_ref[...])
    kk = k * kk_ref[...]
    norm = jnp.sqrt(_head_sum(kk * kk, ones))
    kk = kk / jnp.maximum(norm, 1e-12)
    x = w0_ref[...] + _dot(jnp.tanh(wlo).astype(BF16), dup_ref[...])
    logw = -(jnp.maximum(-x, 0.0) + jnp.log(1.0 + jnp.exp(-jnp.abs(x)))) - 0.5
    decay = jnp.exp(-jnp.exp(logw))
    a = _sigmoid(a0_ref[...] + _dot(alo.astype(BF16), aup_ref[...]))
    ka = ka_ref[...]
    kd_sum = None
    for d in range(2):
        a_d = a[:, d * C_WIDTH:(d + 1) * C_WIDTH]
        k_d = k * (1.0 + (a_d - 1.0) * ka)
        w_ref[d] = decay[:, d * C_WIDTH:(d + 1) * C_WIDTH]
        b_ref[d] = kk * a_d
        k_ref[d] = k_d
        kd_sum = k_d if kd_sum is None else kd_sum + k_d
    bonus_ref[...] = _head_sum(r * kd_sum * rk_ref[...], ones) * v
    r_ref[0] = r
    v_ref[0] = v
    nkk_ref[0] = -kk


def _rwkv_prep(st, zc_raw, w_shift, dec_up, iclr_up, gate_up, w0, a0, k_k, k_a, r_k, ones_bd):
    tiles_per_halo = TM // _HALO
    n_halo = st.n_tok // _HALO
    row = lambda n: _resident((1, n))
    stacked = lambda d: (pl.BlockSpec((d, TM, C_WIDTH), lambda i: (0, i, 0)),
                         jax.ShapeDtypeStruct((d, st.n_tok, C_WIDTH), F32))
    flat = (_tile(C_WIDTH), jax.ShapeDtypeStruct((st.n_tok, C_WIDTH), F32))
    outs = [stacked(1)] * 3 + [stacked(2)] * 3 + [flat] * 2
    return pl.pallas_call(
        functools.partial(_rwkv_prep_kernel, tiles_per_seq=st.tiles_per_seq),
        grid=(st.tiles,),
        in_specs=[
            _tile(C_COLS),
            pl.BlockSpec((_HALO, C_COLS), lambda i: (jnp.maximum(i * tiles_per_halo - 1, 0), 0)),
            pl.BlockSpec((_HALO, C_COLS), lambda i: (jnp.minimum((i + 1) * tiles_per_halo, n_halo - 1), 0)),
            _resident((3, C_COLS)),
            _resident((2 * DECAY_RANK, 2 * C_WIDTH)),
            _resident((2 * ICLR_RANK, 2 * C_WIDTH)),
            _resident((GATE_RANK, C_WIDTH)),
            row(2 * C_WIDTH), row(2 * C_WIDTH), row(C_WIDTH), row(C_WIDTH), row(C_WIDTH),
            _resident((C_WIDTH, C_WIDTH)),
        ],
        out_specs=[o[0] for o in outs],
        out_shape=[o[1] for o in outs],
        compiler_params=_params("parallel"),
        name=f"rwkv_prep_{st.name}",
    )(zc_raw, zc_raw, zc_raw, w_shift, dec_up, iclr_up, gate_up, w0, a0, k_k, k_a, r_k, ones_bd)


_PB = 2 * SCAN_NB


def _scan_kernel(wf, wb, af, ab, bf, bb, kf, kb, rf, rb, vf, vb, s0_ref, yf_ref, yb_ref, state,
                 ops_a, ops_b, red, wcum, y_a, y_b):
    @pl.when(pl.program_id(1) == 0)
    def _():
        state[...] = s0_ref[...]
        y_b[...] = jnp.zeros_like(y_b)

    wcum[...] = jnp.ones_like(wcum)
    left = lax.broadcasted_iota(jnp.int32, (_PB, LANES), 1) < HEAD_DIM
    low = lax.broadcasted_iota(jnp.int32, (SUBLANES, LANES), 0) < SUBLANES // 2

    def load_ops(i, ops):
        ir = SCAN_TB - 1 - i
        for p, (pf, pb, qf, qb) in enumerate(((wf, wb, af, ab), (bf, bb, kf, kb), (rf, rb, vf, vb))):
            xp = jnp.concatenate([pf[0, :, i, :], pb[0, :, ir, :]], axis=0)
            xq = jnp.concatenate([qf[0, :, i, :], qb[0, :, ir, :]], axis=0)
            blocks = []
            for j in range(C_HEADS // 2):
                pv = xp[:, j * LANES:(j + 1) * LANES]
                qv = xq[:, j * LANES:(j + 1) * LANES]
                blocks.append(jnp.where(left, pv, pltpu.roll(qv, HEAD_DIM, 1)))
                blocks.append(jnp.where(left, pltpu.roll(pv, HEAD_DIM, 1), qv))
            ops[p] = jnp.concatenate(blocks, axis=0).T

    def fold(x):
        return x.reshape(HEAD_DIM // SUBLANES, SUBLANES, LANES).sum(axis=0)

    def compute(ops, ybuf):
        w_prev = wcum[...]
        w_now = w_prev * ops[0, 0:HEAD_DIM]
        wcum[...] = w_now
        w_inv = 1.0 / w_now
        a = ops[0, HEAD_DIM:] * w_prev
        b = ops[1, 0:HEAD_DIM]
        k = ops[1, HEAD_DIM:]
        r = ops[2, 0:HEAD_DIM]
        br = jnp.sum(b * r, axis=0, keepdims=True)
        kr = jnp.sum(k * r, axis=0, keepdims=True)
        b = b * w_inv
        k = k * w_inv
        r = r * w_now
        for vi in range(HEAD_DIM):
            z = state[0, vi]
            p1 = fold(z * a)
            p2 = fold(z * r)
            t = jnp.where(low, p1, p2) + pltpu.roll(jnp.where(low, p2, p1), SUBLANES // 2, 0)
            t = t + pltpu.roll(t, SUBLANES - 2, 0)
            red[vi] = t + pltpu.roll(t, SUBLANES - 1, 0)
        for vi in range(HEAD_DIM):
            state[0, vi] = state[0, vi] + red[vi, 0:1] * b + ops[2, HEAD_DIM + vi:HEAD_DIM + vi + 1] * k
        ybuf[...] = red[:, SUBLANES // 2, :] + red[:, 0, :] * br + ops[2, HEAD_DIM:] * kr

    def emit(i, ybuf):
        y = ybuf[...]
        t = jnp.concatenate([y, y], axis=0).T
        cols = [jnp.where(left, t[(2 * j) * _PB:(2 * j + 1) * _PB], t[(2 * j + 1) * _PB:(2 * j + 2) * _PB])
                for j in range(C_HEADS // 2)]
        out = jnp.concatenate(cols, axis=1)
        yf_ref[:, i, :] = out[:SCAN_NB]
        yb_ref[:, SCAN_TB - 1 - i, :] = out[SCAN_NB:]

    load_ops(0, ops_a)

    def body(m, carry):
        i0 = 2 * m
        load_ops(i0 + 1, ops_b)
        compute(ops_a, y_a)
        emit(jnp.maximum(i0 - 1, 0), y_b)
        load_ops(jnp.minimum(i0 + 2, SCAN_TB - 1), ops_a)
        compute(ops_b, y_b)
        emit(i0, y_a)
        return carry

    lax.fori_loop(0, SCAN_TB // 2, body, 0)
    emit(SCAN_TB - 1, y_b)
    w_end = wcum[...]
    for vi in range(HEAD_DIM):
        state[0, vi] = state[0, vi] * w_end


def _scan(st, w, b, k, nkk, r, v, s0):
    groups = st.nb // SCAN_NB
    n_t = st.seq // SCAN_TB
    view = lambda x: x.reshape(x.shape[0], st.nb, st.seq, C_WIDTH)
    blk = (1, SCAN_NB, SCAN_TB, C_WIDTH)
    fwd = lambda d: pl.BlockSpec(blk, lambda g, t: (d, g, t, 0))
    bwd = lambda d: pl.BlockSpec(blk, lambda g, t: (d, g, n_t - 1 - t, 0))
    st_blk = (1, HEAD_DIM, HEAD_DIM, LANES)
    w, b, k, nkk, r, v = (view(x) for x in (w, b, k, nkk, r, v))
    tile = pltpu.VMEM((3, LANES, LANES), F32)
    ybuf = pltpu.VMEM((HEAD_DIM, LANES), F32)
    yblk = (SCAN_NB, SCAN_TB, C_WIDTH)
    y_shape = jax.ShapeDtypeStruct((st.nb, st.seq, C_WIDTH), F32)
    yf, yb, s_t = pl.pallas_call(
        _scan_kernel,
        grid=(groups, n_t),
        in_specs=[fwd(0), bwd(1), fwd(0), bwd(0), fwd(0), bwd(1), fwd(0), bwd(1), fwd(0), bwd(0), fwd(0), bwd(0),
                  pl.BlockSpec(st_blk, lambda g, t: (g, 0, 0, 0), pipeline_mode=pl.Buffered(1))],
        out_specs=[pl.BlockSpec(yblk, lambda g, t: (g, t, 0)), pl.BlockSpec(yblk, lambda g, t: (g, n_t - 1 - t, 0)),
                   pl.BlockSpec(st_blk, lambda g, t: (g, 0, 0, 0))],
        out_shape=[y_shape, y_shape, jax.ShapeDtypeStruct((groups,) + st_blk[1:], F32)],
        scratch_shapes=[tile, tile, pltpu.VMEM((HEAD_DIM, SUBLANES, LANES), F32), ybuf, ybuf, ybuf],
        compiler_params=_params("arbitrary", "arbitrary"),
        name=f"rwkv_scan_{st.name}",
    )(w, w, nkk, nkk, b, b, k, k, r, r, v, v, s0)
    return yf.reshape(st.n_tok, C_WIDTH), yb.reshape(st.n_tok, C_WIDTH), s_t


def _merge_kernel(x_ref, mod_ref, oa_ref, ob_ref, yf_ref, yb_ref, bonus_ref, g_ref, sg_ref, pa_ref, pb_ref, pc_ref,
                  wo_ref, ones_ref, gng_ref, gnb_ref, lng_ref, lnb_ref, o_ref):
    y = yf_ref[...] + yb_ref[...]
    ones = ones_ref[...]
    mu = _head_sum(y, ones) * (1.0 / HEAD_DIM)
    d = y - mu
    var = _head_sum(d * d, ones) * (1.0 / HEAD_DIM)
    yn = d * lax.rsqrt(var + GN_EPS) * gng_ref[...] + gnb_ref[...]
    oc = ((yn + bonus_ref[...]) * g_ref[...]).astype(BF16)
    merged = (sg_ref[:, :D_MODEL] * _dot(oa_ref[...].astype(BF16), pa_ref[...])
              + sg_ref[:, D_MODEL:2 * D_MODEL] * _dot(ob_ref[...].astype(BF16), pb_ref[...])
              + sg_ref[:, 2 * D_MODEL:] * _dot(oc, pc_ref[...]))
    o = _dot(merged.astype(BF16), wo_ref[...])
    o_ref[...] = _layer_norm(ALPHA * x_ref[...] + mod_ref[0, 5:6, :] * o, lng_ref[...], lnb_ref[...])


def _merge(st, x, mods, oa, oa_col, ob, ob_col, yf, yb, bonus, g, sg, pa, pb, pc, wo, avg, gn_g, gn_b, ln_g, ln_b):
    return pl.pallas_call(
        _merge_kernel,
        grid=(st.tiles,),
        in_specs=[
            _tile(D_MODEL), _mod_spec(st), _tile(A_Q, oa_col), _tile(B_W, ob_col), _tile(C_WIDTH), _tile(C_WIDTH),
            _tile(C_WIDTH), _tile(C_WIDTH), _tile(GATE_COLS),
            _resident((A_Q, D_MODEL)), _resident((B_W, D_MODEL)), _resident((C_WIDTH, D_MODEL)),
            _resident((D_MODEL, D_MODEL)), _resident((C_WIDTH, C_WIDTH)),
            _resident((1, C_WIDTH)), _resident((1, C_WIDTH)), _resident((1, D_MODEL)), _resident((1, D_MODEL)),
        ],
        out_specs=_tile(D_MODEL),
        out_shape=jax.ShapeDtypeStruct((st.n_tok, D_MODEL), F32),
        compiler_params=_params("parallel"),
        name=f"merge_{st.name}",
    )(x, mods, oa, ob, yf, yb, bonus, g, sg,
      pa, pb, pc, wo, avg,
      gn_g.reshape(1, C_WIDTH), gn_b.reshape(1, C_WIDTH), ln_g.reshape(1, D_MODEL), ln_b.reshape(1, D_MODEL))


def _block_diag2(a, b):
    za = jnp.zeros((a.shape[0], b.shape[1]), a.dtype)
    zb = jnp.zeros((b.shape[0], a.shape[1]), b.dtype)
    return jnp.concatenate([jnp.concatenate([a, za], axis=1), jnp.concatenate([zb, b], axis=1)], axis=0)


def kernel(x_prompt, x_sample, cache_attn_k, cache_attn_v, cache_na_k, cache_na_v, state_rwkv, c, c_ctx, w_ada, b_ada, ffn1_w_in, ffn1_w_out, ffn2_w_in, ffn2_w_out, w_in, w_shift, attn_sink, na_rpb, decay_w0, decay_up, iclr_a0, iclr_up, gate_up, k_k, k_a, r_k, gn_g, gn_b, proj_a, proj_b, proj_c, w_out, ln_g, ln_b):
    xs = {CTX: x_prompt.reshape(CTX.n_tok, D_MODEL), LAT: x_sample.reshape(LAT.n_tok, D_MODEL)}
    c_rows = jnp.zeros((MOD_ROWS, D_MODEL), F32).at[:DEC_BATCH].set(c).at[DEC_BATCH].set(c_ctx)
    mods_all = _ada_mods(c_rows, w_ada, b_ada)
    cos_t, sin_t = _rope_tables()
    head_of = np.arange(C_WIDTH) // HEAD_DIM
    ones_bd = jnp.asarray(head_of[:, None] == head_of[None, :], BF16)
    ctx_groups = BATCH // SCAN_NB
    zero_state = jnp.zeros((ctx_groups, HEAD_DIM, HEAD_DIM, LANES), F32)

    new_ka, new_va, new_kb, new_vb, new_st = [], [], [], [], []
    for l in range(DEPTH):
        mods = mods_all[l]
        bf = lambda w: w[l].astype(BF16)
        ffn1 = (bf(ffn1_w_in), bf(ffn1_w_out), ln_g[l, 0], ln_b[l, 0], 0)
        ffn2 = (bf(ffn2_w_in), bf(ffn2_w_out), ln_g[l, 2], ln_b[l, 2], 6)
        w_in_l = bf(w_in)
        prep_w = (w_shift[l],
                  _block_diag2(decay_up[l, 0], decay_up[l, 1]).astype(BF16),
                  _block_diag2(iclr_up[l, 0], iclr_up[l, 1]).astype(BF16),
                  bf(gate_up),
                  decay_w0[l].reshape(1, 2 * C_WIDTH), iclr_a0[l].reshape(1, 2 * C_WIDTH),
                  k_k[l].reshape(1, C_WIDTH), k_a[l].reshape(1, C_WIDTH), r_k[l].reshape(1, C_WIDTH), ones_bd)
        merge_w = (bf(proj_a), bf(proj_b), bf(proj_c), bf(w_out), ones_bd, gn_g[l], gn_b[l], ln_g[l, 1], ln_b[l, 1])
        s0_lat = state_rwkv[:, l].transpose(3, 4, 2, 1, 0).reshape(1, HEAD_DIM, HEAD_DIM, LANES)

        for st in (CTX, LAT):
            x = _ffn(st, xs[st], mods, *ffn1)
            za, zb, zc_raw, sg = _inproj(st, x, mods, w_in_l)
            if st is CTX:
                o_ab = _ctx_attn(za, zb, attn_sink[l])
                attn = (o_ab, 0, o_ab, 1)
                zac = za.reshape(BATCH, SEQ, A_COLS)
                zbc = zb.reshape(BATCH, SEQ, B_COLS)
                new_ka.append(zac[..., A_Q:A_Q + A_KV].reshape(BATCH, SEQ, A_KV_HEADS, HEAD_DIM))
                new_va.append(zac[..., A_Q + A_KV:].reshape(BATCH, SEQ, A_KV_HEADS, HEAD_DIM))
                new_kb.append(zbc[..., B_W:2 * B_W].reshape(BATCH, SEQ, B_HEADS, HEAD_DIM))
                new_vb.append(zbc[..., 2 * B_W:].reshape(BATCH, SEQ, B_HEADS, HEAD_DIM))
            else:
                oa = _win_attn(za, cache_attn_k[:, l].reshape(DEC_BATCH, PAST_LEN, A_KV),
                               cache_attn_v[:, l].reshape(DEC_BATCH, PAST_LEN, A_KV), attn_sink[l], cos_t, sin_t)
                ob = _na_attn(zb, cache_na_k[:, l].reshape(DEC_BATCH, PAST_LEN, B_W),
                              cache_na_v[:, l].reshape(DEC_BATCH, PAST_LEN, B_W), na_rpb[l].reshape(-1))
                attn = (oa, 0, ob, 0)
            r, v, nkk, w, b, k, bonus, g = _rwkv_prep(st, zc_raw, *prep_w)
            yf, yb, s_t = _scan(st, w, b, k, nkk, r, v, zero_state if st is CTX else s0_lat)
            if st is CTX:
                s_t = s_t.reshape(ctx_groups, HEAD_DIM, HEAD_DIM, C_HEADS, 2, SCAN_NB)
                new_st.append(s_t.transpose(0, 5, 4, 3, 1, 2).reshape(BATCH, 2, C_HEADS, HEAD_DIM, HEAD_DIM))
            x = _merge(st, x, mods, *attn, yf, yb, bonus, g, sg, *merge_w)
            xs[st] = _ffn(st, x, mods, *ffn2)

    y_p = xs[CTX].reshape(BATCH, SEQ, D_MODEL)
    y_s = xs[LAT].reshape(DEC_BATCH, DEC_SEQ, D_MODEL)
    return (y_p, y_s, jnp.stack(new_ka, axis=1), jnp.stack(new_va, axis=1), jnp.stack(new_kb, axis=1),
            jnp.stack(new_vb, axis=1), jnp.stack(new_st, axis=1))
```

```python
import collections
import functools

import jax
import jax.numpy as jnp
import numpy as np
from jax import lax
from jax.experimental import pallas as pl
from jax.experimental.pallas import tpu as pltpu

D_MODEL = 1024
BATCH = 16
SEQ = 256
DEPTH = 2
DEC_BATCH = 8
DEC_SEQ = 1024
PAST_LEN = 256
GRID_W = 64
GRID_ROWS = DEC_SEQ // GRID_W
HEAD_DIM = 64
SCALE = HEAD_DIM ** -0.5
A_HEADS = 8
A_KV_HEADS = 2
A_GROUP = A_HEADS // A_KV_HEADS
A_WINDOW = 128
A_BLOCK = 128
A_Q = A_HEADS * HEAD_DIM
A_KV = A_KV_HEADS * HEAD_DIM
A_COLS = A_Q + 2 * A_KV
B_HEADS = 8
B_W = B_HEADS * HEAD_DIM
B_COLS = 3 * B_W
NA_ROWS = 8
NA_COLS = 16
C_HEADS = 8
C_WIDTH = C_HEADS * HEAD_DIM
DECAY_RANK = 64
ICLR_RANK = 64
GATE_RANK = 128
C_COLS = 3 * C_WIDTH + 2 * DECAY_RANK + 2 * ICLR_RANK + GATE_RANK
GN_EPS = 64e-5
GATE_COLS = 3 * D_MODEL
IN_COLS = A_COLS + B_COLS + C_COLS + GATE_COLS
D_FF = 2816
N_MOD = 9
ROPE_BASE = 10000.0
LN_EPS = 1e-5
NEG_INF = -1e30
ALPHA = (2 * DEPTH) ** 0.25

LANES = 128
SUBLANES = 8
TM = 256
MOD_ROWS = 16
SCAN_TB = 32
SCAN_NB = LANES // (2 * C_HEADS)
VMEM_LIMIT = 56 * 1024 * 1024

F32 = jnp.float32
BF16 = jnp.bfloat16

Stream = collections.namedtuple("Stream", "name nb seq n_tok tiles tiles_per_seq mod_row")


def _make_stream(name, nb, seq, mod_row):
    return Stream(name, nb, seq, nb * seq, nb * seq // TM, seq // TM, mod_row)


CTX = _make_stream("ctx", BATCH, SEQ, lambda i: DEC_BATCH)
LAT = _make_stream("lat", DEC_BATCH, DEC_SEQ, lambda i: i // (DEC_SEQ // TM))


def _sigmoid(x):
    return 1.0 / (1.0 + jnp.exp(-x))


def _layer_norm(y, g, b):
    mu = jnp.mean(y, axis=-1, keepdims=True)
    d = y - mu
    var = jnp.mean(d * d, axis=-1, keepdims=True)
    return d * lax.rsqrt(var + LN_EPS) * g + b


def _dot(a, b):
    return jnp.dot(a, b, preferred_element_type=F32)


def _dot_nt(a, b):
    return lax.dot_general(a, b, (((1,), (1,)), ((), ())), preferred_element_type=F32)


def _head_sum(x, same_head):
    hi = x.astype(BF16)
    lo = (x - hi.astype(F32)).astype(BF16)
    slabs = [_dot(hi[:, j:j + LANES], same_head) + _dot(lo[:, j:j + LANES], same_head)
             for j in range(0, x.shape[-1], LANES)]
    return jnp.concatenate(slabs, axis=-1)


def _resident(shape):
    zeros = (0,) * len(shape)
    return pl.BlockSpec(shape, lambda *_: zeros, pipeline_mode=pl.Buffered(1))


def _tile(width, col=0):
    return pl.BlockSpec((TM, width), lambda i: (i, col))


def _mod_spec(st):
    return pl.BlockSpec((1, N_MOD, D_MODEL), lambda i: (st.mod_row(i), 0, 0))


def _params(*sem):
    return pltpu.CompilerParams(dimension_semantics=sem, vmem_limit_bytes=VMEM_LIMIT)


def _ada_kernel(c_ref, w_ref, b_ref, o_ref):
    c = c_ref[...]
    s = (c * _sigmoid(c)).astype(BF16)
    o_ref[0] = _dot(s, w_ref[0].astype(BF16)) + b_ref[0]


def _ada_mods(c_rows, w_ada, b_ada):
    out = pl.pallas_call(
        _ada_kernel,
        grid=(DEPTH, N_MOD),
        in_specs=[
            pl.BlockSpec((MOD_ROWS, D_MODEL), lambda l, j: (0, 0)),
            pl.BlockSpec((1, D_MODEL, D_MODEL), lambda l, j: (l, 0, j)),
            pl.BlockSpec((1, 1, D_MODEL), lambda l, j: (l, 0, j)),
        ],
        out_specs=pl.BlockSpec((1, MOD_ROWS, D_MODEL), lambda l, j: (l, 0, j)),
        out_shape=jax.ShapeDtypeStruct((DEPTH, MOD_ROWS, N_MOD * D_MODEL), F32),
        compiler_params=_params("parallel", "parallel"),
        name="ada_mods",
    )(c_rows, w_ada, b_ada.reshape(DEPTH, 1, N_MOD * D_MODEL))
    return out.reshape(DEPTH, MOD_ROWS, N_MOD, D_MODEL)


def _ffn_kernel(x_ref, mod_ref, w1_ref, w2_ref, g_ref, b_ref, o_ref, *, base):
    x = x_ref[...]
    shift = mod_ref[0, base:base + 1, :]
    scale = mod_ref[0, base + 1:base + 2, :]
    gate = mod_ref[0, base + 2:base + 3, :]
    h = (x * (1.0 + scale) + shift).astype(BF16)
    a = _dot(h, w1_ref[:, :D_FF])
    g = _dot(h, w1_ref[:, D_FF:])
    u = (a * _sigmoid(a) * g).astype(BF16)
    f = _dot(u, w2_ref[...])
    o_ref[...] = _layer_norm(ALPHA * x + 0.5 * gate * f, g_ref[...], b_ref[...])


def _ffn(st, x, mods, w1, w2, ln_g, ln_b, base):
    return pl.pallas_call(
        functools.partial(_ffn_kernel, base=base),
        grid=(st.tiles,),
        in_specs=[
            _tile(D_MODEL), _mod_spec(st),
            _resident((D_MODEL, 2 * D_FF)), _resident((D_FF, D_MODEL)),
            _resident((1, D_MODEL)), _resident((1, D_MODEL)),
        ],
        out_specs=_tile(D_MODEL),
        out_shape=jax.ShapeDtypeStruct((st.n_tok, D_MODEL), F32),
        compiler_params=_params("parallel"),
        name=f"ffn{base // 6 + 1}_{st.name}",
    )(x, mods, w1, w2, ln_g.reshape(1, D_MODEL), ln_b.reshape(1, D_MODEL))


_COL_A = 0
_COL_B = A_COLS
_COL_C = A_COLS + B_COLS
_COL_G = A_COLS + B_COLS + C_COLS


def _softmax_pv(scores, values, sink=None):
    m = scores[0].max(axis=-1, keepdims=True)
    for s in scores[1:]:
        m = jnp.maximum(m, s.max(axis=-1, keepdims=True))
    if sink is not None:
        m = jnp.maximum(m, sink)
    den = jnp.exp(sink - m) if sink is not None else 0.0
    acc = None
    for s, v in zip(scores, values):
        p = jnp.exp(s - m)
        den = den + p.sum(axis=-1, keepdims=True)
        pv = _dot(p.astype(BF16), v)
        acc = pv if acc is None else acc + pv
    return acc / den


def _head(x, h):
    return x[:, h * HEAD_DIM:(h + 1) * HEAD_DIM]


def _ctx_attn_kernel(sink_ref, za_ref, zb_ref, o_ref):
    za = za_ref[...]
    zb = zb_ref[...]
    outs = []
    row = lax.broadcasted_iota(jnp.int32, (A_GROUP * SEQ, 1), 0) // SEQ
    for g in range(A_KV_HEADS):
        q4 = jnp.concatenate([_head(za, A_GROUP * g + j) for j in range(A_GROUP)], axis=0).astype(BF16)
        k = _head(za, A_HEADS + g).astype(BF16)
        v = _head(za, A_HEADS + A_KV_HEADS + g).astype(BF16)
        sink = jnp.zeros((A_GROUP * SEQ, 1), F32)
        for j in range(A_GROUP):
            sink = jnp.where(row == j, sink_ref[A_GROUP * g + j], sink)
        o4 = _softmax_pv([_dot_nt(q4, k) * SCALE], [v], sink)
        outs.extend(o4[j * SEQ:(j + 1) * SEQ] for j in range(A_GROUP))
    for h in range(B_HEADS):
        q = _head(zb, h).astype(BF16)
        k = _head(zb, B_HEADS + h).astype(BF16)
        v = _head(zb, 2 * B_HEADS + h).astype(BF16)
        outs.append(_softmax_pv([_dot_nt(q, k) * SCALE], [v]))
    o_ref[...] = jnp.concatenate(outs, axis=-1)


def _ctx_attn(za, zb, sink):
    return pl.pallas_call(
        _ctx_attn_kernel,
        grid=(BATCH,),
        in_specs=[
            pl.BlockSpec(memory_space=pltpu.SMEM),
            pl.BlockSpec((SEQ, A_COLS), lambda b: (b, 0)),
            pl.BlockSpec((SEQ, B_COLS), lambda b: (b, 0)),
        ],
        out_specs=pl.BlockSpec((SEQ, A_Q + B_W), lambda b: (b, 0)),
        out_shape=jax.ShapeDtypeStruct((CTX.n_tok, A_Q + B_W), F32),
        compiler_params=_params("parallel"),
        name="ctx_attn",
    )(sink, za, zb)


def _rope(x, cos, sin_signed):
    w = x.shape[-1]
    half = HEAD_DIM // 2
    lane = lax.broadcasted_iota(jnp.int32, x.shape, 1)
    partner = jnp.where(lane % HEAD_DIM < half, pltpu.roll(x, w - half, 1), pltpu.roll(x, half, 1))
    return x * cos + partner * sin_signed


def _win_attn_kernel(sink_ref, za_ref, ck_ref, cv_ref, cos_ref, sin_ref, o_ref):
    n = pl.program_id(1)
    nb = DEC_SEQ // A_BLOCK
    q0 = pl.multiple_of(n * A_BLOCK, A_BLOCK)
    q = _rope(za_ref[pl.ds(q0, A_BLOCK), :A_Q], cos_ref[pl.ds(q0, A_BLOCK), :], sin_ref[pl.ds(q0, A_BLOCK), :])
    ks, vs = [], []
    for j in range(3):
        kb = jnp.clip(n - 1 + j, 0, nb - 1)
        k0 = pl.multiple_of(kb * A_BLOCK, A_BLOCK)
        kblk = za_ref[pl.ds(k0, A_BLOCK), A_Q:A_Q + A_KV]
        ks.append(_rope(kblk, cos_ref[pl.ds(k0, A_BLOCK), :A_KV], sin_ref[pl.ds(k0, A_BLOCK), :A_KV]))
        vs.append(za_ref[pl.ds(k0, A_BLOCK), A_Q + A_KV:])
    kband = jnp.concatenate(ks, axis=0).astype(BF16)
    vband = jnp.concatenate(vs, axis=0).astype(BF16)
    ck = ck_ref[0].astype(BF16)
    cv = cv_ref[0].astype(BF16)
    m4 = A_GROUP * A_BLOCK
    qi = lax.broadcasted_iota(jnp.int32, (m4, 3 * A_BLOCK), 0) % A_BLOCK
    kj = lax.broadcasted_iota(jnp.int32, (m4, 3 * A_BLOCK), 1)
    rel = kj - A_BLOCK - qi
    kpos = (n - 1) * A_BLOCK + kj
    valid = (jnp.abs(rel) <= A_WINDOW) & (kpos >= 0) & (kpos < DEC_SEQ)
    row = lax.broadcasted_iota(jnp.int32, (m4, 1), 0) // A_BLOCK
    outs = []
    for g in range(A_KV_HEADS):
        q4 = jnp.concatenate([_head(q, A_GROUP * g + j) for j in range(A_GROUP)], axis=0).astype(BF16)
        s_loc = jnp.where(valid, _dot_nt(q4, _head(kband, g)) * SCALE, NEG_INF)
        s_ctx = _dot_nt(q4, _head(ck, g)) * SCALE
        sink = jnp.zeros((m4, 1), F32)
        for j in range(A_GROUP):
            sink = jnp.where(row == j, sink_ref[A_GROUP * g + j], sink)
        o4 = _softmax_pv([s_loc, s_ctx], [_head(vband, g), _head(cv, g)], sink)
        outs.extend(o4[j * A_BLOCK:(j + 1) * A_BLOCK] for j in range(A_GROUP))
    o_ref[...] = jnp.concatenate(outs, axis=-1)


def _rope_tables():
    t = np.arange(DEC_SEQ)
    n_freq = HEAD_DIM // 4
    inv = ROPE_BASE ** (-jnp.arange(n_freq, dtype=F32) / n_freq)
    rows = jnp.asarray(t // GRID_W, F32)
    cols = jnp.asarray(t % GRID_W, F32)
    ang = jnp.concatenate([rows[:, None] * inv, cols[:, None] * inv], axis=-1)
    cos, sin = jnp.cos(ang), jnp.sin(ang)
    cos_h = jnp.concatenate([cos, cos], axis=-1)
    sin_h = jnp.concatenate([-sin, sin], axis=-1)
    return jnp.tile(cos_h, (1, A_HEADS)), jnp.tile(sin_h, (1, A_HEADS))


def _win_attn(za, ck, cv, sink, cos_t, sin_t):
    nb = DEC_SEQ // A_BLOCK
    return pl.pallas_call(
        _win_attn_kernel,
        grid=(DEC_BATCH, nb),
        in_specs=[
            pl.BlockSpec(memory_space=pltpu.SMEM),
            pl.BlockSpec((DEC_SEQ, A_COLS), lambda b, n: (b, 0)),
            pl.BlockSpec((1, PAST_LEN, A_KV), lambda b, n: (b, 0, 0)),
            pl.BlockSpec((1, PAST_LEN, A_KV), lambda b, n: (b, 0, 0)),
            pl.BlockSpec((DEC_SEQ, A_Q), lambda b, n: (0, 0)),
            pl.BlockSpec((DEC_SEQ, A_Q), lambda b, n: (0, 0)),
        ],
        out_specs=pl.BlockSpec((A_BLOCK, A_Q), lambda b, n: (b * nb + n, 0)),
        out_shape=jax.ShapeDtypeStruct((LAT.n_tok, A_Q), F32),
        compiler_params=_params("parallel", "arbitrary"),
        name="win_attn",
    )(sink, za, ck, cv, cos_t, sin_t)


_NA_QBLK = 256
_RPB_I = 2 * NA_ROWS - 1
_RPB_J = 2 * NA_COLS - 1


def _na_build_bias(rpb_ref, head, bias_ref, slot):
    shape = (GRID_W, 2 * GRID_W)
    qc = lax.broadcasted_iota(jnp.int32, shape, 0)
    lane = lax.broadcasted_iota(jnp.int32, shape, 1)
    kc = lane % GRID_W
    left = lane < GRID_W
    cs = jnp.clip(qc - NA_COLS // 2, 0, GRID_W - NA_COLS)
    idx = jnp.where((kc >= cs) & (kc < cs + NA_COLS), jnp.clip(kc - qc + NA_COLS - 1, 0, _RPB_J - 1), -1)
    base = head * (_RPB_I * _RPB_J)
    tables = []
    for i in range(_RPB_I):
        t = jnp.full(shape, NEG_INF, F32)
        for j in range(_RPB_J):
            t = jnp.where(idx == j, rpb_ref[base + i * _RPB_J + j], t)
        tables.append(t)
    neg = jnp.full(shape, NEG_INF, F32)
    half = NA_ROWS // 2
    for qr in range(GRID_ROWS):
        rs = min(max(qr - half, 0), GRID_ROWS - NA_ROWS)
        for m in range(GRID_ROWS // 2):
            kr0, kr1 = 2 * m, 2 * m + 1
            v0 = rs <= kr0 < rs + NA_ROWS
            v1 = rs <= kr1 < rs + NA_ROWS
            i0 = kr0 - qr + NA_ROWS - 1
            if v0 and v1:
                piece = jnp.where(left, tables[i0], tables[i0 + 1])
            elif v0:
                piece = jnp.where(left, tables[i0], NEG_INF)
            elif v1:
                piece = jnp.where(left, NEG_INF, tables[i0 + 1])
            else:
                piece = neg
            bias_ref[slot, qr * GRID_W:(qr + 1) * GRID_W, m * 2 * GRID_W:(m + 1) * 2 * GRID_W] = piece


def _na_attn_kernel(rpb_ref, q_ref, k_ref, v_ref, ck_ref, cv_ref, o_ref, bias_ref):
    hp = pl.program_id(0)

    @pl.when(pl.program_id(1) == 0)
    def _():
        for half in range(2):
            _na_build_bias(rpb_ref, 2 * hp + half, bias_ref, half)

    ck2 = ck_ref[0].astype(BF16)
    cv2 = cv_ref[0].astype(BF16)
    rows_per_blk = _NA_QBLK // GRID_W
    for qb in range(DEC_SEQ // _NA_QBLK):
        rows = slice(qb * _NA_QBLK, (qb + 1) * _NA_QBLK)
        r_lo = min(max(qb * rows_per_blk - NA_ROWS // 2, 0), GRID_ROWS - NA_ROWS)
        r_hi = min(max((qb + 1) * rows_per_blk - 1 - NA_ROWS // 2, 0), GRID_ROWS - NA_ROWS) + NA_ROWS
        keys = slice(r_lo // 2 * 2 * GRID_W, -(-r_hi // 2) * 2 * GRID_W)
        k2 = k_ref[keys, :].astype(BF16)
        v2 = v_ref[keys, :].astype(BF16)
        q2 = q_ref[rows, :].astype(BF16)
        outs = []
        for half in range(2):
            q = _head(q2, half)
            s_loc = _dot_nt(q, _head(k2, half)) * SCALE + bias_ref[half, rows, keys]
            s_ctx = _dot_nt(q, _head(ck2, half)) * SCALE
            outs.append(_softmax_pv([s_loc, s_ctx], [_head(v2, half), _head(cv2, half)]))
        o_ref[rows, :] = jnp.concatenate(outs, axis=-1)


def _na_attn(zb, ck, cv, rpb_flat):
    pairs = B_HEADS // 2
    pw = 2 * HEAD_DIM
    return pl.pallas_call(
        _na_attn_kernel,
        grid=(pairs, DEC_BATCH),
        in_specs=[
            pl.BlockSpec(memory_space=pltpu.SMEM),
            pl.BlockSpec((DEC_SEQ, pw), lambda hp, b: (b, hp)),
            pl.BlockSpec((DEC_SEQ, pw), lambda hp, b: (b, pairs + hp)),
            pl.BlockSpec((DEC_SEQ, pw), lambda hp, b: (b, 2 * pairs + hp)),
            pl.BlockSpec((1, PAST_LEN, pw), lambda hp, b: (b, 0, hp)),
            pl.BlockSpec((1, PAST_LEN, pw), lambda hp, b: (b, 0, hp)),
        ],
        out_specs=pl.BlockSpec((DEC_SEQ, pw), lambda hp, b: (b, hp)),
        out_shape=jax.ShapeDtypeStruct((LAT.n_tok, B_W), F32),
        scratch_shapes=[pltpu.VMEM((2, DEC_SEQ, DEC_SEQ), F32)],
        compiler_params=_params("arbitrary", "arbitrary"),
        name="na_attn",
    )(rpb_flat, zb, zb, zb, ck, cv)


_C_R, _C_K, _C_V = 0, C_WIDTH, 2 * C_WIDTH
_C_WLO = 3 * C_WIDTH
_C_ALO = _C_WLO + 2 * DECAY_RANK
_C_GLO = _C_ALO + 2 * ICLR_RANK
_HALO = SUBLANES


def _mixer_in_kernel(x_ref, xp_ref, xn_ref, mod_ref, win_ref, wsh_ref, dup_ref, aup_ref, gup_ref, w0_ref, a0_ref,
                     kk_ref, ka_ref, rk_ref, ones_ref,
                     za_ref, zb_ref, sg_ref, r_ref, v_ref, nkk_ref, w_ref, b_ref, k_ref, bonus_ref, g_ref,
                     *, tiles_per_seq):
    j = pl.program_id(0) % tiles_per_seq
    scale = 1.0 + mod_ref[0, 4:5, :]
    shift = mod_ref[0, 3:4, :]
    x = x_ref[...]
    h = (x * scale + shift).astype(BF16)
    za_ref[...] = _dot(h, win_ref[:, _COL_A:_COL_B])
    zb_ref[...] = _dot(h, win_ref[:, _COL_B:_COL_C])
    sg_ref[...] = _sigmoid(_dot(h, win_ref[:, _COL_G:]))

    x_ext = jnp.concatenate([xp_ref[...], x, xn_ref[...]], axis=0)
    z_ext = _dot((x_ext * scale + shift).astype(BF16), win_ref[:, _COL_C:_COL_G])
    row = lax.broadcasted_iota(jnp.int32, (TM, C_COLS), 0)
    z = z_ext[_HALO:_HALO + TM]
    z_prev = jnp.where(jnp.logical_and(row == 0, j == 0), 0.0, z_ext[_HALO - 1:_HALO - 1 + TM])
    z_next = jnp.where(jnp.logical_and(row == TM - 1, j == tiles_per_seq - 1), 0.0, z_ext[_HALO + 1:_HALO + 1 + TM])
    zc = z_prev * wsh_ref[0:1, :] + z * wsh_ref[1:2, :] + z_next * wsh_ref[2:3, :]

    r = zc[:, _C_R:_C_R + C_WIDTH]
    k = zc[:, _C_K:_C_K + C_WIDTH]
    v = zc[:, _C_V:_C_V + C_WIDTH]
    wlo = zc[:, _C_WLO:_C_WLO + 2 * DECAY_RANK]
    alo = zc[:, _C_ALO:_C_ALO + 2 * ICLR_RANK]
    glo = zc[:, _C_GLO:_C_GLO + GATE_RANK]
    ones = ones_ref[...]

    g_ref[...] = _dot(_sigmoid(glo).astype(BF16), gup_ref[...])
    kk = k * kk_ref[...]
    norm = jnp.sqrt(_head_sum(kk * kk, ones))
    kk = kk / jnp.maximum(norm, 1e-12)
    xw = w0_ref[...] + _dot(jnp.tanh(wlo).astype(BF16), dup_ref[...])
    logw = -(jnp.maximum(-xw, 0.0) + jnp.log(1.0 + jnp.exp(-jnp.abs(xw)))) - 0.5
    decay = jnp.exp(-jnp.exp(logw))
    a = _sigmoid(a0_ref[...] + _dot(alo.astype(BF16), aup_ref[...]))
    ka = ka_ref[...]
    kd_sum = None
    for d in range(2):
        a_d = a[:, d * C_WIDTH:(d + 1) * C_WIDTH]
        k_d = k * (1.0 + (a_d - 1.0) * ka)
        w_ref[d] = decay[:, d * C_WIDTH:(d + 1) * C_WIDTH]
        b_ref[d] = kk * a_d
        k_ref[d] = k_d
        kd_sum = k_d if kd_sum is None else kd_sum + k_d
    bonus_ref[...] = _head_sum(r * kd_sum * rk_ref[...], ones) * v
    r_ref[0] = r
    v_ref[0] = v
    nkk_ref[0] = -kk


def _mixer_in(st, x, mods, w_in, w_shift, dec_up, iclr_up, gate_up, w0, a0, k_k, k_a, r_k, ones_bd):
    tiles_per_halo = TM // _HALO
    n_halo = st.n_tok // _HALO
    row = lambda n: _resident((1, n))
    stacked = lambda d: (pl.BlockSpec((d, TM, C_WIDTH), lambda i: (0, i, 0)),
                         jax.ShapeDtypeStruct((d, st.n_tok, C_WIDTH), F32))
    flat = lambda w: (_tile(w), jax.ShapeDtypeStruct((st.n_tok, w), F32))
    outs = [flat(A_COLS), flat(B_COLS), flat(GATE_COLS)] + [stacked(1)] * 3 + [stacked(2)] * 3 + [flat(C_WIDTH)] * 2
    return pl.pallas_call(
        functools.partial(_mixer_in_kernel, tiles_per_seq=st.tiles_per_seq),
        grid=(st.tiles,),
        in_specs=[
            _tile(D_MODEL),
            pl.BlockSpec((_HALO, D_MODEL), lambda i: (jnp.maximum(i * tiles_per_halo - 1, 0), 0)),
            pl.BlockSpec((_HALO, D_MODEL), lambda i: (jnp.minimum((i + 1) * tiles_per_halo, n_halo - 1), 0)),
            _mod_spec(st),
            _resident((D_MODEL, IN_COLS)),
            _resident((3, C_COLS)),
            _resident((2 * DECAY_RANK, 2 * C_WIDTH)),
            _resident((2 * ICLR_RANK, 2 * C_WIDTH)),
            _resident((GATE_RANK, C_WIDTH)),
            row(2 * C_WIDTH), row(2 * C_WIDTH), row(C_WIDTH), row(C_WIDTH), row(C_WIDTH),
            _resident((LANES, LANES)),
        ],
        out_specs=[o[0] for o in outs],
        out_shape=[o[1] for o in outs],
        compiler_params=_params("parallel"),
        name=f"mixer_in_{st.name}",
    )(x, x, x, mods, w_in, w_shift, dec_up, iclr_up, gate_up, w0, a0, k_k, k_a, r_k, ones_bd)


_PB = 2 * SCAN_NB


def _scan_kernel(wf, wb, af, ab, bf, bb, kf, kb, rf, rb, vf, vb, s0_ref, yf_ref, yb_ref, state,
                 ops_a, ops_b, red, wcum, y_a, y_b):
    @pl.when(pl.program_id(1) == 0)
    def _():
        state[...] = s0_ref[...]
        y_b[...] = jnp.zeros_like(y_b)

    wcum[...] = jnp.ones_like(wcum)
    left = lax.broadcasted_iota(jnp.int32, (_PB, LANES), 1) < HEAD_DIM
    low = lax.broadcasted_iota(jnp.int32, (SUBLANES, LANES), 0) < SUBLANES // 2

    def load_ops(i, ops):
        ir = SCAN_TB - 1 - i
        for p, (pf, pb, qf, qb) in enumerate(((wf, wb, af, ab), (bf, bb, kf, kb), (rf, rb, vf, vb))):
            xp = jnp.concatenate([pf[0, :, i, :], pb[0, :, ir, :]], axis=0)
            xq = jnp.concatenate([qf[0, :, i, :], qb[0, :, ir, :]], axis=0)
            blocks = []
            for j in range(C_HEADS // 2):
                pv = xp[:, j * LANES:(j + 1) * LANES]
                qv = xq[:, j * LANES:(j + 1) * LANES]
                blocks.append(jnp.where(left, pv, pltpu.roll(qv, HEAD_DIM, 1)))
                blocks.append(jnp.where(left, pltpu.roll(pv, HEAD_DIM, 1), qv))
            ops[p] = jnp.concatenate(blocks, axis=0).T

    def fold(x):
        return x.reshape(HEAD_DIM // SUBLANES, SUBLANES, LANES).sum(axis=0)

    def compute(ops, ybuf):
        w_prev = wcum[...]
        w_now = w_prev * ops[0, 0:HEAD_DIM]
        wcum[...] = w_now
        w_inv = 1.0 / w_now
        a = ops[0, HEAD_DIM:] * w_prev
        b = ops[1, 0:HEAD_DIM]
        k = ops[1, HEAD_DIM:]
        r = ops[2, 0:HEAD_DIM]
        br = jnp.sum(b * r, axis=0, keepdims=True)
        kr = jnp.sum(k * r, axis=0, keepdims=True)
        b = b * w_inv
        k = k * w_inv
        r = r * w_now
        for vi in range(HEAD_DIM):
            z = state[0, vi]
            p1 = fold(z * a)
            p2 = fold(z * r)
            t = jnp.where(low, p1, p2) + pltpu.roll(jnp.where(low, p2, p1), SUBLANES // 2, 0)
            t = t + pltpu.roll(t, SUBLANES - 2, 0)
            red[vi] = t + pltpu.roll(t, SUBLANES - 1, 0)
        for vi in range(HEAD_DIM):
            state[0, vi] = state[0, vi] + red[vi, 0:1] * b + ops[2, HEAD_DIM + vi:HEAD_DIM + vi + 1] * k
        ybuf[...] = red[:, SUBLANES // 2, :] + red[:, 0, :] * br + ops[2, HEAD_DIM:] * kr

    def emit(i, ybuf):
        y = ybuf[...]
        t = jnp.concatenate([y, y], axis=0).T
        cols = [jnp.where(left, t[(2 * j) * _PB:(2 * j + 1) * _PB], t[(2 * j + 1) * _PB:(2 * j + 2) * _PB])
                for j in range(C_HEADS // 2)]
        out = jnp.concatenate(cols, axis=1)
        yf_ref[:, i, :] = out[:SCAN_NB]
        yb_ref[:, SCAN_TB - 1 - i, :] = out[SCAN_NB:]

    load_ops(0, ops_a)

    def body(m, carry):
        i0 = 2 * m
        load_ops(i0 + 1, ops_b)
        compute(ops_a, y_a)
        emit(jnp.maximum(i0 - 1, 0), y_b)
        load_ops(jnp.minimum(i0 + 2, SCAN_TB - 1), ops_a)
        compute(ops_b, y_b)
        emit(i0, y_a)
        return carry

    lax.fori_loop(0, SCAN_TB // 2, body, 0)
    emit(SCAN_TB - 1, y_b)
    w_end = wcum[...]
    for vi in range(HEAD_DIM):
        state[0, vi] = state[0, vi] * w_end


def _scan(st, w, b, k, nkk, r, v, s0):
    groups = st.nb // SCAN_NB
    n_t = st.seq // SCAN_TB
    view = lambda x: x.reshape(x.shape[0], st.nb, st.seq, C_WIDTH)
    blk = (1, SCAN_NB, SCAN_TB, C_WIDTH)
    fwd = lambda d: pl.BlockSpec(blk, lambda g, t: (d, g, t, 0))
    bwd = lambda d: pl.BlockSpec(blk, lambda g, t: (d, g, n_t - 1 - t, 0))
    st_blk = (1, HEAD_DIM, HEAD_DIM, LANES)
    w, b, k, nkk, r, v = (view(x) for x in (w, b, k, nkk, r, v))
    tile = pltpu.VMEM((3, LANES, LANES), F32)
    ybuf = pltpu.VMEM((HEAD_DIM, LANES), F32)
    yblk = (SCAN_NB, SCAN_TB, C_WIDTH)
    y_shape = jax.ShapeDtypeStruct((st.nb, st.seq, C_WIDTH), F32)
    yf, yb, s_t = pl.pallas_call(
        _scan_kernel,
        grid=(groups, n_t),
        in_specs=[fwd(0), bwd(1), fwd(0), bwd(0), fwd(0), bwd(1), fwd(0), bwd(1), fwd(0), bwd(0), fwd(0), bwd(0),
                  pl.BlockSpec(st_blk, lambda g, t: (g, 0, 0, 0), pipeline_mode=pl.Buffered(1))],
        out_specs=[pl.BlockSpec(yblk, lambda g, t: (g, t, 0)), pl.BlockSpec(yblk, lambda g, t: (g, n_t - 1 - t, 0)),
                   pl.BlockSpec(st_blk, lambda g, t: (g, 0, 0, 0))],
        out_shape=[y_shape, y_shape, jax.ShapeDtypeStruct((groups,) + st_blk[1:], F32)],
        scratch_shapes=[tile, tile, pltpu.VMEM((HEAD_DIM, SUBLANES, LANES), F32), ybuf, ybuf, ybuf],
        compiler_params=_params("arbitrary", "arbitrary"),
        name=f"rwkv_scan_{st.name}",
    )(w, w, nkk, nkk, b, b, k, k, r, r, v, v, s0)
    return yf.reshape(st.n_tok, C_WIDTH), yb.reshape(st.n_tok, C_WIDTH), s_t


def _merge_ffn_kernel(x_ref, mod_ref, oa_ref, ob_ref, yf_ref, yb_ref, bonus_ref, g_ref, sg_ref, pa_ref, pb_ref, pc_ref,
                      wo_ref, ones_ref, gng_ref, gnb_ref, lng_ref, lnb_ref, w1_ref, w2_ref, lng2_ref, lnb2_ref, o_ref):
    y = yf_ref[...] + yb_ref[...]
    ones = ones_ref[...]
    mu = _head_sum(y, ones) * (1.0 / HEAD_DIM)
    d = y - mu
    var = _head_sum(d * d, ones) * (1.0 / HEAD_DIM)
    yn = d * lax.rsqrt(var + GN_EPS) * gng_ref[...] + gnb_ref[...]
    oc = ((yn + bonus_ref[...]) * g_ref[...]).astype(BF16)
    merged = (sg_ref[:, :D_MODEL] * _dot(oa_ref[...].astype(BF16), pa_ref[...])
              + sg_ref[:, D_MODEL:2 * D_MODEL] * _dot(ob_ref[...].astype(BF16), pb_ref[...])
              + sg_ref[:, 2 * D_MODEL:] * _dot(oc, pc_ref[...]))
    o = _dot(merged.astype(BF16), wo_ref[...])
    x = _layer_norm(ALPHA * x_ref[...] + mod_ref[0, 5:6, :] * o, lng_ref[...], lnb_ref[...])

    h = (x * (1.0 + mod_ref[0, 7:8, :]) + mod_ref[0, 6:7, :]).astype(BF16)
    a = _dot(h, w1_ref[:, :D_FF])
    g = _dot(h, w1_ref[:, D_FF:])
    u = (a * _sigmoid(a) * g).astype(BF16)
    f = _dot(u, w2_ref[...])
    o_ref[...] = _layer_norm(ALPHA * x + 0.5 * mod_ref[0, 8:9, :] * f, lng2_ref[...], lnb2_ref[...])


def _merge_ffn(st, x, mods, oa, oa_col, ob, ob_col, yf, yb, bonus, g, sg, pa, pb, pc, wo, ones, gn_g, gn_b, ln_g, ln_b,
               w1, w2, ln_g2, ln_b2):
    vec = lambda a, n: a.reshape(1, n)
    return pl.pallas_call(
        _merge_ffn_kernel,
        grid=(st.tiles,),
        in_specs=[
            _tile(D_MODEL), _mod_spec(st), _tile(A_Q, oa_col), _tile(B_W, ob_col), _tile(C_WIDTH), _tile(C_WIDTH),
            _tile(C_WIDTH), _tile(C_WIDTH), _tile(GATE_COLS),
            _resident((A_Q, D_MODEL)), _resident((B_W, D_MODEL)), _resident((C_WIDTH, D_MODEL)),
            _resident((D_MODEL, D_MODEL)), _resident((LANES, LANES)),
            _resident((1, C_WIDTH)), _resident((1, C_WIDTH)), _resident((1, D_MODEL)), _resident((1, D_MODEL)),
            _resident((D_MODEL, 2 * D_FF)), _resident((D_FF, D_MODEL)), _resident((1, D_MODEL)), _resident((1, D_MODEL)),
        ],
        out_specs=_tile(D_MODEL),
        out_shape=jax.ShapeDtypeStruct((st.n_tok, D_MODEL), F32),
        compiler_params=_params("parallel"),
        name=f"merge_ffn2_{st.name}",
    )(x, mods, oa, ob, yf, yb, bonus, g, sg, pa, pb, pc, wo, ones,
      vec(gn_g, C_WIDTH), vec(gn_b, C_WIDTH), vec(ln_g, D_MODEL), vec(ln_b, D_MODEL),
      w1, w2, vec(ln_g2, D_MODEL), vec(ln_b2, D_MODEL))


def _block_diag2(a, b):
    za = jnp.zeros((a.shape[0], b.shape[1]), a.dtype)
    zb = jnp.zeros((b.shape[0], a.shape[1]), b.dtype)
    return jnp.concatenate([jnp.concatenate([a, za], axis=1), jnp.concatenate([zb, b], axis=1)], axis=0)


def kernel(x_prompt, x_sample, cache_attn_k, cache_attn_v, cache_na_k, cache_na_v, state_rwkv, c, c_ctx, w_ada, b_ada, ffn1_w_in, ffn1_w_out, ffn2_w_in, ffn2_w_out, w_in, w_shift, attn_sink, na_rpb, decay_w0, decay_up, iclr_a0, iclr_up, gate_up, k_k, k_a, r_k, gn_g, gn_b, proj_a, proj_b, proj_c, w_out, ln_g, ln_b):
    xs = {CTX: x_prompt.reshape(CTX.n_tok, D_MODEL), LAT: x_sample.reshape(LAT.n_tok, D_MODEL)}
    c_rows = jnp.zeros((MOD_ROWS, D_MODEL), F32).at[:DEC_BATCH].set(c).at[DEC_BATCH].set(c_ctx)
    mods_all = _ada_mods(c_rows, w_ada, b_ada)
    cos_t, sin_t = _rope_tables()
    head_of = np.arange(LANES) // HEAD_DIM
    ones_bd = jnp.asarray(head_of[:, None] == head_of[None, :], BF16)
    ctx_groups = BATCH // SCAN_NB
    zero_state = jnp.zeros((ctx_groups, HEAD_DIM, HEAD_DIM, LANES), F32)

    new_ka, new_va, new_kb, new_vb, new_st = [], [], [], [], []
    for l in range(DEPTH):
        mods = mods_all[l]
        bf = lambda w: w[l].astype(BF16)
        ffn1 = (bf(ffn1_w_in), bf(ffn1_w_out), ln_g[l, 0], ln_b[l, 0], 0)
        ffn2 = (bf(ffn2_w_in), bf(ffn2_w_out), ln_g[l, 2], ln_b[l, 2], 6)
        w_in_l = bf(w_in)
        prep_w = (w_shift[l],
                  _block_diag2(decay_up[l, 0], decay_up[l, 1]).astype(BF16),
                  _block_diag2(iclr_up[l, 0], iclr_up[l, 1]).astype(BF16),
                  bf(gate_up),
                  decay_w0[l].reshape(1, 2 * C_WIDTH), iclr_a0[l].reshape(1, 2 * C_WIDTH),
                  k_k[l].reshape(1, C_WIDTH), k_a[l].reshape(1, C_WIDTH), r_k[l].reshape(1, C_WIDTH), ones_bd)
        merge_w = (bf(proj_a), bf(proj_b), bf(proj_c), bf(w_out), ones_bd, gn_g[l], gn_b[l], ln_g[l, 1], ln_b[l, 1])
        s0_lat = state_rwkv[:, l].transpose(3, 4, 2, 1, 0).reshape(1, HEAD_DIM, HEAD_DIM, LANES)

        for st in (CTX, LAT):
            x = _ffn(st, xs[st], mods, *ffn1)
            za, zb, sg, r, v, nkk, w, b, k, bonus, g = _mixer_in(st, x, mods, w_in_l, *prep_w)
            if st is CTX:
                o_ab = _ctx_attn(za, zb, attn_sink[l])
                attn = (o_ab, 0, o_ab, 1)
                zac = za.reshape(BATCH, SEQ, A_COLS)
                zbc = zb.reshape(BATCH, SEQ, B_COLS)
                new_ka.append(zac[..., A_Q:A_Q + A_KV].reshape(BATCH, SEQ, A_KV_HEADS, HEAD_DIM))
                new_va.append(zac[..., A_Q + A_KV:].reshape(BATCH, SEQ, A_KV_HEADS, HEAD_DIM))
                new_kb.append(zbc[..., B_W:2 * B_W].reshape(BATCH, SEQ, B_HEADS, HEAD_DIM))
                new_vb.append(zbc[..., 2 * B_W:].reshape(BATCH, SEQ, B_HEADS, HEAD_DIM))
            else:
                oa = _win_attn(za, cache_attn_k[:, l].reshape(DEC_BATCH, PAST_LEN, A_KV),
                               cache_attn_v[:, l].reshape(DEC_BATCH, PAST_LEN, A_KV), attn_sink[l], cos_t, sin_t)
                ob = _na_attn(zb, cache_na_k[:, l].reshape(DEC_BATCH, PAST_LEN, B_W),
                              cache_na_v[:, l].reshape(DEC_BATCH, PAST_LEN, B_W), na_rpb[l].reshape(-1))
                attn = (oa, 0, ob, 0)
            yf, yb, s_t = _scan(st, w, b, k, nkk, r, v, zero_state if st is CTX else s0_lat)
            if st is CTX:
                s_t = s_t.reshape(ctx_groups, HEAD_DIM, HEAD_DIM, C_HEADS, 2, SCAN_NB)
                new_st.append(s_t.transpose(0, 5, 4, 3, 1, 2).reshape(BATCH, 2, C_HEADS, HEAD_DIM, HEAD_DIM))
            xs[st] = _merge_ffn(st, x, mods, *attn, yf, yb, bonus, g, sg, *merge_w, *ffn2[:4])

    y_p = xs[CTX].reshape(BATCH, SEQ, D_MODEL)
    y_s = xs[LAT].reshape(DEC_BATCH, DEC_SEQ, D_MODEL)
    return (y_p, y_s, jnp.stack(new_ka, axis=1), jnp.stack(new_va, axis=1), jnp.stack(new_kb, axis=1),
            jnp.stack(new_vb, axis=1), jnp.stack(new_st, axis=1))
```

```python
import collections
import functools

import jax
import jax.numpy as jnp
import numpy as np
from jax import lax
from jax.experimental import pallas as pl
from jax.experimental.pallas import tpu as pltpu

D_MODEL = 1024
BATCH = 16
SEQ = 256
DEPTH = 2
DEC_BATCH = 8
DEC_SEQ = 1024
PAST_LEN = 256
GRID_W = 64
GRID_ROWS = DEC_SEQ // GRID_W
HEAD_DIM = 64
SCALE = HEAD_DIM ** -0.5
A_HEADS = 8
A_KV_HEADS = 2
A_GROUP = A_HEADS // A_KV_HEADS
A_WINDOW = 128
A_BLOCK = 128
A_Q = A_HEADS * HEAD_DIM
A_KV = A_KV_HEADS * HEAD_DIM
A_COLS = A_Q + 2 * A_KV
B_HEADS = 8
B_W = B_HEADS * HEAD_DIM
B_COLS = 3 * B_W
NA_ROWS = 8
NA_COLS = 16
C_HEADS = 8
C_WIDTH = C_HEADS * HEAD_DIM
DECAY_RANK = 64
ICLR_RANK = 64
GATE_RANK = 128
C_COLS = 3 * C_WIDTH + 2 * DECAY_RANK + 2 * ICLR_RANK + GATE_RANK
GN_EPS = 64e-5
GATE_COLS = 3 * D_MODEL
IN_COLS = A_COLS + B_COLS + C_COLS + GATE_COLS
D_FF = 2816
N_MOD = 9
ROPE_BASE = 10000.0
LN_EPS = 1e-5
NEG_INF = -1e30
ALPHA = (2 * DEPTH) ** 0.25

LANES = 128
SUBLANES = 8
TM = 256
MOD_ROWS = 16
SCAN_TB = 32
SCAN_NB = LANES // (2 * C_HEADS)
VMEM_LIMIT = 56 * 1024 * 1024

F32 = jnp.float32
BF16 = jnp.bfloat16

Stream = collections.namedtuple("Stream", "name nb seq n_tok tiles tiles_per_seq mod_row")


def _make_stream(name, nb, seq, mod_row):
    return Stream(name, nb, seq, nb * seq, nb * seq // TM, seq // TM, mod_row)


CTX = _make_stream("ctx", BATCH, SEQ, lambda i: DEC_BATCH)
LAT = _make_stream("lat", DEC_BATCH, DEC_SEQ, lambda i: i // (DEC_SEQ // TM))


def _sigmoid(x):
    return 1.0 / (1.0 + jnp.exp(-x))


def _layer_norm(y, g, b):
    mu = jnp.mean(y, axis=-1, keepdims=True)
    d = y - mu
    var = jnp.mean(d * d, axis=-1, keepdims=True)
    return d * lax.rsqrt(var + LN_EPS) * g + b


def _dot(a, b):
    return jnp.dot(a, b, preferred_element_type=F32)


def _dot_nt(a, b):
    return lax.dot_general(a, b, (((1,), (1,)), ((), ())), preferred_element_type=F32)


def _head_sum(x, same_head):
    hi = x.astype(BF16)
    lo = (x - hi.astype(F32)).astype(BF16)
    slabs = [_dot(hi[:, j:j + LANES], same_head) + _dot(lo[:, j:j + LANES], same_head)
             for j in range(0, x.shape[-1], LANES)]
    return jnp.concatenate(slabs, axis=-1)


def _resident(shape, layer=None):
    zeros = (0,) * len(shape)
    if layer is None:
        return pl.BlockSpec(shape, lambda *_: zeros, pipeline_mode=pl.Buffered(1))
    return pl.BlockSpec((None,) + tuple(shape), lambda *_: (layer,) + zeros, pipeline_mode=pl.Buffered(1))


def _tile(width, col=0):
    return pl.BlockSpec((TM, width), lambda i: (i, col))


def _mod_spec(st, layer):
    return pl.BlockSpec((None, 1, N_MOD, D_MODEL), lambda i: (layer, st.mod_row(i), 0, 0))


def _params(*sem):
    return pltpu.CompilerParams(dimension_semantics=sem, vmem_limit_bytes=VMEM_LIMIT)


def _ada_kernel(c_ref, w_ref, b_ref, o_ref):
    c = c_ref[...]
    s = (c * _sigmoid(c)).astype(BF16)
    o_ref[0] = _dot(s, w_ref[0].astype(BF16)) + b_ref[0]


def _ada_mods(c_rows, w_ada, b_ada):
    out = pl.pallas_call(
        _ada_kernel,
        grid=(DEPTH, N_MOD),
        in_specs=[
            pl.BlockSpec((MOD_ROWS, D_MODEL), lambda l, j: (0, 0)),
            pl.BlockSpec((1, D_MODEL, D_MODEL), lambda l, j: (l, 0, j)),
            pl.BlockSpec((1, 1, D_MODEL), lambda l, j: (l, 0, j)),
        ],
        out_specs=pl.BlockSpec((1, MOD_ROWS, D_MODEL), lambda l, j: (l, 0, j)),
        out_shape=jax.ShapeDtypeStruct((DEPTH, MOD_ROWS, N_MOD * D_MODEL), F32),
        compiler_params=_params("parallel", "parallel"),
        name="ada_mods",
    )(c_rows, w_ada, b_ada.reshape(DEPTH, 1, N_MOD * D_MODEL))
    return out.reshape(DEPTH, MOD_ROWS, N_MOD, D_MODEL)


def _ffn_kernel(x_ref, mod_ref, w1_ref, w2_ref, g_ref, b_ref, o_ref, *, base):
    x = x_ref[...]
    shift = mod_ref[0, base:base + 1, :]
    scale = mod_ref[0, base + 1:base + 2, :]
    gate = mod_ref[0, base + 2:base + 3, :]
    h = (x * (1.0 + scale) + shift).astype(BF16)
    a = _dot(h, w1_ref[:, :D_FF])
    g = _dot(h, w1_ref[:, D_FF:])
    u = (a * _sigmoid(a) * g).astype(BF16)
    f = _dot(u, w2_ref[...])
    o_ref[...] = _layer_norm(ALPHA * x + 0.5 * gate * f, g_ref[...], b_ref[...])


def _ffn(st, layer, x, mods, w1, w2, ln_g, ln_b, base):
    return pl.pallas_call(
        functools.partial(_ffn_kernel, base=base),
        grid=(st.tiles,),
        in_specs=[
            _tile(D_MODEL), _mod_spec(st, layer),
            _resident((D_MODEL, 2 * D_FF), layer), _resident((D_FF, D_MODEL), layer),
            _resident((1, D_MODEL)), _resident((1, D_MODEL)),
        ],
        out_specs=_tile(D_MODEL),
        out_shape=jax.ShapeDtypeStruct((st.n_tok, D_MODEL), F32),
        compiler_params=_params("parallel"),
        name=f"ffn{base // 6 + 1}_{st.name}",
    )(x, mods, w1, w2, ln_g.reshape(1, D_MODEL), ln_b.reshape(1, D_MODEL))


_COL_A = 0
_COL_B = A_COLS
_COL_C = A_COLS + B_COLS
_COL_G = A_COLS + B_COLS + C_COLS


def _softmax_pv(scores, values, sink=None):
    m = scores[0].max(axis=-1, keepdims=True)
    for s in scores[1:]:
        m = jnp.maximum(m, s.max(axis=-1, keepdims=True))
    if sink is not None:
        m = jnp.maximum(m, sink)
    den = jnp.exp(sink - m) if sink is not None else 0.0
    acc = None
    for s, v in zip(scores, values):
        p = jnp.exp(s - m)
        den = den + p.sum(axis=-1, keepdims=True)
        pv = _dot(p.astype(BF16), v)
        acc = pv if acc is None else acc + pv
    return acc / den


def _head(x, h):
    return x[:, h * HEAD_DIM:(h + 1) * HEAD_DIM]


def _ctx_attn_kernel(sink_ref, za_ref, zb_ref, o_ref):
    za = za_ref[...]
    zb = zb_ref[...]
    outs = []
    row = lax.broadcasted_iota(jnp.int32, (A_GROUP * SEQ, 1), 0) // SEQ
    for g in range(A_KV_HEADS):
        q4 = jnp.concatenate([_head(za, A_GROUP * g + j) for j in range(A_GROUP)], axis=0).astype(BF16)
        k = _head(za, A_HEADS + g).astype(BF16)
        v = _head(za, A_HEADS + A_KV_HEADS + g).astype(BF16)
        sink = jnp.zeros((A_GROUP * SEQ, 1), F32)
        for j in range(A_GROUP):
            sink = jnp.where(row == j, sink_ref[A_GROUP * g + j], sink)
        o4 = _softmax_pv([_dot_nt(q4, k) * SCALE], [v], sink)
        outs.extend(o4[j * SEQ:(j + 1) * SEQ] for j in range(A_GROUP))
    for h in range(B_HEADS):
        q = _head(zb, h).astype(BF16)
        k = _head(zb, B_HEADS + h).astype(BF16)
        v = _head(zb, 2 * B_HEADS + h).astype(BF16)
        outs.append(_softmax_pv([_dot_nt(q, k) * SCALE], [v]))
    o_ref[...] = jnp.concatenate(outs, axis=-1)


def _ctx_attn(za, zb, sink):
    return pl.pallas_call(
        _ctx_attn_kernel,
        grid=(BATCH,),
        in_specs=[
            pl.BlockSpec(memory_space=pltpu.SMEM),
            pl.BlockSpec((SEQ, A_COLS), lambda b: (b, 0)),
            pl.BlockSpec((SEQ, B_COLS), lambda b: (b, 0)),
        ],
        out_specs=pl.BlockSpec((SEQ, A_Q + B_W), lambda b: (b, 0)),
        out_shape=jax.ShapeDtypeStruct((CTX.n_tok, A_Q + B_W), F32),
        compiler_params=_params("parallel"),
        name="ctx_attn",
    )(sink, za, zb)


def _rope(x, cos, sin_signed):
    w = x.shape[-1]
    half = HEAD_DIM // 2
    lane = lax.broadcasted_iota(jnp.int32, x.shape, 1)
    partner = jnp.where(lane % HEAD_DIM < half, pltpu.roll(x, w - half, 1), pltpu.roll(x, half, 1))
    return x * cos + partner * sin_signed


def _win_attn_kernel(sink_ref, za_ref, ck_ref, cv_ref, cos_ref, sin_ref, o_ref):
    n = pl.program_id(1)
    nb = DEC_SEQ // A_BLOCK
    q0 = pl.multiple_of(n * A_BLOCK, A_BLOCK)
    q = _rope(za_ref[pl.ds(q0, A_BLOCK), :A_Q], cos_ref[pl.ds(q0, A_BLOCK), :], sin_ref[pl.ds(q0, A_BLOCK), :])
    ks, vs = [], []
    for j in range(3):
        kb = jnp.clip(n - 1 + j, 0, nb - 1)
        k0 = pl.multiple_of(kb * A_BLOCK, A_BLOCK)
        kblk = za_ref[pl.ds(k0, A_BLOCK), A_Q:A_Q + A_KV]
        ks.append(_rope(kblk, cos_ref[pl.ds(k0, A_BLOCK), :A_KV], sin_ref[pl.ds(k0, A_BLOCK), :A_KV]))
        vs.append(za_ref[pl.ds(k0, A_BLOCK), A_Q + A_KV:])
    kband = jnp.concatenate(ks, axis=0).astype(BF16)
    vband = jnp.concatenate(vs, axis=0).astype(BF16)
    ck = ck_ref[0].astype(BF16)
    cv = cv_ref[0].astype(BF16)
    m4 = A_GROUP * A_BLOCK
    qi = lax.broadcasted_iota(jnp.int32, (m4, 3 * A_BLOCK), 0) % A_BLOCK
    kj = lax.broadcasted_iota(jnp.int32, (m4, 3 * A_BLOCK), 1)
    rel = kj - A_BLOCK - qi
    kpos = (n - 1) * A_BLOCK + kj
    valid = (jnp.abs(rel) <= A_WINDOW) & (kpos >= 0) & (kpos < DEC_SEQ)
    row = lax.broadcasted_iota(jnp.int32, (m4, 1), 0) // A_BLOCK
    outs = []
    for g in range(A_KV_HEADS):
        q4 = jnp.concatenate([_head(q, A_GROUP * g + j) for j in range(A_GROUP)], axis=0).astype(BF16)
        s_loc = jnp.where(valid, _dot_nt(q4, _head(kband, g)) * SCALE, NEG_INF)
        s_ctx = _dot_nt(q4, _head(ck, g)) * SCALE
        sink = jnp.zeros((m4, 1), F32)
        for j in range(A_GROUP):
            sink = jnp.where(row == j, sink_ref[A_GROUP * g + j], sink)
        o4 = _softmax_pv([s_loc, s_ctx], [_head(vband, g), _head(cv, g)], sink)
        outs.extend(o4[j * A_BLOCK:(j + 1) * A_BLOCK] for j in range(A_GROUP))
    o_ref[...] = jnp.concatenate(outs, axis=-1)


def _rope_tables():
    t = np.arange(DEC_SEQ)
    n_freq = HEAD_DIM // 4
    inv = ROPE_BASE ** (-jnp.arange(n_freq, dtype=F32) / n_freq)
    rows = jnp.asarray(t // GRID_W, F32)
    cols = jnp.asarray(t % GRID_W, F32)
    ang = jnp.concatenate([rows[:, None] * inv, cols[:, None] * inv], axis=-1)
    cos, sin = jnp.cos(ang), jnp.sin(ang)
    cos_h = jnp.concatenate([cos, cos], axis=-1)
    sin_h = jnp.concatenate([-sin, sin], axis=-1)
    return jnp.tile(cos_h, (1, A_HEADS)), jnp.tile(sin_h, (1, A_HEADS))


def _win_attn(za, ck, cv, sink, cos_t, sin_t):
    nb = DEC_SEQ // A_BLOCK
    return pl.pallas_call(
        _win_attn_kernel,
        grid=(DEC_BATCH, nb),
        in_specs=[
            pl.BlockSpec(memory_space=pltpu.SMEM),
            pl.BlockSpec((DEC_SEQ, A_COLS), lambda b, n: (b, 0)),
            pl.BlockSpec((1, PAST_LEN, A_KV), lambda b, n: (b, 0, 0)),
            pl.BlockSpec((1, PAST_LEN, A_KV), lambda b, n: (b, 0, 0)),
            pl.BlockSpec((DEC_SEQ, A_Q), lambda b, n: (0, 0)),
            pl.BlockSpec((DEC_SEQ, A_Q), lambda b, n: (0, 0)),
        ],
        out_specs=pl.BlockSpec((A_BLOCK, A_Q), lambda b, n: (b * nb + n, 0)),
        out_shape=jax.ShapeDtypeStruct((LAT.n_tok, A_Q), F32),
        compiler_params=_params("parallel", "arbitrary"),
        name="win_attn",
    )(sink, za, ck, cv, cos_t, sin_t)


_NA_QBLK = 256
_RPB_I = 2 * NA_ROWS - 1
_RPB_J = 2 * NA_COLS - 1


def _na_build_bias(rpb_ref, head, bias_ref, slot):
    shape = (GRID_W, 2 * GRID_W)
    qc = lax.broadcasted_iota(jnp.int32, shape, 0)
    lane = lax.broadcasted_iota(jnp.int32, shape, 1)
    kc = lane % GRID_W
    left = lane < GRID_W
    cs = jnp.clip(qc - NA_COLS // 2, 0, GRID_W - NA_COLS)
    idx = jnp.where((kc >= cs) & (kc < cs + NA_COLS), jnp.clip(kc - qc + NA_COLS - 1, 0, _RPB_J - 1), -1)
    base = head * (_RPB_I * _RPB_J)
    tables = []
    for i in range(_RPB_I):
        t = jnp.full(shape, NEG_INF, F32)
        for j in range(_RPB_J):
            t = jnp.where(idx == j, rpb_ref[base + i * _RPB_J + j], t)
        tables.append(t)
    neg = jnp.full(shape, NEG_INF, F32)
    half = NA_ROWS // 2
    for qr in range(GRID_ROWS):
        rs = min(max(qr - half, 0), GRID_ROWS - NA_ROWS)
        for m in range(GRID_ROWS // 2):
            kr0, kr1 = 2 * m, 2 * m + 1
            v0 = rs <= kr0 < rs + NA_ROWS
            v1 = rs <= kr1 < rs + NA_ROWS
            i0 = kr0 - qr + NA_ROWS - 1
            if v0 and v1:
                piece = jnp.where(left, tables[i0], tables[i0 + 1])
            elif v0:
                piece = jnp.where(left, tables[i0], NEG_INF)
            elif v1:
                piece = jnp.where(left, NEG_INF, tables[i0 + 1])
            else:
                piece = neg
            bias_ref[slot, qr * GRID_W:(qr + 1) * GRID_W, m * 2 * GRID_W:(m + 1) * 2 * GRID_W] = piece


def _na_attn_kernel(rpb_ref, q_ref, k_ref, v_ref, ck_ref, cv_ref, o_ref, bias_ref):
    hp = pl.program_id(0)

    @pl.when(pl.program_id(1) == 0)
    def _():
        for half in range(2):
            _na_build_bias(rpb_ref, 2 * hp + half, bias_ref, half)

    ck2 = ck_ref[0].astype(BF16)
    cv2 = cv_ref[0].astype(BF16)
    rows_per_blk = _NA_QBLK // GRID_W
    for qb in range(DEC_SEQ // _NA_QBLK):
        rows = slice(qb * _NA_QBLK, (qb + 1) * _NA_QBLK)
        r_lo = min(max(qb * rows_per_blk - NA_ROWS // 2, 0), GRID_ROWS - NA_ROWS)
        r_hi = min(max((qb + 1) * rows_per_blk - 1 - NA_ROWS // 2, 0), GRID_ROWS - NA_ROWS) + NA_ROWS
        keys = slice(r_lo // 2 * 2 * GRID_W, -(-r_hi // 2) * 2 * GRID_W)
        k2 = k_ref[keys, :].astype(BF16)
        v2 = v_ref[keys, :].astype(BF16)
        q2 = q_ref[rows, :].astype(BF16)
        outs = []
        for half in range(2):
            q = _head(q2, half)
            s_loc = _dot_nt(q, _head(k2, half)) * SCALE + bias_ref[half, rows, keys]
            s_ctx = _dot_nt(q, _head(ck2, half)) * SCALE
            outs.append(_softmax_pv([s_loc, s_ctx], [_head(v2, half), _head(cv2, half)]))
        o_ref[rows, :] = jnp.concatenate(outs, axis=-1)


def _na_attn(zb, ck, cv, rpb_flat):
    pairs = B_HEADS // 2
    pw = 2 * HEAD_DIM
    return pl.pallas_call(
        _na_attn_kernel,
        grid=(pairs, DEC_BATCH),
        in_specs=[
            pl.BlockSpec(memory_space=pltpu.SMEM),
            pl.BlockSpec((DEC_SEQ, pw), lambda hp, b: (b, hp)),
            pl.BlockSpec((DEC_SEQ, pw), lambda hp, b: (b, pairs + hp)),
            pl.BlockSpec((DEC_SEQ, pw), lambda hp, b: (b, 2 * pairs + hp)),
            pl.BlockSpec((1, PAST_LEN, pw), lambda hp, b: (b, 0, hp)),
            pl.BlockSpec((1, PAST_LEN, pw), lambda hp, b: (b, 0, hp)),
        ],
        out_specs=pl.BlockSpec((DEC_SEQ, pw), lambda hp, b: (b, hp)),
        out_shape=jax.ShapeDtypeStruct((LAT.n_tok, B_W), F32),
        scratch_shapes=[pltpu.VMEM((2, DEC_SEQ, DEC_SEQ), F32)],
        compiler_params=_params("arbitrary", "arbitrary"),
        name="na_attn",
    )(rpb_flat, zb, zb, zb, ck, cv)


_C_R, _C_K, _C_V = 0, C_WIDTH, 2 * C_WIDTH
_C_WLO = 3 * C_WIDTH
_C_ALO = _C_WLO + 2 * DECAY_RANK
_C_GLO = _C_ALO + 2 * ICLR_RANK
_HALO = SUBLANES


def _mixer_in_kernel(x_ref, xp_ref, xn_ref, mod_ref, win_ref, wsh_ref, dup_ref, aup_ref, gup_ref, w0_ref, a0_ref,
                     kk_ref, ka_ref, rk_ref, ones_ref,
                     za_ref, zb_ref, sg_ref, r_ref, v_ref, nkk_ref, w_ref, b_ref, k_ref, bonus_ref, g_ref,
                     *, tiles_per_seq):
    j = pl.program_id(0) % tiles_per_seq
    scale = 1.0 + mod_ref[0, 4:5, :]
    shift = mod_ref[0, 3:4, :]
    x = x_ref[...]
    h = (x * scale + shift).astype(BF16)
    za_ref[...] = _dot(h, win_ref[:, _COL_A:_COL_B])
    zb_ref[...] = _dot(h, win_ref[:, _COL_B:_COL_C])
    sg_ref[...] = _sigmoid(_dot(h, win_ref[:, _COL_G:]))

    x_ext = jnp.concatenate([xp_ref[...], x, xn_ref[...]], axis=0)
    z_ext = _dot((x_ext * scale + shift).astype(BF16), win_ref[:, _COL_C:_COL_G])
    row = lax.broadcasted_iota(jnp.int32, (TM, C_COLS), 0)
    z = z_ext[_HALO:_HALO + TM]
    z_prev = jnp.where(jnp.logical_and(row == 0, j == 0), 0.0, z_ext[_HALO - 1:_HALO - 1 + TM])
    z_next = jnp.where(jnp.logical_and(row == TM - 1, j == tiles_per_seq - 1), 0.0, z_ext[_HALO + 1:_HALO + 1 + TM])
    zc = z_prev * wsh_ref[0:1, :] + z * wsh_ref[1:2, :] + z_next * wsh_ref[2:3, :]

    r = zc[:, _C_R:_C_R + C_WIDTH]
    k = zc[:, _C_K:_C_K + C_WIDTH]
    v = zc[:, _C_V:_C_V + C_WIDTH]
    wlo = zc[:, _C_WLO:_C_WLO + 2 * DECAY_RANK]
    alo = zc[:, _C_ALO:_C_ALO + 2 * ICLR_RANK]
    glo = zc[:, _C_GLO:_C_GLO + GATE_RANK]
    ones = ones_ref[...]

    g_ref[...] = _dot(_sigmoid(glo).astype(BF16), gup_ref[...])
    kk = k * kk_ref[...]
    norm = jnp.sqrt(_head_sum(kk * kk, ones))
    kk = kk / jnp.maximum(norm, 1e-12)
    xw = w0_ref[...] + _dot(jnp.tanh(wlo).astype(BF16), dup_ref[...])
    logw = -(jnp.maximum(-xw, 0.0) + jnp.log(1.0 + jnp.exp(-jnp.abs(xw)))) - 0.5
    decay = jnp.exp(-jnp.exp(logw))
    a = _sigmoid(a0_ref[...] + _dot(alo.astype(BF16), aup_ref[...]))
    ka = ka_ref[...]
    kd_sum = None
    for d in range(2):
        a_d = a[:, d * C_WIDTH:(d + 1) * C_WIDTH]
        k_d = k * (1.0 + (a_d - 1.0) * ka)
        w_ref[d] = decay[:, d * C_WIDTH:(d + 1) * C_WIDTH]
        b_ref[d] = kk * a_d
        k_ref[d] = k_d
        kd_sum = k_d if kd_sum is None else kd_sum + k_d
    bonus_ref[...] = _head_sum(r * kd_sum * rk_ref[...], ones) * v
    r_ref[0] = r
    v_ref[0] = v
    nkk_ref[0] = -kk


def _mixer_in(st, layer, x, mods, w_in, w_shift, dec_up, iclr_up, gate_up, w0, a0, k_k, k_a, r_k, ones_bd):
    tiles_per_halo = TM // _HALO
    n_halo = st.n_tok // _HALO
    row = lambda n: _resident((1, n))
    stacked = lambda d: (pl.BlockSpec((d, TM, C_WIDTH), lambda i: (0, i, 0)),
                         jax.ShapeDtypeStruct((d, st.n_tok, C_WIDTH), F32))
    flat = lambda w: (_tile(w), jax.ShapeDtypeStruct((st.n_tok, w), F32))
    outs = [flat(A_COLS), flat(B_COLS), flat(GATE_COLS)] + [stacked(1)] * 3 + [stacked(2)] * 3 + [flat(C_WIDTH)] * 2
    return pl.pallas_call(
        functools.partial(_mixer_in_kernel, tiles_per_seq=st.tiles_per_seq),
        grid=(st.tiles,),
        in_specs=[
            _tile(D_MODEL),
            pl.BlockSpec((_HALO, D_MODEL), lambda i: (jnp.maximum(i * tiles_per_halo - 1, 0), 0)),
            pl.BlockSpec((_HALO, D_MODEL), lambda i: (jnp.minimum((i + 1) * tiles_per_halo, n_halo - 1), 0)),
            _mod_spec(st, layer),
            _resident((D_MODEL, IN_COLS), layer),
            _resident((3, C_COLS)),
            _resident((2 * DECAY_RANK, 2 * C_WIDTH)),
            _resident((2 * ICLR_RANK, 2 * C_WIDTH)),
            _resident((GATE_RANK, C_WIDTH), layer),
            row(2 * C_WIDTH), row(2 * C_WIDTH), row(C_WIDTH), row(C_WIDTH), row(C_WIDTH),
            _resident((LANES, LANES)),
        ],
        out_specs=[o[0] for o in outs],
        out_shape=[o[1] for o in outs],
        compiler_params=_params("parallel"),
        name=f"mixer_in_{st.name}",
    )(x, x, x, mods, w_in, w_shift, dec_up, iclr_up, gate_up, w0, a0, k_k, k_a, r_k, ones_bd)


_PB = 2 * SCAN_NB


def _scan_kernel(wf, wb, af, ab, bf, bb, kf, kb, rf, rb, vf, vb, s0_ref, yf_ref, yb_ref, state,
                 ops_a, ops_b, red, wcum, y_a, y_b):
    @pl.when(pl.program_id(1) == 0)
    def _():
        state[...] = s0_ref[...]
        y_b[...] = jnp.zeros_like(y_b)

    wcum[...] = jnp.ones_like(wcum)
    left = lax.broadcasted_iota(jnp.int32, (_PB, LANES), 1) < HEAD_DIM
    low = lax.broadcasted_iota(jnp.int32, (SUBLANES, LANES), 0) < SUBLANES // 2

    def load_ops(i, ops):
        ir = SCAN_TB - 1 - i
        for p, (pf, pb, qf, qb) in enumerate(((wf, wb, af, ab), (bf, bb, kf, kb), (rf, rb, vf, vb))):
            xp = jnp.concatenate([pf[0, :, i, :], pb[0, :, ir, :]], axis=0)
            xq = jnp.concatenate([qf[0, :, i, :], qb[0, :, ir, :]], axis=0)
            blocks = []
            for j in range(C_HEADS // 2):
                pv = xp[:, j * LANES:(j + 1) * LANES]
                qv = xq[:, j * LANES:(j + 1) * LANES]
                blocks.append(jnp.where(left, pv, pltpu.roll(qv, HEAD_DIM, 1)))
                blocks.append(jnp.where(left, pltpu.roll(pv, HEAD_DIM, 1), qv))
            ops[p] = jnp.concatenate(blocks, axis=0).T

    def fold(x):
        return x.reshape(HEAD_DIM // SUBLANES, SUBLANES, LANES).sum(axis=0)

    def compute(ops, ybuf):
        w_prev = wcum[...]
        w_now = w_prev * ops[0, 0:HEAD_DIM]
        wcum[...] = w_now
        w_inv = 1.0 / w_now
        a = ops[0, HEAD_DIM:] * w_prev
        b = ops[1, 0:HEAD_DIM]
        k = ops[1, HEAD_DIM:]
        r = ops[2, 0:HEAD_DIM]
        br = jnp.sum(b * r, axis=0, keepdims=True)
        kr = jnp.sum(k * r, axis=0, keepdims=True)
        b = b * w_inv
        k = k * w_inv
        r = r * w_now
        for vi in range(HEAD_DIM):
            z = state[0, vi]
            p1 = fold(z * a)
            p2 = fold(z * r)
            t = jnp.where(low, p1, p2) + pltpu.roll(jnp.where(low, p2, p1), SUBLANES // 2, 0)
            t = t + pltpu.roll(t, SUBLANES - 2, 0)
            red[vi] = t + pltpu.roll(t, SUBLANES - 1, 0)
        for vi in range(HEAD_DIM):
            state[0, vi] = state[0, vi] + red[vi, 0:1] * b + ops[2, HEAD_DIM + vi:HEAD_DIM + vi + 1] * k
        ybuf[...] = red[:, SUBLANES // 2, :] + red[:, 0, :] * br + ops[2, HEAD_DIM:] * kr

    def emit(i, ybuf):
        y = ybuf[...]
        t = jnp.concatenate([y, y], axis=0).T
        cols = [jnp.where(left, t[(2 * j) * _PB:(2 * j + 1) * _PB], t[(2 * j + 1) * _PB:(2 * j + 2) * _PB])
                for j in range(C_HEADS // 2)]
        out = jnp.concatenate(cols, axis=1)
        yf_ref[:, i, :] = out[:SCAN_NB]
        yb_ref[:, SCAN_TB - 1 - i, :] = out[SCAN_NB:]

    load_ops(0, ops_a)

    def body(m, carry):
        i0 = 2 * m
        load_ops(i0 + 1, ops_b)
        compute(ops_a, y_a)
        emit(jnp.maximum(i0 - 1, 0), y_b)
        load_ops(jnp.minimum(i0 + 2, SCAN_TB - 1), ops_a)
        compute(ops_b, y_b)
        emit(i0, y_a)
        return carry

    lax.fori_loop(0, SCAN_TB // 2, body, 0)
    emit(SCAN_TB - 1, y_b)
    w_end = wcum[...]
    for vi in range(HEAD_DIM):
        state[0, vi] = state[0, vi] * w_end


def _scan(st, w, b, k, nkk, r, v, s0):
    groups = st.nb // SCAN_NB
    n_t = st.seq // SCAN_TB
    view = lambda x: x.reshape(x.shape[0], st.nb, st.seq, C_WIDTH)
    blk = (1, SCAN_NB, SCAN_TB, C_WIDTH)
    fwd = lambda d: pl.BlockSpec(blk, lambda g, t: (d, g, t, 0))
    bwd = lambda d: pl.BlockSpec(blk, lambda g, t: (d, g, n_t - 1 - t, 0))
    st_blk = (1, HEAD_DIM, HEAD_DIM, LANES)
    w, b, k, nkk, r, v = (view(x) for x in (w, b, k, nkk, r, v))
    tile = pltpu.VMEM((3, LANES, LANES), F32)
    ybuf = pltpu.VMEM((HEAD_DIM, LANES), F32)
    yblk = (SCAN_NB, SCAN_TB, C_WIDTH)
    y_shape = jax.ShapeDtypeStruct((st.nb, st.seq, C_WIDTH), F32)
    yf, yb, s_t = pl.pallas_call(
        _scan_kernel,
        grid=(groups, n_t),
        in_specs=[fwd(0), bwd(1), fwd(0), bwd(0), fwd(0), bwd(1), fwd(0), bwd(1), fwd(0), bwd(0), fwd(0), bwd(0),
                  pl.BlockSpec(st_blk, lambda g, t: (g, 0, 0, 0), pipeline_mode=pl.Buffered(1))],
        out_specs=[pl.BlockSpec(yblk, lambda g, t: (g, t, 0)), pl.BlockSpec(yblk, lambda g, t: (g, n_t - 1 - t, 0)),
                   pl.BlockSpec(st_blk, lambda g, t: (g, 0, 0, 0))],
        out_shape=[y_shape, y_shape, jax.ShapeDtypeStruct((groups,) + st_blk[1:], F32)],
        scratch_shapes=[tile, tile, pltpu.VMEM((HEAD_DIM, SUBLANES, LANES), F32), ybuf, ybuf, ybuf],
        compiler_params=_params("arbitrary", "arbitrary"),
        name=f"rwkv_scan_{st.name}",
    )(w, w, nkk, nkk, b, b, k, k, r, r, v, v, s0)
    return yf.reshape(st.n_tok, C_WIDTH), yb.reshape(st.n_tok, C_WIDTH), s_t


def _merge_ffn_kernel(x_ref, mod_ref, oa_ref, ob_ref, yf_ref, yb_ref, bonus_ref, g_ref, sg_ref, pa_ref, pb_ref, pc_ref,
                      wo_ref, ones_ref, gng_ref, gnb_ref, lng_ref, lnb_ref, w1_ref, w2_ref, lng2_ref, lnb2_ref, o_ref):
    y = yf_ref[...] + yb_ref[...]
    ones = ones_ref[...]
    mu = _head_sum(y, ones) * (1.0 / HEAD_DIM)
    d = y - mu
    var = _head_sum(d * d, ones) * (1.0 / HEAD_DIM)
    yn = d * lax.rsqrt(var + GN_EPS) * gng_ref[...] + gnb_ref[...]
    oc = ((yn + bonus_ref[...]) * g_ref[...]).astype(BF16)
    merged = (sg_ref[:, :D_MODEL] * _dot(oa_ref[...].astype(BF16), pa_ref[...])
              + sg_ref[:, D_MODEL:2 * D_MODEL] * _dot(ob_ref[...].astype(BF16), pb_ref[...])
              + sg_ref[:, 2 * D_MODEL:] * _dot(oc, pc_ref[...]))
    o = _dot(merged.astype(BF16), wo_ref[...])
    x = _layer_norm(ALPHA * x_ref[...] + mod_ref[0, 5:6, :] * o, lng_ref[...], lnb_ref[...])

    h = (x * (1.0 + mod_ref[0, 7:8, :]) + mod_ref[0, 6:7, :]).astype(BF16)
    a = _dot(h, w1_ref[:, :D_FF])
    g = _dot(h, w1_ref[:, D_FF:])
    u = (a * _sigmoid(a) * g).astype(BF16)
    f = _dot(u, w2_ref[...])
    o_ref[...] = _layer_norm(ALPHA * x + 0.5 * mod_ref[0, 8:9, :] * f, lng2_ref[...], lnb2_ref[...])


def _merge_ffn(st, layer, x, mods, oa, oa_col, ob, ob_col, yf, yb, bonus, g, sg, pa, pb, pc, wo, ones, gn_g, gn_b, ln_g, ln_b,
               w1, w2, ln_g2, ln_b2):
    vec = lambda a, n: a.reshape(1, n)
    return pl.pallas_call(
        _merge_ffn_kernel,
        grid=(st.tiles,),
        in_specs=[
            _tile(D_MODEL), _mod_spec(st, layer), _tile(A_Q, oa_col), _tile(B_W, ob_col), _tile(C_WIDTH),
            _tile(C_WIDTH), _tile(C_WIDTH), _tile(C_WIDTH), _tile(GATE_COLS),
            _resident((A_Q, D_MODEL), layer), _resident((B_W, D_MODEL), layer), _resident((C_WIDTH, D_MODEL), layer),
            _resident((D_MODEL, D_MODEL), layer), _resident((LANES, LANES)),
            _resident((1, C_WIDTH)), _resident((1, C_WIDTH)), _resident((1, D_MODEL)), _resident((1, D_MODEL)),
            _resident((D_MODEL, 2 * D_FF), layer), _resident((D_FF, D_MODEL), layer),
            _resident((1, D_MODEL)), _resident((1, D_MODEL)),
        ],
        out_specs=_tile(D_MODEL),
        out_shape=jax.ShapeDtypeStruct((st.n_tok, D_MODEL), F32),
        compiler_params=_params("parallel"),
        name=f"merge_ffn2_{st.name}",
    )(x, mods, oa, ob, yf, yb, bonus, g, sg, pa, pb, pc, wo, ones,
      vec(gn_g, C_WIDTH), vec(gn_b, C_WIDTH), vec(ln_g, D_MODEL), vec(ln_b, D_MODEL),
      w1, w2, vec(ln_g2, D_MODEL), vec(ln_b2, D_MODEL))


def _block_diag2(a, b):
    za = jnp.zeros((a.shape[0], b.shape[1]), a.dtype)
    zb = jnp.zeros((b.shape[0], a.shape[1]), b.dtype)
    return jnp.concatenate([jnp.concatenate([a, za], axis=1), jnp.concatenate([zb, b], axis=1)], axis=0)


def kernel(x_prompt, x_sample, cache_attn_k, cache_attn_v, cache_na_k, cache_na_v, state_rwkv, c, c_ctx, w_ada, b_ada, ffn1_w_in, ffn1_w_out, ffn2_w_in, ffn2_w_out, w_in, w_shift, attn_sink, na_rpb, decay_w0, decay_up, iclr_a0, iclr_up, gate_up, k_k, k_a, r_k, gn_g, gn_b, proj_a, proj_b, proj_c, w_out, ln_g, ln_b):
    xs = {CTX: x_prompt.reshape(CTX.n_tok, D_MODEL), LAT: x_sample.reshape(LAT.n_tok, D_MODEL)}
    c_rows = jnp.zeros((MOD_ROWS, D_MODEL), F32).at[:DEC_BATCH].set(c).at[DEC_BATCH].set(c_ctx)
    mods_all = _ada_mods(c_rows, w_ada, b_ada)
    cos_t, sin_t = _rope_tables()
    head_of = np.arange(LANES) // HEAD_DIM
    ones_bd = jnp.asarray(head_of[:, None] == head_of[None, :], BF16)
    ctx_groups = BATCH // SCAN_NB
    zero_state = jnp.zeros((ctx_groups, HEAD_DIM, HEAD_DIM, LANES), F32)

    bf = lambda w: w.astype(BF16)
    ffn1_w, ffn2_w = (bf(ffn1_w_in), bf(ffn1_w_out)), (bf(ffn2_w_in), bf(ffn2_w_out))
    w_in_bf, gate_up_bf = bf(w_in), bf(gate_up)
    proj_w = (bf(proj_a), bf(proj_b), bf(proj_c), bf(w_out))

    new_ka, new_va, new_kb, new_vb, new_st = [], [], [], [], []
    for l in range(DEPTH):
        mods = mods_all
        ffn1 = (*ffn1_w, ln_g[l, 0], ln_b[l, 0], 0)
        ffn2 = (*ffn2_w, ln_g[l, 2], ln_b[l, 2], 6)
        w_in_l = w_in_bf
        prep_w = (w_shift[l],
                  _block_diag2(decay_up[l, 0], decay_up[l, 1]).astype(BF16),
                  _block_diag2(iclr_up[l, 0], iclr_up[l, 1]).astype(BF16),
                  gate_up_bf,
                  decay_w0[l].reshape(1, 2 * C_WIDTH), iclr_a0[l].reshape(1, 2 * C_WIDTH),
                  k_k[l].reshape(1, C_WIDTH), k_a[l].reshape(1, C_WIDTH), r_k[l].reshape(1, C_WIDTH), ones_bd)
        merge_w = (*proj_w, ones_bd, gn_g[l], gn_b[l], ln_g[l, 1], ln_b[l, 1])
        s0_lat = state_rwkv[:, l].transpose(3, 4, 2, 1, 0).reshape(1, HEAD_DIM, HEAD_DIM, LANES)

        for st in (CTX, LAT):
            x = _ffn(st, l, xs[st], mods, *ffn1)
            za, zb, sg, r, v, nkk, w, b, k, bonus, g = _mixer_in(st, l, x, mods, w_in_l, *prep_w)
            if st is CTX:
                o_ab = _ctx_attn(za, zb, attn_sink[l])
                attn = (o_ab, 0, o_ab, 1)
                zac = za.reshape(BATCH, SEQ, A_COLS)
                zbc = zb.reshape(BATCH, SEQ, B_COLS)
                new_ka.append(zac[..., A_Q:A_Q + A_KV].reshape(BATCH, SEQ, A_KV_HEADS, HEAD_DIM))
                new_va.append(zac[..., A_Q + A_KV:].reshape(BATCH, SEQ, A_KV_HEADS, HEAD_DIM))
                new_kb.append(zbc[..., B_W:2 * B_W].reshape(BATCH, SEQ, B_HEADS, HEAD_DIM))
                new_vb.append(zbc[..., 2 * B_W:].reshape(BATCH, SEQ, B_HEADS, HEAD_DIM))
            else:
                oa = _win_attn(za, cache_attn_k[:, l].reshape(DEC_BATCH, PAST_LEN, A_KV),
                               cache_attn_v[:, l].reshape(DEC_BATCH, PAST_LEN, A_KV), attn_sink[l], cos_t, sin_t)
                ob = _na_attn(zb, cache_na_k[:, l].reshape(DEC_BATCH, PAST_LEN, B_W),
                              cache_na_v[:, l].reshape(DEC_BATCH, PAST_LEN, B_W), na_rpb[l].reshape(-1))
                attn = (oa, 0, ob, 0)
            yf, yb, s_t = _scan(st, w, b, k, nkk, r, v, zero_state if st is CTX else s0_lat)
            if st is CTX:
                s_t = s_t.reshape(ctx_groups, HEAD_DIM, HEAD_DIM, C_HEADS, 2, SCAN_NB)
                new_st.append(s_t.transpose(0, 5, 4, 3, 1, 2).reshape(BATCH, 2, C_HEADS, HEAD_DIM, HEAD_DIM))
            xs[st] = _merge_ffn(st, l, x, mods, *attn, yf, yb, bonus, g, sg, *merge_w, *ffn2[:4])

    y_p = xs[CTX].reshape(BATCH, SEQ, D_MODEL)
    y_s = xs[LAT].reshape(DEC_BATCH, DEC_SEQ, D_MODEL)
    return (y_p, y_s, jnp.stack(new_ka, axis=1), jnp.stack(new_va, axis=1), jnp.stack(new_kb, axis=1),
            jnp.stack(new_vb, axis=1), jnp.stack(new_st, axis=1))
```

```python
import collections
import functools

import jax
import jax.numpy as jnp
import numpy as np
from jax import lax
from jax.experimental import pallas as pl
from jax.experimental.pallas import tpu as pltpu

D_MODEL = 1024
BATCH = 16
SEQ = 256
DEPTH = 2
DEC_BATCH = 8
DEC_SEQ = 1024
PAST_LEN = 256
GRID_W = 64
GRID_ROWS = DEC_SEQ // GRID_W
HEAD_DIM = 64
SCALE = HEAD_DIM ** -0.5
A_HEADS = 8
A_KV_HEADS = 2
A_GROUP = A_HEADS // A_KV_HEADS
A_WINDOW = 128
A_BLOCK = 128
A_Q = A_HEADS * HEAD_DIM
A_KV = A_KV_HEADS * HEAD_DIM
A_COLS = A_Q + 2 * A_KV
B_HEADS = 8
B_W = B_HEADS * HEAD_DIM
B_COLS = 3 * B_W
NA_ROWS = 8
NA_COLS = 16
C_HEADS = 8
C_WIDTH = C_HEADS * HEAD_DIM
DECAY_RANK = 64
ICLR_RANK = 64
GATE_RANK = 128
C_COLS = 3 * C_WIDTH + 2 * DECAY_RANK + 2 * ICLR_RANK + GATE_RANK
GN_EPS = 64e-5
GATE_COLS = 3 * D_MODEL
IN_COLS = A_COLS + B_COLS + C_COLS + GATE_COLS
D_FF = 2816
N_MOD = 9
ROPE_BASE = 10000.0
LN_EPS = 1e-5
NEG_INF = -1e30
ALPHA = (2 * DEPTH) ** 0.25

LANES = 128
SUBLANES = 8
TM = 256
MOD_ROWS = 16
SCAN_TB = 64
SCAN_NB = LANES // (2 * C_HEADS)
VMEM_LIMIT = 56 * 1024 * 1024

F32 = jnp.float32
BF16 = jnp.bfloat16

Stream = collections.namedtuple("Stream", "name nb seq n_tok tiles tiles_per_seq mod_row")


def _make_stream(name, nb, seq, mod_row):
    return Stream(name, nb, seq, nb * seq, nb * seq // TM, seq // TM, mod_row)


CTX = _make_stream("ctx", BATCH, SEQ, lambda i: DEC_BATCH)
LAT = _make_stream("lat", DEC_BATCH, DEC_SEQ, lambda i: i // (DEC_SEQ // TM))


def _sigmoid(x):
    return 1.0 / (1.0 + jnp.exp(-x))


def _layer_norm(y, g, b):
    mu = jnp.mean(y, axis=-1, keepdims=True)
    d = y - mu
    var = jnp.mean(d * d, axis=-1, keepdims=True)
    return d * lax.rsqrt(var + LN_EPS) * g + b


def _dot(a, b):
    return jnp.dot(a, b, preferred_element_type=F32)


def _dot_nt(a, b):
    return lax.dot_general(a, b, (((1,), (1,)), ((), ())), preferred_element_type=F32)


def _head_sum(x, same_head):
    hi = x.astype(BF16)
    lo = (x - hi.astype(F32)).astype(BF16)
    slabs = [_dot(hi[:, j:j + LANES], same_head) + _dot(lo[:, j:j + LANES], same_head)
             for j in range(0, x.shape[-1], LANES)]
    return jnp.concatenate(slabs, axis=-1)


def _resident(shape, layer=None):
    zeros = (0,) * len(shape)
    if layer is None:
        return pl.BlockSpec(shape, lambda *_: zeros, pipeline_mode=pl.Buffered(1))
    return pl.BlockSpec((None,) + tuple(shape), lambda *_: (layer,) + zeros, pipeline_mode=pl.Buffered(1))


def _tile(width, col=0):
    return pl.BlockSpec((TM, width), lambda i: (i, col))


def _mod_spec(st, layer):
    return pl.BlockSpec((None, 1, N_MOD, D_MODEL), lambda i: (layer, st.mod_row(i), 0, 0))


def _params(*sem):
    return pltpu.CompilerParams(dimension_semantics=sem, vmem_limit_bytes=VMEM_LIMIT)


def _ada_kernel(c_ref, w_ref, b_ref, o_ref):
    c = c_ref[...]
    s = (c * _sigmoid(c)).astype(BF16)
    o_ref[0] = _dot(s, w_ref[0].astype(BF16)) + b_ref[0]


def _ada_mods(c_rows, w_ada, b_ada):
    out = pl.pallas_call(
        _ada_kernel,
        grid=(DEPTH, N_MOD),
        in_specs=[
            pl.BlockSpec((MOD_ROWS, D_MODEL), lambda l, j: (0, 0)),
            pl.BlockSpec((1, D_MODEL, D_MODEL), lambda l, j: (l, 0, j)),
            pl.BlockSpec((1, 1, D_MODEL), lambda l, j: (l, 0, j)),
        ],
        out_specs=pl.BlockSpec((1, MOD_ROWS, D_MODEL), lambda l, j: (l, 0, j)),
        out_shape=jax.ShapeDtypeStruct((DEPTH, MOD_ROWS, N_MOD * D_MODEL), F32),
        compiler_params=_params("parallel", "parallel"),
        name="ada_mods",
    )(c_rows, w_ada, b_ada.reshape(DEPTH, 1, N_MOD * D_MODEL))
    return out.reshape(DEPTH, MOD_ROWS, N_MOD, D_MODEL)


def _ffn_kernel(x_ref, mod_ref, w1_ref, w2_ref, g_ref, b_ref, o_ref, *, base):
    x = x_ref[...]
    shift = mod_ref[0, base:base + 1, :]
    scale = mod_ref[0, base + 1:base + 2, :]
    gate = mod_ref[0, base + 2:base + 3, :]
    h = (x * (1.0 + scale) + shift).astype(BF16)
    a = _dot(h, w1_ref[:, :D_FF])
    g = _dot(h, w1_ref[:, D_FF:])
    u = (a * _sigmoid(a) * g).astype(BF16)
    f = _dot(u, w2_ref[...])
    o_ref[...] = _layer_norm(ALPHA * x + 0.5 * gate * f, g_ref[...], b_ref[...])


def _ffn(st, layer, x, mods, w1, w2, ln_g, ln_b, base):
    return pl.pallas_call(
        functools.partial(_ffn_kernel, base=base),
        grid=(st.tiles,),
        in_specs=[
            _tile(D_MODEL), _mod_spec(st, layer),
            _resident((D_MODEL, 2 * D_FF), layer), _resident((D_FF, D_MODEL), layer),
            _resident((1, D_MODEL)), _resident((1, D_MODEL)),
        ],
        out_specs=_tile(D_MODEL),
        out_shape=jax.ShapeDtypeStruct((st.n_tok, D_MODEL), F32),
        compiler_params=_params("parallel"),
        name=f"ffn{base // 6 + 1}_{st.name}",
    )(x, mods, w1, w2, ln_g.reshape(1, D_MODEL), ln_b.reshape(1, D_MODEL))


_COL_A = 0
_COL_B = A_COLS
_COL_C = A_COLS + B_COLS
_COL_G = A_COLS + B_COLS + C_COLS


def _softmax_pv(scores, values, sink=None):
    m = scores[0].max(axis=-1, keepdims=True)
    for s in scores[1:]:
        m = jnp.maximum(m, s.max(axis=-1, keepdims=True))
    if sink is not None:
        m = jnp.maximum(m, sink)
    den = jnp.exp(sink - m) if sink is not None else 0.0
    acc = None
    for s, v in zip(scores, values):
        p = jnp.exp(s - m)
        den = den + p.sum(axis=-1, keepdims=True)
        pv = _dot(p.astype(BF16), v)
        acc = pv if acc is None else acc + pv
    return acc / den


def _head(x, h):
    return x[:, h * HEAD_DIM:(h + 1) * HEAD_DIM]


def _ctx_attn_kernel(sink_ref, za_ref, zb_ref, o_ref):
    za = za_ref[...]
    zb = zb_ref[...]
    outs = []
    row = lax.broadcasted_iota(jnp.int32, (A_GROUP * SEQ, 1), 0) // SEQ
    for g in range(A_KV_HEADS):
        q4 = jnp.concatenate([_head(za, A_GROUP * g + j) for j in range(A_GROUP)], axis=0).astype(BF16)
        k = _head(za, A_HEADS + g).astype(BF16)
        v = _head(za, A_HEADS + A_KV_HEADS + g).astype(BF16)
        sink = jnp.zeros((A_GROUP * SEQ, 1), F32)
        for j in range(A_GROUP):
            sink = jnp.where(row == j, sink_ref[A_GROUP * g + j], sink)
        o4 = _softmax_pv([_dot_nt(q4, k) * SCALE], [v], sink)
        outs.extend(o4[j * SEQ:(j + 1) * SEQ] for j in range(A_GROUP))
    for h in range(B_HEADS):
        q = _head(zb, h).astype(BF16)
        k = _head(zb, B_HEADS + h).astype(BF16)
        v = _head(zb, 2 * B_HEADS + h).astype(BF16)
        outs.append(_softmax_pv([_dot_nt(q, k) * SCALE], [v]))
    o_ref[...] = jnp.concatenate(outs, axis=-1)


def _ctx_attn(za, zb, sink):
    return pl.pallas_call(
        _ctx_attn_kernel,
        grid=(BATCH,),
        in_specs=[
            pl.BlockSpec(memory_space=pltpu.SMEM),
            pl.BlockSpec((SEQ, A_COLS), lambda b: (b, 0)),
            pl.BlockSpec((SEQ, B_COLS), lambda b: (b, 0)),
        ],
        out_specs=pl.BlockSpec((SEQ, A_Q + B_W), lambda b: (b, 0)),
        out_shape=jax.ShapeDtypeStruct((CTX.n_tok, A_Q + B_W), F32),
        compiler_params=_params("parallel"),
        name="ctx_attn",
    )(sink, za, zb)


def _rope(x, cos, sin_signed):
    w = x.shape[-1]
    half = HEAD_DIM // 2
    lane = lax.broadcasted_iota(jnp.int32, x.shape, 1)
    partner = jnp.where(lane % HEAD_DIM < half, pltpu.roll(x, w - half, 1), pltpu.roll(x, half, 1))
    return x * cos + partner * sin_signed


def _win_attn_kernel(sink_ref, za_ref, ck_ref, cv_ref, cos_ref, sin_ref, o_ref):
    n = pl.program_id(1)
    nb = DEC_SEQ // A_BLOCK
    q0 = pl.multiple_of(n * A_BLOCK, A_BLOCK)
    q = _rope(za_ref[pl.ds(q0, A_BLOCK), :A_Q], cos_ref[pl.ds(q0, A_BLOCK), :], sin_ref[pl.ds(q0, A_BLOCK), :])
    ks, vs = [], []
    for j in range(3):
        kb = jnp.clip(n - 1 + j, 0, nb - 1)
        k0 = pl.multiple_of(kb * A_BLOCK, A_BLOCK)
        kblk = za_ref[pl.ds(k0, A_BLOCK), A_Q:A_Q + A_KV]
        ks.append(_rope(kblk, cos_ref[pl.ds(k0, A_BLOCK), :A_KV], sin_ref[pl.ds(k0, A_BLOCK), :A_KV]))
        vs.append(za_ref[pl.ds(k0, A_BLOCK), A_Q + A_KV:])
    kband = jnp.concatenate(ks, axis=0).astype(BF16)
    vband = jnp.concatenate(vs, axis=0).astype(BF16)
    ck = ck_ref[0].astype(BF16)
    cv = cv_ref[0].astype(BF16)
    m4 = A_GROUP * A_BLOCK
    qi = lax.broadcasted_iota(jnp.int32, (m4, 3 * A_BLOCK), 0) % A_BLOCK
    kj = lax.broadcasted_iota(jnp.int32, (m4, 3 * A_BLOCK), 1)
    rel = kj - A_BLOCK - qi
    kpos = (n - 1) * A_BLOCK + kj
    valid = (jnp.abs(rel) <= A_WINDOW) & (kpos >= 0) & (kpos < DEC_SEQ)
    row = lax.broadcasted_iota(jnp.int32, (m4, 1), 0) // A_BLOCK
    outs = []
    for g in range(A_KV_HEADS):
        q4 = jnp.concatenate([_head(q, A_GROUP * g + j) for j in range(A_GROUP)], axis=0).astype(BF16)
        s_loc = jnp.where(valid, _dot_nt(q4, _head(kband, g)) * SCALE, NEG_INF)
        s_ctx = _dot_nt(q4, _head(ck, g)) * SCALE
        sink = jnp.zeros((m4, 1), F32)
        for j in range(A_GROUP):
            sink = jnp.where(row == j, sink_ref[A_GROUP * g + j], sink)
        o4 = _softmax_pv([s_loc, s_ctx], [_head(vband, g), _head(cv, g)], sink)
        outs.extend(o4[j * A_BLOCK:(j + 1) * A_BLOCK] for j in range(A_GROUP))
    o_ref[...] = jnp.concatenate(outs, axis=-1)


def _rope_tables():
    t = np.arange(DEC_SEQ)
    n_freq = HEAD_DIM // 4
    inv = ROPE_BASE ** (-jnp.arange(n_freq, dtype=F32) / n_freq)
    rows = jnp.asarray(t // GRID_W, F32)
    cols = jnp.asarray(t % GRID_W, F32)
    ang = jnp.concatenate([rows[:, None] * inv, cols[:, None] * inv], axis=-1)
    cos, sin = jnp.cos(ang), jnp.sin(ang)
    cos_h = jnp.concatenate([cos, cos], axis=-1)
    sin_h = jnp.concatenate([-sin, sin], axis=-1)
    return jnp.tile(cos_h, (1, A_HEADS)), jnp.tile(sin_h, (1, A_HEADS))


def _win_attn(za, ck, cv, sink, cos_t, sin_t):
    nb = DEC_SEQ // A_BLOCK
    return pl.pallas_call(
        _win_attn_kernel,
        grid=(DEC_BATCH, nb),
        in_specs=[
            pl.BlockSpec(memory_space=pltpu.SMEM),
            pl.BlockSpec((DEC_SEQ, A_COLS), lambda b, n: (b, 0)),
            pl.BlockSpec((1, PAST_LEN, A_KV), lambda b, n: (b, 0, 0)),
            pl.BlockSpec((1, PAST_LEN, A_KV), lambda b, n: (b, 0, 0)),
            pl.BlockSpec((DEC_SEQ, A_Q), lambda b, n: (0, 0)),
            pl.BlockSpec((DEC_SEQ, A_Q), lambda b, n: (0, 0)),
        ],
        out_specs=pl.BlockSpec((A_BLOCK, A_Q), lambda b, n: (b * nb + n, 0)),
        out_shape=jax.ShapeDtypeStruct((LAT.n_tok, A_Q), F32),
        compiler_params=_params("parallel", "arbitrary"),
        name="win_attn",
    )(sink, za, ck, cv, cos_t, sin_t)


_NA_QBLK = 256
_RPB_I = 2 * NA_ROWS - 1
_RPB_J = 2 * NA_COLS - 1


def _na_build_bias(rpb_ref, head, bias_ref, slot):
    shape = (GRID_W, 2 * GRID_W)
    qc = lax.broadcasted_iota(jnp.int32, shape, 0)
    lane = lax.broadcasted_iota(jnp.int32, shape, 1)
    kc = lane % GRID_W
    left = lane < GRID_W
    cs = jnp.clip(qc - NA_COLS // 2, 0, GRID_W - NA_COLS)
    idx = jnp.where((kc >= cs) & (kc < cs + NA_COLS), jnp.clip(kc - qc + NA_COLS - 1, 0, _RPB_J - 1), -1)
    base = head * (_RPB_I * _RPB_J)
    tables = []
    for i in range(_RPB_I):
        t = jnp.full(shape, NEG_INF, F32)
        for j in range(_RPB_J):
            t = jnp.where(idx == j, rpb_ref[base + i * _RPB_J + j], t)
        tables.append(t)
    neg = jnp.full(shape, NEG_INF, F32)
    half = NA_ROWS // 2
    for qr in range(GRID_ROWS):
        rs = min(max(qr - half, 0), GRID_ROWS - NA_ROWS)
        for m in range(GRID_ROWS // 2):
            kr0, kr1 = 2 * m, 2 * m + 1
            v0 = rs <= kr0 < rs + NA_ROWS
            v1 = rs <= kr1 < rs + NA_ROWS
            i0 = kr0 - qr + NA_ROWS - 1
            if v0 and v1:
                piece = jnp.where(left, tables[i0], tables[i0 + 1])
            elif v0:
                piece = jnp.where(left, tables[i0], NEG_INF)
            elif v1:
                piece = jnp.where(left, NEG_INF, tables[i0 + 1])
            else:
                piece = neg
            bias_ref[slot, qr * GRID_W:(qr + 1) * GRID_W, m * 2 * GRID_W:(m + 1) * 2 * GRID_W] = piece


def _na_attn_kernel(rpb_ref, q_ref, k_ref, v_ref, ck_ref, cv_ref, o_ref, bias_ref):
    hp = pl.program_id(0)

    @pl.when(pl.program_id(1) == 0)
    def _():
        for half in range(2):
            _na_build_bias(rpb_ref, 2 * hp + half, bias_ref, half)

    ck2 = ck_ref[0].astype(BF16)
    cv2 = cv_ref[0].astype(BF16)
    rows_per_blk = _NA_QBLK // GRID_W
    for qb in range(DEC_SEQ // _NA_QBLK):
        rows = slice(qb * _NA_QBLK, (qb + 1) * _NA_QBLK)
        r_lo = min(max(qb * rows_per_blk - NA_ROWS // 2, 0), GRID_ROWS - NA_ROWS)
        r_hi = min(max((qb + 1) * rows_per_blk - 1 - NA_ROWS // 2, 0), GRID_ROWS - NA_ROWS) + NA_ROWS
        keys = slice(r_lo // 2 * 2 * GRID_W, -(-r_hi // 2) * 2 * GRID_W)
        k2 = k_ref[keys, :].astype(BF16)
        v2 = v_ref[keys, :].astype(BF16)
        q2 = q_ref[rows, :].astype(BF16)
        outs = []
        for half in range(2):
            q = _head(q2, half)
            s_loc = _dot_nt(q, _head(k2, half)) * SCALE + bias_ref[half, rows, keys]
            s_ctx = _dot_nt(q, _head(ck2, half)) * SCALE
            outs.append(_softmax_pv([s_loc, s_ctx], [_head(v2, half), _head(cv2, half)]))
        o_ref[rows, :] = jnp.concatenate(outs, axis=-1)


def _na_attn(zb, ck, cv, rpb_flat):
    pairs = B_HEADS // 2
    pw = 2 * HEAD_DIM
    return pl.pallas_call(
        _na_attn_kernel,
        grid=(pairs, DEC_BATCH),
        in_specs=[
            pl.BlockSpec(memory_space=pltpu.SMEM),
            pl.BlockSpec((DEC_SEQ, pw), lambda hp, b: (b, hp)),
            pl.BlockSpec((DEC_SEQ, pw), lambda hp, b: (b, pairs + hp)),
            pl.BlockSpec((DEC_SEQ, pw), lambda hp, b: (b, 2 * pairs + hp)),
            pl.BlockSpec((1, PAST_LEN, pw), lambda hp, b: (b, 0, hp)),
            pl.BlockSpec((1, PAST_LEN, pw), lambda hp, b: (b, 0, hp)),
        ],
        out_specs=pl.BlockSpec((DEC_SEQ, pw), lambda hp, b: (b, hp)),
        out_shape=jax.ShapeDtypeStruct((LAT.n_tok, B_W), F32),
        scratch_shapes=[pltpu.VMEM((2, DEC_SEQ, DEC_SEQ), F32)],
        compiler_params=_params("arbitrary", "arbitrary"),
        name="na_attn",
    )(rpb_flat, zb, zb, zb, ck, cv)


_C_R, _C_K, _C_V = 0, C_WIDTH, 2 * C_WIDTH
_C_WLO = 3 * C_WIDTH
_C_ALO = _C_WLO + 2 * DECAY_RANK
_C_GLO = _C_ALO + 2 * ICLR_RANK
_HALO = SUBLANES


def _mixer_in_kernel(x_ref, xp_ref, xn_ref, mod_ref, win_ref, wsh_ref, dup_ref, aup_ref, gup_ref, w0_ref, a0_ref,
                     kk_ref, ka_ref, rk_ref, ones_ref,
                     za_ref, zb_ref, sg_ref, r_ref, v_ref, nkk_ref, w_ref, b_ref, k_ref, bonus_ref, g_ref, *kv_refs,
                     tiles_per_seq):
    j = pl.program_id(0) % tiles_per_seq
    scale = 1.0 + mod_ref[0, 4:5, :]
    shift = mod_ref[0, 3:4, :]
    x = x_ref[...]
    h = (x * scale + shift).astype(BF16)
    za = _dot(h, win_ref[:, _COL_A:_COL_B])
    zb = _dot(h, win_ref[:, _COL_B:_COL_C])
    za_ref[...] = za
    zb_ref[...] = zb
    sg_ref[...] = _sigmoid(_dot(h, win_ref[:, _COL_G:]))
    if kv_refs:
        key_a_ref, val_a_ref, key_b_ref, val_b_ref = kv_refs
        key_a_ref[...] = za[:, A_Q:A_Q + A_KV]
        val_a_ref[...] = za[:, A_Q + A_KV:]
        key_b_ref[...] = zb[:, B_W:2 * B_W]
        val_b_ref[...] = zb[:, 2 * B_W:]

    x_ext = jnp.concatenate([xp_ref[...], x, xn_ref[...]], axis=0)
    z_ext = _dot((x_ext * scale + shift).astype(BF16), win_ref[:, _COL_C:_COL_G])
    row = lax.broadcasted_iota(jnp.int32, (TM, C_COLS), 0)
    z = z_ext[_HALO:_HALO + TM]
    z_prev = jnp.where(jnp.logical_and(row == 0, j == 0), 0.0, z_ext[_HALO - 1:_HALO - 1 + TM])
    z_next = jnp.where(jnp.logical_and(row == TM - 1, j == tiles_per_seq - 1), 0.0, z_ext[_HALO + 1:_HALO + 1 + TM])
    zc = z_prev * wsh_ref[0:1, :] + z * wsh_ref[1:2, :] + z_next * wsh_ref[2:3, :]

    r = zc[:, _C_R:_C_R + C_WIDTH]
    k = zc[:, _C_K:_C_K + C_WIDTH]
    v = zc[:, _C_V:_C_V + C_WIDTH]
    wlo = zc[:, _C_WLO:_C_WLO + 2 * DECAY_RANK]
    alo = zc[:, _C_ALO:_C_ALO + 2 * ICLR_RANK]
    glo = zc[:, _C_GLO:_C_GLO + GATE_RANK]
    ones = ones_ref[...]

    g_ref[...] = _dot(_sigmoid(glo).astype(BF16), gup_ref[...])
    kk = k * kk_ref[...]
    norm = jnp.sqrt(_head_sum(kk * kk, ones))
    kk = kk / jnp.maximum(norm, 1e-12)
    xw = w0_ref[...] + _dot(jnp.tanh(wlo).astype(BF16), dup_ref[...])
    logw = -(jnp.maximum(-xw, 0.0) + jnp.log(1.0 + jnp.exp(-jnp.abs(xw)))) - 0.5
    decay = jnp.exp(-jnp.exp(logw))
    a = _sigmoid(a0_ref[...] + _dot(alo.astype(BF16), aup_ref[...]))
    ka = ka_ref[...]
    kd_sum = None
    for d in range(2):
        a_d = a[:, d * C_WIDTH:(d + 1) * C_WIDTH]
        k_d = k * (1.0 + (a_d - 1.0) * ka)
        w_ref[d] = decay[:, d * C_WIDTH:(d + 1) * C_WIDTH]
        b_ref[d] = kk * a_d
        k_ref[d] = k_d
        kd_sum = k_d if kd_sum is None else kd_sum + k_d
    bonus_ref[...] = _head_sum(r * kd_sum * rk_ref[...], ones) * v
    r_ref[0] = r
    v_ref[0] = v
    nkk_ref[0] = -kk


def _mixer_in(st, layer, x, mods, w_in, w_shift, dec_up, iclr_up, gate_up, w0, a0, k_k, k_a, r_k, ones_bd, emit_kv):
    tiles_per_halo = TM // _HALO
    n_halo = st.n_tok // _HALO
    row = lambda n: _resident((1, n))
    stacked = lambda d: (pl.BlockSpec((d, TM, C_WIDTH), lambda i: (0, i, 0)),
                         jax.ShapeDtypeStruct((d, st.n_tok, C_WIDTH), F32))
    flat = lambda w: (_tile(w), jax.ShapeDtypeStruct((st.n_tok, w), F32))
    outs = [flat(A_COLS), flat(B_COLS), flat(GATE_COLS)] + [stacked(1)] * 3 + [stacked(2)] * 3 + [flat(C_WIDTH)] * 2
    if emit_kv:
        outs += [flat(A_KV)] * 2 + [flat(B_W)] * 2
    return pl.pallas_call(
        functools.partial(_mixer_in_kernel, tiles_per_seq=st.tiles_per_seq),
        grid=(st.tiles,),
        in_specs=[
            _tile(D_MODEL),
            pl.BlockSpec((_HALO, D_MODEL), lambda i: (jnp.maximum(i * tiles_per_halo - 1, 0), 0)),
            pl.BlockSpec((_HALO, D_MODEL), lambda i: (jnp.minimum((i + 1) * tiles_per_halo, n_halo - 1), 0)),
            _mod_spec(st, layer),
            _resident((D_MODEL, IN_COLS), layer),
            _resident((3, C_COLS)),
            _resident((2 * DECAY_RANK, 2 * C_WIDTH)),
            _resident((2 * ICLR_RANK, 2 * C_WIDTH)),
            _resident((GATE_RANK, C_WIDTH), layer),
            row(2 * C_WIDTH), row(2 * C_WIDTH), row(C_WIDTH), row(C_WIDTH), row(C_WIDTH),
            _resident((LANES, LANES)),
        ],
        out_specs=[o[0] for o in outs],
        out_shape=[o[1] for o in outs],
        compiler_params=_params("parallel"),
        name=f"mixer_in_{st.name}",
    )(x, x, x, mods, w_in, w_shift, dec_up, iclr_up, gate_up, w0, a0, k_k, k_a, r_k, ones_bd)


_PB = 2 * SCAN_NB


def _scan_kernel(wf, wb, af, ab, bf, bb, kf, kb, rf, rb, vf, vb, s0_ref, yf_ref, yb_ref, state,
                 ops_a, ops_b, red, wcum, y_a, y_b):
    @pl.when(pl.program_id(1) == 0)
    def _():
        state[...] = s0_ref[...]
        y_b[...] = jnp.zeros_like(y_b)

    wcum[...] = jnp.ones_like(wcum)
    left = lax.broadcasted_iota(jnp.int32, (_PB, LANES), 1) < HEAD_DIM
    low = lax.broadcasted_iota(jnp.int32, (SUBLANES, LANES), 0) < SUBLANES // 2

    def load_ops(i, ops):
        ir = SCAN_TB - 1 - i
        for p, (pf, pb, qf, qb) in enumerate(((wf, wb, af, ab), (bf, bb, kf, kb), (rf, rb, vf, vb))):
            xp = jnp.concatenate([pf[0, :, i, :], pb[0, :, ir, :]], axis=0)
            xq = jnp.concatenate([qf[0, :, i, :], qb[0, :, ir, :]], axis=0)
            blocks = []
            for j in range(C_HEADS // 2):
                pv = xp[:, j * LANES:(j + 1) * LANES]
                qv = xq[:, j * LANES:(j + 1) * LANES]
                blocks.append(jnp.where(left, pv, pltpu.roll(qv, HEAD_DIM, 1)))
                blocks.append(jnp.where(left, pltpu.roll(pv, HEAD_DIM, 1), qv))
            ops[p] = jnp.concatenate(blocks, axis=0).T

    def fold(x):
        return x.reshape(HEAD_DIM // SUBLANES, SUBLANES, LANES).sum(axis=0)

    def compute(ops, ybuf):
        w_prev = wcum[...]
        w_now = w_prev * ops[0, 0:HEAD_DIM]
        wcum[...] = w_now
        w_inv = 1.0 / w_now
        a = ops[0, HEAD_DIM:] * w_prev
        b = ops[1, 0:HEAD_DIM]
        k = ops[1, HEAD_DIM:]
        r = ops[2, 0:HEAD_DIM]
        br = jnp.sum(b * r, axis=0, keepdims=True)
        kr = jnp.sum(k * r, axis=0, keepdims=True)
        b = b * w_inv
        k = k * w_inv
        r = r * w_now
        for vi in range(HEAD_DIM):
            z = state[0, vi]
            p1 = fold(z * a)
            p2 = fold(z * r)
            t = jnp.where(low, p1, p2) + pltpu.roll(jnp.where(low, p2, p1), SUBLANES // 2, 0)
            t = t + pltpu.roll(t, SUBLANES - 2, 0)
            red[vi] = t + pltpu.roll(t, SUBLANES - 1, 0)
        for vi in range(HEAD_DIM):
            state[0, vi] = state[0, vi] + red[vi, 0:1] * b + ops[2, HEAD_DIM + vi:HEAD_DIM + vi + 1] * k
        ybuf[...] = red[:, SUBLANES // 2, :] + red[:, 0, :] * br + ops[2, HEAD_DIM:] * kr

    def emit(i, ybuf):
        y = ybuf[...]
        t = jnp.concatenate([y, y], axis=0).T
        cols = [jnp.where(left, t[(2 * j) * _PB:(2 * j + 1) * _PB], t[(2 * j + 1) * _PB:(2 * j + 2) * _PB])
                for j in range(C_HEADS // 2)]
        out = jnp.concatenate(cols, axis=1)
        yf_ref[:, i, :] = out[:SCAN_NB]
        yb_ref[:, SCAN_TB - 1 - i, :] = out[SCAN_NB:]

    load_ops(0, ops_a)

    def body(m, carry):
        i0 = 2 * m
        load_ops(i0 + 1, ops_b)
        compute(ops_a, y_a)
        emit(jnp.maximum(i0 - 1, 0), y_b)
        load_ops(jnp.minimum(i0 + 2, SCAN_TB - 1), ops_a)
        compute(ops_b, y_b)
        emit(i0, y_a)
        return carry

    lax.fori_loop(0, SCAN_TB // 2, body, 0)
    emit(SCAN_TB - 1, y_b)
    w_end = wcum[...]
    for vi in range(HEAD_DIM):
        state[0, vi] = state[0, vi] * w_end


def _scan(st, w, b, k, nkk, r, v, s0):
    groups = st.nb // SCAN_NB
    n_t = st.seq // SCAN_TB
    view = lambda x: x.reshape(x.shape[0], st.nb, st.seq, C_WIDTH)
    blk = (1, SCAN_NB, SCAN_TB, C_WIDTH)
    fwd = lambda d: pl.BlockSpec(blk, lambda g, t: (d, g, t, 0))
    bwd = lambda d: pl.BlockSpec(blk, lambda g, t: (d, g, n_t - 1 - t, 0))
    st_blk = (1, HEAD_DIM, HEAD_DIM, LANES)
    w, b, k, nkk, r, v = (view(x) for x in (w, b, k, nkk, r, v))
    tile = pltpu.VMEM((3, LANES, LANES), F32)
    ybuf = pltpu.VMEM((HEAD_DIM, LANES), F32)
    yblk = (SCAN_NB, SCAN_TB, C_WIDTH)
    y_shape = jax.ShapeDtypeStruct((st.nb, st.seq, C_WIDTH), F32)
    yf, yb, s_t = pl.pallas_call(
        _scan_kernel,
        grid=(groups, n_t),
        in_specs=[fwd(0), bwd(1), fwd(0), bwd(0), fwd(0), bwd(1), fwd(0), bwd(1), fwd(0), bwd(0), fwd(0), bwd(0),
                  pl.BlockSpec(st_blk, lambda g, t: (g, 0, 0, 0), pipeline_mode=pl.Buffered(1))],
        out_specs=[pl.BlockSpec(yblk, lambda g, t: (g, t, 0)), pl.BlockSpec(yblk, lambda g, t: (g, n_t - 1 - t, 0)),
                   pl.BlockSpec(st_blk, lambda g, t: (g, 0, 0, 0))],
        out_shape=[y_shape, y_shape, jax.ShapeDtypeStruct((groups,) + st_blk[1:], F32)],
        scratch_shapes=[tile, tile, pltpu.VMEM((HEAD_DIM, SUBLANES, LANES), F32), ybuf, ybuf, ybuf],
        compiler_params=_params("arbitrary", "arbitrary"),
        name=f"rwkv_scan_{st.name}",
    )(w, w, nkk, nkk, b, b, k, k, r, r, v, v, s0)
    return yf.reshape(st.n_tok, C_WIDTH), yb.reshape(st.n_tok, C_WIDTH), s_t


def _merge_ffn_kernel(x_ref, mod_ref, oa_ref, ob_ref, yf_ref, yb_ref, bonus_ref, g_ref, sg_ref, pa_ref, pb_ref, pc_ref,
                      wo_ref, ones_ref, gng_ref, gnb_ref, lng_ref, lnb_ref, w1_ref, w2_ref, lng2_ref, lnb2_ref, o_ref):
    y = yf_ref[...] + yb_ref[...]
    ones = ones_ref[...]
    mu = _head_sum(y, ones) * (1.0 / HEAD_DIM)
    d = y - mu
    var = _head_sum(d * d, ones) * (1.0 / HEAD_DIM)
    yn = d * lax.rsqrt(var + GN_EPS) * gng_ref[...] + gnb_ref[...]
    oc = ((yn + bonus_ref[...]) * g_ref[...]).astype(BF16)
    merged = (sg_ref[:, :D_MODEL] * _dot(oa_ref[...].astype(BF16), pa_ref[...])
              + sg_ref[:, D_MODEL:2 * D_MODEL] * _dot(ob_ref[...].astype(BF16), pb_ref[...])
              + sg_ref[:, 2 * D_MODEL:] * _dot(oc, pc_ref[...]))
    o = _dot(merged.astype(BF16), wo_ref[...])
    x = _layer_norm(ALPHA * x_ref[...] + mod_ref[0, 5:6, :] * o, lng_ref[...], lnb_ref[...])

    h = (x * (1.0 + mod_ref[0, 7:8, :]) + mod_ref[0, 6:7, :]).astype(BF16)
    a = _dot(h, w1_ref[:, :D_FF])
    g = _dot(h, w1_ref[:, D_FF:])
    u = (a * _sigmoid(a) * g).astype(BF16)
    f = _dot(u, w2_ref[...])
    o_ref[...] = _layer_norm(ALPHA * x + 0.5 * mod_ref[0, 8:9, :] * f, lng2_ref[...], lnb2_ref[...])


def _merge_ffn(st, layer, x, mods, oa, oa_col, ob, ob_col, yf, yb, bonus, g, sg, pa, pb, pc, wo, ones, gn_g, gn_b, ln_g, ln_b,
               w1, w2, ln_g2, ln_b2):
    vec = lambda a, n: a.reshape(1, n)
    return pl.pallas_call(
        _merge_ffn_kernel,
        grid=(st.tiles,),
        in_specs=[
            _tile(D_MODEL), _mod_spec(st, layer), _tile(A_Q, oa_col), _tile(B_W, ob_col), _tile(C_WIDTH),
            _tile(C_WIDTH), _tile(C_WIDTH), _tile(C_WIDTH), _tile(GATE_COLS),
            _resident((A_Q, D_MODEL), layer), _resident((B_W, D_MODEL), layer), _resident((C_WIDTH, D_MODEL), layer),
            _resident((D_MODEL, D_MODEL), layer), _resident((LANES, LANES)),
            _resident((1, C_WIDTH)), _resident((1, C_WIDTH)), _resident((1, D_MODEL)), _resident((1, D_MODEL)),
            _resident((D_MODEL, 2 * D_FF), layer), _resident((D_FF, D_MODEL), layer),
            _resident((1, D_MODEL)), _resident((1, D_MODEL)),
        ],
        out_specs=_tile(D_MODEL),
        out_shape=jax.ShapeDtypeStruct((st.n_tok, D_MODEL), F32),
        compiler_params=_params("parallel"),
        name=f"merge_ffn2_{st.name}",
    )(x, mods, oa, ob, yf, yb, bonus, g, sg, pa, pb, pc, wo, ones,
      vec(gn_g, C_WIDTH), vec(gn_b, C_WIDTH), vec(ln_g, D_MODEL), vec(ln_b, D_MODEL),
      w1, w2, vec(ln_g2, D_MODEL), vec(ln_b2, D_MODEL))


def _block_diag2(a, b):
    za = jnp.zeros((a.shape[0], b.shape[1]), a.dtype)
    zb = jnp.zeros((b.shape[0], a.shape[1]), b.dtype)
    return jnp.concatenate([jnp.concatenate([a, za], axis=1), jnp.concatenate([zb, b], axis=1)], axis=0)


def kernel(x_prompt, x_sample, cache_attn_k, cache_attn_v, cache_na_k, cache_na_v, state_rwkv, c, c_ctx, w_ada, b_ada, ffn1_w_in, ffn1_w_out, ffn2_w_in, ffn2_w_out, w_in, w_shift, attn_sink, na_rpb, decay_w0, decay_up, iclr_a0, iclr_up, gate_up, k_k, k_a, r_k, gn_g, gn_b, proj_a, proj_b, proj_c, w_out, ln_g, ln_b):
    xs = {CTX: x_prompt.reshape(CTX.n_tok, D_MODEL), LAT: x_sample.reshape(LAT.n_tok, D_MODEL)}
    c_rows = jnp.zeros((MOD_ROWS, D_MODEL), F32).at[:DEC_BATCH].set(c).at[DEC_BATCH].set(c_ctx)
    mods_all = _ada_mods(c_rows, w_ada, b_ada)
    cos_t, sin_t = _rope_tables()
    head_of = np.arange(LANES) // HEAD_DIM
    ones_bd = jnp.asarray(head_of[:, None] == head_of[None, :], BF16)
    ctx_groups = BATCH // SCAN_NB
    zero_state = jnp.zeros((ctx_groups, HEAD_DIM, HEAD_DIM, LANES), F32)

    bf = lambda w: w.astype(BF16)
    ffn1_w, ffn2_w = (bf(ffn1_w_in), bf(ffn1_w_out)), (bf(ffn2_w_in), bf(ffn2_w_out))
    w_in_bf, gate_up_bf = bf(w_in), bf(gate_up)
    proj_w = (bf(proj_a), bf(proj_b), bf(proj_c), bf(w_out))

    new_ka, new_va, new_kb, new_vb, new_st = [], [], [], [], []
    for l in range(DEPTH):
        mods = mods_all
        ffn1 = (*ffn1_w, ln_g[l, 0], ln_b[l, 0], 0)
        ffn2 = (*ffn2_w, ln_g[l, 2], ln_b[l, 2], 6)
        w_in_l = w_in_bf
        prep_w = (w_shift[l],
                  _block_diag2(decay_up[l, 0], decay_up[l, 1]).astype(BF16),
                  _block_diag2(iclr_up[l, 0], iclr_up[l, 1]).astype(BF16),
                  gate_up_bf,
                  decay_w0[l].reshape(1, 2 * C_WIDTH), iclr_a0[l].reshape(1, 2 * C_WIDTH),
                  k_k[l].reshape(1, C_WIDTH), k_a[l].reshape(1, C_WIDTH), r_k[l].reshape(1, C_WIDTH), ones_bd)
        merge_w = (*proj_w, ones_bd, gn_g[l], gn_b[l], ln_g[l, 1], ln_b[l, 1])
        s0_lat = state_rwkv[:, l].transpose(3, 4, 2, 1, 0).reshape(1, HEAD_DIM, HEAD_DIM, LANES)

        for st in (CTX, LAT):
            x = _ffn(st, l, xs[st], mods, *ffn1)
            res = _mixer_in(st, l, x, mods, w_in_l, *prep_w, emit_kv=st is CTX)
            za, zb, sg, r, v, nkk, w, b, k, bonus, g = res[:11]
            if st is CTX:
                o_ab = _ctx_attn(za, zb, attn_sink[l])
                attn = (o_ab, 0, o_ab, 1)
                ka, va, kb, vb = res[11:]
                new_ka.append(ka.reshape(BATCH, SEQ, A_KV_HEADS, HEAD_DIM))
                new_va.append(va.reshape(BATCH, SEQ, A_KV_HEADS, HEAD_DIM))
                new_kb.append(kb.reshape(BATCH, SEQ, B_HEADS, HEAD_DIM))
                new_vb.append(vb.reshape(BATCH, SEQ, B_HEADS, HEAD_DIM))
            else:
                oa = _win_attn(za, cache_attn_k[:, l].reshape(DEC_BATCH, PAST_LEN, A_KV),
                               cache_attn_v[:, l].reshape(DEC_BATCH, PAST_LEN, A_KV), attn_sink[l], cos_t, sin_t)
                ob = _na_attn(zb, cache_na_k[:, l].reshape(DEC_BATCH, PAST_LEN, B_W),
                              cache_na_v[:, l].reshape(DEC_BATCH, PAST_LEN, B_W), na_rpb[l].reshape(-1))
                attn = (oa, 0, ob, 0)
            yf, yb, s_t = _scan(st, w, b, k, nkk, r, v, zero_state if st is CTX else s0_lat)
            if st is CTX:
                s_t = s_t.reshape(ctx_groups, HEAD_DIM, HEAD_DIM, C_HEADS, 2, SCAN_NB)
                new_st.append(s_t.transpose(0, 5, 4, 3, 1, 2).reshape(BATCH, 2, C_HEADS, HEAD_DIM, HEAD_DIM))
            xs[st] = _merge_ffn(st, l, x, mods, *attn, yf, yb, bonus, g, sg, *merge_w, *ffn2[:4])

    y_p = xs[CTX].reshape(BATCH, SEQ, D_MODEL)
    y_s = xs[LAT].reshape(DEC_BATCH, DEC_SEQ, D_MODEL)
    return (y_p, y_s, jnp.stack(new_ka, axis=1), jnp.stack(new_va, axis=1), jnp.stack(new_kb, axis=1),
            jnp.stack(new_vb, axis=1), jnp.stack(new_st, axis=1))
```

```python
import collections
import functools

import jax
import jax.numpy as jnp
import numpy as np
from jax import lax
from jax.experimental import pallas as pl
from jax.experimental.pallas import tpu as pltpu

D_MODEL = 1024
BATCH = 16
SEQ = 256
DEPTH = 2
DEC_BATCH = 8
DEC_SEQ = 1024
PAST_LEN = 256
GRID_W = 64
GRID_ROWS = DEC_SEQ // GRID_W
HEAD_DIM = 64
SCALE = HEAD_DIM ** -0.5
A_HEADS = 8
A_KV_HEADS = 2
A_GROUP = A_HEADS // A_KV_HEADS
A_WINDOW = 128
A_BLOCK = 128
A_Q = A_HEADS * HEAD_DIM
A_KV = A_KV_HEADS * HEAD_DIM
A_COLS = A_Q + 2 * A_KV
B_HEADS = 8
B_W = B_HEADS * HEAD_DIM
B_COLS = 3 * B_W
NA_ROWS = 8
NA_COLS = 16
C_HEADS = 8
C_WIDTH = C_HEADS * HEAD_DIM
DECAY_RANK = 64
ICLR_RANK = 64
GATE_RANK = 128
C_COLS = 3 * C_WIDTH + 2 * DECAY_RANK + 2 * ICLR_RANK + GATE_RANK
GN_EPS = 64e-5
GATE_COLS = 3 * D_MODEL
IN_COLS = A_COLS + B_COLS + C_COLS + GATE_COLS
D_FF = 2816
N_MOD = 9
ROPE_BASE = 10000.0
LN_EPS = 1e-5
NEG_INF = -1e30
ALPHA = (2 * DEPTH) ** 0.25

LANES = 128
SUBLANES = 8
TM = 256
MOD_ROWS = 16
SCAN_TB = 32
SCAN_NB = LANES // (2 * C_HEADS)
VMEM_LIMIT = 56 * 1024 * 1024

F32 = jnp.float32
BF16 = jnp.bfloat16

Stream = collections.namedtuple("Stream", "name nb seq n_tok tiles tiles_per_seq mod_row")


def _make_stream(name, nb, seq, mod_row):
    return Stream(name, nb, seq, nb * seq, nb * seq // TM, seq // TM, mod_row)


CTX = _make_stream("ctx", BATCH, SEQ, lambda i: DEC_BATCH)
LAT = _make_stream("lat", DEC_BATCH, DEC_SEQ, lambda i: i // (DEC_SEQ // TM))


def _sigmoid(x):
    return 1.0 / (1.0 + jnp.exp(-x))


def _layer_norm(y, g, b):
    mu = jnp.mean(y, axis=-1, keepdims=True)
    d = y - mu
    var = jnp.mean(d * d, axis=-1, keepdims=True)
    return d * lax.rsqrt(var + LN_EPS) * g + b


def _dot(a, b):
    return jnp.dot(a, b, preferred_element_type=F32)


def _dot_nt(a, b):
    return lax.dot_general(a, b, (((1,), (1,)), ((), ())), preferred_element_type=F32)


def _head_sum(x, same_head):
    hi = x.astype(BF16)
    lo = (x - hi.astype(F32)).astype(BF16)
    slabs = [_dot(hi[:, j:j + LANES], same_head) + _dot(lo[:, j:j + LANES], same_head)
             for j in range(0, x.shape[-1], LANES)]
    return jnp.concatenate(slabs, axis=-1)


def _resident(shape, layer=None):
    zeros = (0,) * len(shape)
    if layer is None:
        return pl.BlockSpec(shape, lambda *_: zeros, pipeline_mode=pl.Buffered(1))
    return pl.BlockSpec((None,) + tuple(shape), lambda *_: (layer,) + zeros, pipeline_mode=pl.Buffered(1))


def _tile(width, col=0):
    return pl.BlockSpec((TM, width), lambda i: (i, col))


def _mod_spec(st, layer):
    return pl.BlockSpec((None, 1, N_MOD, D_MODEL), lambda i: (layer, st.mod_row(i), 0, 0))


def _params(*sem):
    return pltpu.CompilerParams(dimension_semantics=sem, vmem_limit_bytes=VMEM_LIMIT)


def _ada_kernel(c_ref, w_ref, b_ref, o_ref):
    c = c_ref[...]
    s = (c * _sigmoid(c)).astype(BF16)
    o_ref[0] = _dot(s, w_ref[0].astype(BF16)) + b_ref[0]


def _ada_mods(c_rows, w_ada, b_ada):
    out = pl.pallas_call(
        _ada_kernel,
        grid=(DEPTH, N_MOD),
        in_specs=[
            pl.BlockSpec((MOD_ROWS, D_MODEL), lambda l, j: (0, 0)),
            pl.BlockSpec((1, D_MODEL, D_MODEL), lambda l, j: (l, 0, j)),
            pl.BlockSpec((1, 1, D_MODEL), lambda l, j: (l, 0, j)),
        ],
        out_specs=pl.BlockSpec((1, MOD_ROWS, D_MODEL), lambda l, j: (l, 0, j)),
        out_shape=jax.ShapeDtypeStruct((DEPTH, MOD_ROWS, N_MOD * D_MODEL), F32),
        compiler_params=_params("parallel", "parallel"),
        name="ada_mods",
    )(c_rows, w_ada, b_ada.reshape(DEPTH, 1, N_MOD * D_MODEL))
    return out.reshape(DEPTH, MOD_ROWS, N_MOD, D_MODEL)


def _ffn_kernel(x_ref, mod_ref, w1_ref, w2_ref, g_ref, b_ref, o_ref, *, base):
    x = x_ref[...]
    shift = mod_ref[0, base:base + 1, :]
    scale = mod_ref[0, base + 1:base + 2, :]
    gate = mod_ref[0, base + 2:base + 3, :]
    h = (x * (1.0 + scale) + shift).astype(BF16)
    a = _dot(h, w1_ref[:, :D_FF])
    g = _dot(h, w1_ref[:, D_FF:])
    u = (a * _sigmoid(a) * g).astype(BF16)
    f = _dot(u, w2_ref[...])
    o_ref[...] = _layer_norm(ALPHA * x + 0.5 * gate * f, g_ref[...], b_ref[...])


def _ffn(st, layer, x, mods, w1, w2, ln_g, ln_b, base):
    return pl.pallas_call(
        functools.partial(_ffn_kernel, base=base),
        grid=(st.tiles,),
        in_specs=[
            _tile(D_MODEL), _mod_spec(st, layer),
            _resident((D_MODEL, 2 * D_FF), layer), _resident((D_FF, D_MODEL), layer),
            _resident((1, D_MODEL)), _resident((1, D_MODEL)),
        ],
        out_specs=_tile(D_MODEL),
        out_shape=jax.ShapeDtypeStruct((st.n_tok, D_MODEL), F32),
        compiler_params=_params("parallel"),
        name=f"ffn{base // 6 + 1}_{st.name}",
    )(x, mods, w1, w2, ln_g.reshape(1, D_MODEL), ln_b.reshape(1, D_MODEL))


_COL_A = 0
_COL_B = A_COLS
_COL_C = A_COLS + B_COLS
_COL_G = A_COLS + B_COLS + C_COLS


def _softmax_pv(scores, values, sink=None):
    m = scores[0].max(axis=-1, keepdims=True)
    for s in scores[1:]:
        m = jnp.maximum(m, s.max(axis=-1, keepdims=True))
    if sink is not None:
        m = jnp.maximum(m, sink)
    den = jnp.exp(sink - m) if sink is not None else 0.0
    acc = None
    for s, v in zip(scores, values):
        p = jnp.exp(s - m)
        den = den + p.sum(axis=-1, keepdims=True)
        pv = _dot(p.astype(BF16), v)
        acc = pv if acc is None else acc + pv
    return acc / den


_PW = 2 * HEAD_DIM


def _left_half(rows):
    return lax.broadcasted_iota(jnp.int32, (rows, _PW), 1) < HEAD_DIM


def _split_heads(x, left):
    return jnp.concatenate([jnp.where(left, x, 0.0), jnp.where(left, 0.0, x)], axis=0).astype(BF16)


def _both_halves(slab, swapped, left, head):
    return (jnp.where(left, slab, swapped) if head == 0 else jnp.where(left, swapped, slab)).astype(BF16)


def _ctx_attn_kernel(sink_ref, za_ref, zb_ref, o_ref):
    left = _left_half(SEQ)
    slab = lambda ref, j: ref[:, j * _PW:(j + 1) * _PW]
    k_slab = slab(za_ref, A_Q // _PW)
    v_slab = slab(za_ref, (A_Q + A_KV) // _PW)
    k_swap = pltpu.roll(k_slab, HEAD_DIM, 1)
    v_swap = pltpu.roll(v_slab, HEAD_DIM, 1)
    row = lax.broadcasted_iota(jnp.int32, (A_GROUP * SEQ, 1), 0) // SEQ
    for g in range(A_KV_HEADS):
        slabs = range(g * A_GROUP // 2, (g + 1) * A_GROUP // 2)
        q4 = jnp.concatenate([_split_heads(slab(za_ref, j), left) for j in slabs], axis=0)
        sink = jnp.zeros((A_GROUP * SEQ, 1), F32)
        for j in range(A_GROUP):
            sink = jnp.where(row == j, sink_ref[A_GROUP * g + j], sink)
        o4 = _softmax_pv([_dot_nt(q4, _both_halves(k_slab, k_swap, left, g)) * SCALE],
                         [_both_halves(v_slab, v_swap, left, g)], sink)
        for i, j in enumerate(slabs):
            o_ref[:, j * _PW:(j + 1) * _PW] = jnp.where(left, o4[2 * i * SEQ:(2 * i + 1) * SEQ],
                                                        o4[(2 * i + 1) * SEQ:(2 * i + 2) * SEQ])
    for j in range(B_HEADS // 2):
        k2 = slab(zb_ref, B_HEADS // 2 + j).astype(BF16)
        v2 = slab(zb_ref, B_HEADS + j).astype(BF16)
        o2 = _softmax_pv([_dot_nt(_split_heads(slab(zb_ref, j), left), k2) * SCALE], [v2])
        o_ref[:, A_Q + j * _PW:A_Q + (j + 1) * _PW] = jnp.where(left, o2[:SEQ], o2[SEQ:])


def _ctx_attn(za, zb, sink):
    return pl.pallas_call(
        _ctx_attn_kernel,
        grid=(BATCH,),
        in_specs=[
            pl.BlockSpec(memory_space=pltpu.SMEM),
            pl.BlockSpec((SEQ, A_COLS), lambda b: (b, 0)),
            pl.BlockSpec((SEQ, B_COLS), lambda b: (b, 0)),
        ],
        out_specs=pl.BlockSpec((SEQ, A_Q + B_W), lambda b: (b, 0)),
        out_shape=jax.ShapeDtypeStruct((CTX.n_tok, A_Q + B_W), F32),
        compiler_params=_params("parallel"),
        name="ctx_attn",
    )(sink, za, zb)


def _rope(x, cos, sin_signed):
    w = x.shape[-1]
    half = HEAD_DIM // 2
    lane = lax.broadcasted_iota(jnp.int32, x.shape, 1)
    partner = jnp.where(lane % HEAD_DIM < half, pltpu.roll(x, w - half, 1), pltpu.roll(x, half, 1))
    return x * cos + partner * sin_signed


def _win_attn_kernel(sink_ref, za_ref, ck_ref, cv_ref, cos_ref, sin_ref, o_ref):
    n = pl.program_id(1)
    nb = DEC_SEQ // A_BLOCK
    q0 = pl.multiple_of(n * A_BLOCK, A_BLOCK)
    q = _rope(za_ref[pl.ds(q0, A_BLOCK), :A_Q], cos_ref[pl.ds(q0, A_BLOCK), :], sin_ref[pl.ds(q0, A_BLOCK), :])
    ks, vs = [], []
    for j in range(3):
        kb = jnp.clip(n - 1 + j, 0, nb - 1)
        k0 = pl.multiple_of(kb * A_BLOCK, A_BLOCK)
        kblk = za_ref[pl.ds(k0, A_BLOCK), A_Q:A_Q + A_KV]
        ks.append(_rope(kblk, cos_ref[pl.ds(k0, A_BLOCK), :A_KV], sin_ref[pl.ds(k0, A_BLOCK), :A_KV]))
        vs.append(za_ref[pl.ds(k0, A_BLOCK), A_Q + A_KV:])
    left = _left_half(A_BLOCK)
    bands = []
    for x in (jnp.concatenate(ks, axis=0), jnp.concatenate(vs, axis=0), ck_ref[0], cv_ref[0]):
        bands.append((x, pltpu.roll(x, HEAD_DIM, 1), _left_half(x.shape[0])))
    m4 = A_GROUP * A_BLOCK
    qi = lax.broadcasted_iota(jnp.int32, (m4, 3 * A_BLOCK), 0) % A_BLOCK
    kj = lax.broadcasted_iota(jnp.int32, (m4, 3 * A_BLOCK), 1)
    rel = kj - A_BLOCK - qi
    kpos = (n - 1) * A_BLOCK + kj
    valid = (jnp.abs(rel) <= A_WINDOW) & (kpos >= 0) & (kpos < DEC_SEQ)
    row = lax.broadcasted_iota(jnp.int32, (m4, 1), 0) // A_BLOCK
    for g in range(A_KV_HEADS):
        kband, vband, ck, cv = (_both_halves(x, swapped, lft, g) for x, swapped, lft in bands)
        slabs = range(g * A_GROUP // 2, (g + 1) * A_GROUP // 2)
        q4 = jnp.concatenate([_split_heads(q[:, j * _PW:(j + 1) * _PW], left) for j in slabs], axis=0)
        s_loc = jnp.where(valid, _dot_nt(q4, kband) * SCALE, NEG_INF)
        s_ctx = _dot_nt(q4, ck) * SCALE
        sink = jnp.zeros((m4, 1), F32)
        for j in range(A_GROUP):
            sink = jnp.where(row == j, sink_ref[A_GROUP * g + j], sink)
        o4 = _softmax_pv([s_loc, s_ctx], [vband, cv], sink)
        for i, j in enumerate(slabs):
            o_ref[:, j * _PW:(j + 1) * _PW] = jnp.where(left, o4[2 * i * A_BLOCK:(2 * i + 1) * A_BLOCK],
                                                        o4[(2 * i + 1) * A_BLOCK:(2 * i + 2) * A_BLOCK])


def _rope_tables():
    t = np.arange(DEC_SEQ)
    n_freq = HEAD_DIM // 4
    inv = ROPE_BASE ** (-jnp.arange(n_freq, dtype=F32) / n_freq)
    rows = jnp.asarray(t // GRID_W, F32)
    cols = jnp.asarray(t % GRID_W, F32)
    ang = jnp.concatenate([rows[:, None] * inv, cols[:, None] * inv], axis=-1)
    cos, sin = jnp.cos(ang), jnp.sin(ang)
    cos_h = jnp.concatenate([cos, cos], axis=-1)
    sin_h = jnp.concatenate([-sin, sin], axis=-1)
    return jnp.tile(cos_h, (1, A_HEADS)), jnp.tile(sin_h, (1, A_HEADS))


def _win_attn(za, ck, cv, sink, cos_t, sin_t):
    nb = DEC_SEQ // A_BLOCK
    return pl.pallas_call(
        _win_attn_kernel,
        grid=(DEC_BATCH, nb),
        in_specs=[
            pl.BlockSpec(memory_space=pltpu.SMEM),
            pl.BlockSpec((DEC_SEQ, A_COLS), lambda b, n: (b, 0)),
            pl.BlockSpec((1, PAST_LEN, A_KV), lambda b, n: (b, 0, 0)),
            pl.BlockSpec((1, PAST_LEN, A_KV), lambda b, n: (b, 0, 0)),
            pl.BlockSpec((DEC_SEQ, A_Q), lambda b, n: (0, 0)),
            pl.BlockSpec((DEC_SEQ, A_Q), lambda b, n: (0, 0)),
        ],
        out_specs=pl.BlockSpec((A_BLOCK, A_Q), lambda b, n: (b * nb + n, 0)),
        out_shape=jax.ShapeDtypeStruct((LAT.n_tok, A_Q), F32),
        compiler_params=_params("parallel", "arbitrary"),
        name="win_attn",
    )(sink, za, ck, cv, cos_t, sin_t)


_NA_QBLK = 256
_RPB_I = 2 * NA_ROWS - 1
_RPB_J = 2 * NA_COLS - 1


def _na_build_bias(rpb_ref, head, bias_ref, slot):
    shape = (GRID_W, 2 * GRID_W)
    qc = lax.broadcasted_iota(jnp.int32, shape, 0)
    lane = lax.broadcasted_iota(jnp.int32, shape, 1)
    kc = lane % GRID_W
    left = lane < GRID_W
    cs = jnp.clip(qc - NA_COLS // 2, 0, GRID_W - NA_COLS)
    idx = jnp.where((kc >= cs) & (kc < cs + NA_COLS), jnp.clip(kc - qc + NA_COLS - 1, 0, _RPB_J - 1), -1)
    base = head * (_RPB_I * _RPB_J)
    tables = []
    for i in range(_RPB_I):
        t = jnp.full(shape, NEG_INF, F32)
        for j in range(_RPB_J):
            t = jnp.where(idx == j, rpb_ref[base + i * _RPB_J + j], t)
        tables.append(t)
    neg = jnp.full(shape, NEG_INF, F32)
    half = NA_ROWS // 2
    for qr in range(GRID_ROWS):
        rs = min(max(qr - half, 0), GRID_ROWS - NA_ROWS)
        for m in range(GRID_ROWS // 2):
            kr0, kr1 = 2 * m, 2 * m + 1
            v0 = rs <= kr0 < rs + NA_ROWS
            v1 = rs <= kr1 < rs + NA_ROWS
            i0 = kr0 - qr + NA_ROWS - 1
            if v0 and v1:
                piece = jnp.where(left, tables[i0], tables[i0 + 1])
            elif v0:
                piece = jnp.where(left, tables[i0], NEG_INF)
            elif v1:
                piece = jnp.where(left, NEG_INF, tables[i0 + 1])
            else:
                piece = neg
            bias_ref[slot, qr * GRID_W:(qr + 1) * GRID_W, m * 2 * GRID_W:(m + 1) * 2 * GRID_W] = piece


def _na_attn_kernel(rpb_ref, q_ref, k_ref, v_ref, ck_ref, cv_ref, o_ref, bias_ref):
    hp = pl.program_id(0)

    @pl.when(pl.program_id(1) == 0)
    def _():
        for half in range(2):
            _na_build_bias(rpb_ref, 2 * hp + half, bias_ref, half)

    ck2 = ck_ref[0].astype(BF16)
    cv2 = cv_ref[0].astype(BF16)
    left = _left_half(_NA_QBLK)
    rows_per_blk = _NA_QBLK // GRID_W
    for qb in range(DEC_SEQ // _NA_QBLK):
        rows = slice(qb * _NA_QBLK, (qb + 1) * _NA_QBLK)
        r_lo = min(max(qb * rows_per_blk - NA_ROWS // 2, 0), GRID_ROWS - NA_ROWS)
        r_hi = min(max((qb + 1) * rows_per_blk - 1 - NA_ROWS // 2, 0), GRID_ROWS - NA_ROWS) + NA_ROWS
        keys = slice(r_lo // 2 * 2 * GRID_W, -(-r_hi // 2) * 2 * GRID_W)
        q2 = _split_heads(q_ref[rows, :], left)
        bias = jnp.concatenate([bias_ref[0, rows, keys], bias_ref[1, rows, keys]], axis=0)
        s_loc = _dot_nt(q2, k_ref[keys, :].astype(BF16)) * SCALE + bias
        s_ctx = _dot_nt(q2, ck2) * SCALE
        o2 = _softmax_pv([s_loc, s_ctx], [v_ref[keys, :].astype(BF16), cv2])
        o_ref[rows, :] = jnp.where(left, o2[:_NA_QBLK], o2[_NA_QBLK:])


def _na_attn(zb, ck, cv, rpb_flat):
    pairs = B_HEADS // 2
    pw = _PW
    return pl.pallas_call(
        _na_attn_kernel,
        grid=(pairs, DEC_BATCH),
        in_specs=[
            pl.BlockSpec(memory_space=pltpu.SMEM),
            pl.BlockSpec((DEC_SEQ, pw), lambda hp, b: (b, hp)),
            pl.BlockSpec((DEC_SEQ, pw), lambda hp, b: (b, pairs + hp)),
            pl.BlockSpec((DEC_SEQ, pw), lambda hp, b: (b, 2 * pairs + hp)),
            pl.BlockSpec((1, PAST_LEN, pw), lambda hp, b: (b, 0, hp)),
            pl.BlockSpec((1, PAST_LEN, pw), lambda hp, b: (b, 0, hp)),
        ],
        out_specs=pl.BlockSpec((DEC_SEQ, pw), lambda hp, b: (b, hp)),
        out_shape=jax.ShapeDtypeStruct((LAT.n_tok, B_W), F32),
        scratch_shapes=[pltpu.VMEM((2, DEC_SEQ, DEC_SEQ), F32)],
        compiler_params=_params("arbitrary", "arbitrary"),
        name="na_attn",
    )(rpb_flat, zb, zb, zb, ck, cv)


_C_R, _C_K, _C_V = 0, C_WIDTH, 2 * C_WIDTH
_C_WLO = 3 * C_WIDTH
_C_ALO = _C_WLO + 2 * DECAY_RANK
_C_GLO = _C_ALO + 2 * ICLR_RANK
_HALO = SUBLANES


def _mixer_in_kernel(x_ref, xp_ref, xn_ref, mod_ref, win_ref, wsh_ref, dup_ref, aup_ref, gup_ref, w0_ref, a0_ref,
                     kk_ref, ka_ref, rk_ref, ones_ref,
                     za_ref, zb_ref, sg_ref, r_ref, v_ref, nkk_ref, w_ref, b_ref, k_ref, bonus_ref, g_ref,
                     *, tiles_per_seq):
    j = pl.program_id(0) % tiles_per_seq
    scale = 1.0 + mod_ref[0, 4:5, :]
    shift = mod_ref[0, 3:4, :]
    x = x_ref[...]
    h = (x * scale + shift).astype(BF16)
    za_ref[...] = _dot(h, win_ref[:, _COL_A:_COL_B])
    zb_ref[...] = _dot(h, win_ref[:, _COL_B:_COL_C])
    sg_ref[...] = _sigmoid(_dot(h, win_ref[:, _COL_G:]))

    x_ext = jnp.concatenate([xp_ref[...], x, xn_ref[...]], axis=0)
    z_ext = _dot((x_ext * scale + shift).astype(BF16), win_ref[:, _COL_C:_COL_G])
    row = lax.broadcasted_iota(jnp.int32, (TM, C_COLS), 0)
    z = z_ext[_HALO:_HALO + TM]
    z_prev = jnp.where(jnp.logical_and(row == 0, j == 0), 0.0, z_ext[_HALO - 1:_HALO - 1 + TM])
    z_next = jnp.where(jnp.logical_and(row == TM - 1, j == tiles_per_seq - 1), 0.0, z_ext[_HALO + 1:_HALO + 1 + TM])
    zc = z_prev * wsh_ref[0:1, :] + z * wsh_ref[1:2, :] + z_next * wsh_ref[2:3, :]

    r = zc[:, _C_R:_C_R + C_WIDTH]
    k = zc[:, _C_K:_C_K + C_WIDTH]
    v = zc[:, _C_V:_C_V + C_WIDTH]
    wlo = zc[:, _C_WLO:_C_WLO + 2 * DECAY_RANK]
    alo = zc[:, _C_ALO:_C_ALO + 2 * ICLR_RANK]
    glo = zc[:, _C_GLO:_C_GLO + GATE_RANK]
    ones = ones_ref[...]

    g_ref[...] = _dot(_sigmoid(glo).astype(BF16), gup_ref[...])
    kk = k * kk_ref[...]
    norm = jnp.sqrt(_head_sum(kk * kk, ones))
    kk = kk / jnp.maximum(norm, 1e-12)
    xw = w0_ref[...] + _dot(jnp.tanh(wlo).astype(BF16), dup_ref[...])
    logw = -(jnp.maximum(-xw, 0.0) + jnp.log(1.0 + jnp.exp(-jnp.abs(xw)))) - 0.5
    decay = jnp.exp(-jnp.exp(logw))
    a = _sigmoid(a0_ref[...] + _dot(alo.astype(BF16), aup_ref[...]))
    ka = ka_ref[...]
    kd_sum = None
    for d in range(2):
        a_d = a[:, d * C_WIDTH:(d + 1) * C_WIDTH]
        k_d = k * (1.0 + (a_d - 1.0) * ka)
        w_ref[d] = decay[:, d * C_WIDTH:(d + 1) * C_WIDTH]
        b_ref[d] = kk * a_d
        k_ref[d] = k_d
        kd_sum = k_d if kd_sum is None else kd_sum + k_d
    bonus_ref[...] = _head_sum(r * kd_sum * rk_ref[...], ones) * v
    r_ref[0] = r
    v_ref[0] = v
    nkk_ref[0] = -kk


def _mixer_in(st, layer, x, mods, w_in, w_shift, dec_up, iclr_up, gate_up, w0, a0, k_k, k_a, r_k, ones_bd):
    tiles_per_halo = TM // _HALO
    n_halo = st.n_tok // _HALO
    row = lambda n: _resident((1, n))
    stacked = lambda d: (pl.BlockSpec((d, TM, C_WIDTH), lambda i: (0, i, 0)),
                         jax.ShapeDtypeStruct((d, st.n_tok, C_WIDTH), F32))
    flat = lambda w: (_tile(w), jax.ShapeDtypeStruct((st.n_tok, w), F32))
    outs = [flat(A_COLS), flat(B_COLS), flat(GATE_COLS)] + [stacked(1)] * 3 + [stacked(2)] * 3 + [flat(C_WIDTH)] * 2
    return pl.pallas_call(
        functools.partial(_mixer_in_kernel, tiles_per_seq=st.tiles_per_seq),
        grid=(st.tiles,),
        in_specs=[
            _tile(D_MODEL),
            pl.BlockSpec((_HALO, D_MODEL), lambda i: (jnp.maximum(i * tiles_per_halo - 1, 0), 0)),
            pl.BlockSpec((_HALO, D_MODEL), lambda i: (jnp.minimum((i + 1) * tiles_per_halo, n_halo - 1), 0)),
            _mod_spec(st, layer),
            _resident((D_MODEL, IN_COLS), layer),
            _resident((3, C_COLS)),
            _resident((2 * DECAY_RANK, 2 * C_WIDTH)),
            _resident((2 * ICLR_RANK, 2 * C_WIDTH)),
            _resident((GATE_RANK, C_WIDTH), layer),
            row(2 * C_WIDTH), row(2 * C_WIDTH), row(C_WIDTH), row(C_WIDTH), row(C_WIDTH),
            _resident((LANES, LANES)),
        ],
        out_specs=[o[0] for o in outs],
        out_shape=[o[1] for o in outs],
        compiler_params=_params("parallel"),
        name=f"mixer_in_{st.name}",
    )(x, x, x, mods, w_in, w_shift, dec_up, iclr_up, gate_up, w0, a0, k_k, k_a, r_k, ones_bd)


_PB = 2 * SCAN_NB


def _scan_kernel(wf, wb, af, ab, bf, bb, kf, kb, rf, rb, vf, vb, s0_ref, yf_ref, yb_ref, state,
                 ops_a, ops_b, red, wcum, y_a, y_b):
    @pl.when(pl.program_id(1) == 0)
    def _():
        state[...] = s0_ref[...]
        y_b[...] = jnp.zeros_like(y_b)

    wcum[...] = jnp.ones_like(wcum)
    left = lax.broadcasted_iota(jnp.int32, (_PB, LANES), 1) < HEAD_DIM
    low = lax.broadcasted_iota(jnp.int32, (SUBLANES, LANES), 0) < SUBLANES // 2

    def load_ops(i, ops):
        ir = SCAN_TB - 1 - i
        for p, (pf, pb, qf, qb) in enumerate(((wf, wb, af, ab), (bf, bb, kf, kb), (rf, rb, vf, vb))):
            xp = jnp.concatenate([pf[0, :, i, :], pb[0, :, ir, :]], axis=0)
            xq = jnp.concatenate([qf[0, :, i, :], qb[0, :, ir, :]], axis=0)
            blocks = []
            for j in range(C_HEADS // 2):
                pv = xp[:, j * LANES:(j + 1) * LANES]
                qv = xq[:, j * LANES:(j + 1) * LANES]
                blocks.append(jnp.where(left, pv, pltpu.roll(qv, HEAD_DIM, 1)))
                blocks.append(jnp.where(left, pltpu.roll(pv, HEAD_DIM, 1), qv))
            ops[p] = jnp.concatenate(blocks, axis=0).T

    def fold(x):
        return x.reshape(HEAD_DIM // SUBLANES, SUBLANES, LANES).sum(axis=0)

    def compute(ops, ybuf):
        w_prev = wcum[...]
        w_now = w_prev * ops[0, 0:HEAD_DIM]
        wcum[...] = w_now
        w_inv = 1.0 / w_now
        a = ops[0, HEAD_DIM:] * w_prev
        b = ops[1, 0:HEAD_DIM]
        k = ops[1, HEAD_DIM:]
        r = ops[2, 0:HEAD_DIM]
        br = jnp.sum(b * r, axis=0, keepdims=True)
        kr = jnp.sum(k * r, axis=0, keepdims=True)
        b = b * w_inv
        k = k * w_inv
        r = r * w_now
        for vi in range(HEAD_DIM):
            z = state[0, vi]
            p1 = fold(z * a)
            p2 = fold(z * r)
            t = jnp.where(low, p1, p2) + pltpu.roll(jnp.where(low, p2, p1), SUBLANES // 2, 0)
            t = t + pltpu.roll(t, SUBLANES - 2, 0)
            red[vi] = t + pltpu.roll(t, SUBLANES - 1, 0)
        for vi in range(HEAD_DIM):
            state[0, vi] = state[0, vi] + red[vi, 0:1] * b + ops[2, HEAD_DIM + vi:HEAD_DIM + vi + 1] * k
        ybuf[...] = red[:, SUBLANES // 2, :] + red[:, 0, :] * br + ops[2, HEAD_DIM:] * kr

    def emit(i, ybuf):
        y = ybuf[...]
        t = jnp.concatenate([y, y], axis=0).T
        cols = [jnp.where(left, t[(2 * j) * _PB:(2 * j + 1) * _PB], t[(2 * j + 1) * _PB:(2 * j + 2) * _PB])
                for j in range(C_HEADS // 2)]
        out = jnp.concatenate(cols, axis=1)
        yf_ref[:, i, :] = out[:SCAN_NB]
        yb_ref[:, SCAN_TB - 1 - i, :] = out[SCAN_NB:]

    load_ops(0, ops_a)

    def body(m, carry):
        i0 = 2 * m
        load_ops(i0 + 1, ops_b)
        compute(ops_a, y_a)
        emit(jnp.maximum(i0 - 1, 0), y_b)
        load_ops(jnp.minimum(i0 + 2, SCAN_TB - 1), ops_a)
        compute(ops_b, y_b)
        emit(i0, y_a)
        return carry

    lax.fori_loop(0, SCAN_TB // 2, body, 0)
    emit(SCAN_TB - 1, y_b)
    w_end = wcum[...]
    for vi in range(HEAD_DIM):
        state[0, vi] = state[0, vi] * w_end


def _scan(st, w, b, k, nkk, r, v, s0):
    groups = st.nb // SCAN_NB
    n_t = st.seq // SCAN_TB
    view = lambda x: x.reshape(x.shape[0], st.nb, st.seq, C_WIDTH)
    blk = (1, SCAN_NB, SCAN_TB, C_WIDTH)
    fwd = lambda d: pl.BlockSpec(blk, lambda g, t: (d, g, t, 0))
    bwd = lambda d: pl.BlockSpec(blk, lambda g, t: (d, g, n_t - 1 - t, 0))
    st_blk = (1, HEAD_DIM, HEAD_DIM, LANES)
    w, b, k, nkk, r, v = (view(x) for x in (w, b, k, nkk, r, v))
    tile = pltpu.VMEM((3, LANES, LANES), F32)
    ybuf = pltpu.VMEM((HEAD_DIM, LANES), F32)
    yblk = (SCAN_NB, SCAN_TB, C_WIDTH)
    y_shape = jax.ShapeDtypeStruct((st.nb, st.seq, C_WIDTH), F32)
    yf, yb, s_t = pl.pallas_call(
        _scan_kernel,
        grid=(groups, n_t),
        in_specs=[fwd(0), bwd(1), fwd(0), bwd(0), fwd(0), bwd(1), fwd(0), bwd(1), fwd(0), bwd(0), fwd(0), bwd(0),
                  pl.BlockSpec(st_blk, lambda g, t: (g, 0, 0, 0), pipeline_mode=pl.Buffered(1))],
        out_specs=[pl.BlockSpec(yblk, lambda g, t: (g, t, 0)), pl.BlockSpec(yblk, lambda g, t: (g, n_t - 1 - t, 0)),
                   pl.BlockSpec(st_blk, lambda g, t: (g, 0, 0, 0))],
        out_shape=[y_shape, y_shape, jax.ShapeDtypeStruct((groups,) + st_blk[1:], F32)],
        scratch_shapes=[tile, tile, pltpu.VMEM((HEAD_DIM, SUBLANES, LANES), F32), ybuf, ybuf, ybuf],
        compiler_params=_params("arbitrary", "arbitrary"),
        name=f"rwkv_scan_{st.name}",
    )(w, w, nkk, nkk, b, b, k, k, r, r, v, v, s0)
    return yf.reshape(st.n_tok, C_WIDTH), yb.reshape(st.n_tok, C_WIDTH), s_t


def _merge_ffn_kernel(x_ref, mod_ref, oa_ref, ob_ref, yf_ref, yb_ref, bonus_ref, g_ref, sg_ref, pa_ref, pb_ref, pc_ref,
                      wo_ref, ones_ref, gng_ref, gnb_ref, lng_ref, lnb_ref, w1_ref, w2_ref, lng2_ref, lnb2_ref, o_ref):
    y = yf_ref[...] + yb_ref[...]
    ones = ones_ref[...]
    mu = _head_sum(y, ones) * (1.0 / HEAD_DIM)
    d = y - mu
    var = _head_sum(d * d, ones) * (1.0 / HEAD_DIM)
    yn = d * lax.rsqrt(var + GN_EPS) * gng_ref[...] + gnb_ref[...]
    oc = ((yn + bonus_ref[...]) * g_ref[...]).astype(BF16)
    merged = (sg_ref[:, :D_MODEL] * _dot(oa_ref[...].astype(BF16), pa_ref[...])
              + sg_ref[:, D_MODEL:2 * D_MODEL] * _dot(ob_ref[...].astype(BF16), pb_ref[...])
              + sg_ref[:, 2 * D_MODEL:] * _dot(oc, pc_ref[...]))
    o = _dot(merged.astype(BF16), wo_ref[...])
    x = _layer_norm(ALPHA * x_ref[...] + mod_ref[0, 5:6, :] * o, lng_ref[...], lnb_ref[...])

    h = (x * (1.0 + mod_ref[0, 7:8, :]) + mod_ref[0, 6:7, :]).astype(BF16)
    a = _dot(h, w1_ref[:, :D_FF])
    g = _dot(h, w1_ref[:, D_FF:])
    u = (a * _sigmoid(a) * g).astype(BF16)
    f = _dot(u, w2_ref[...])
    o_ref[...] = _layer_norm(ALPHA * x + 0.5 * mod_ref[0, 8:9, :] * f, lng2_ref[...], lnb2_ref[...])


def _merge_ffn(st, layer, x, mods, oa, oa_col, ob, ob_col, yf, yb, bonus, g, sg, pa, pb, pc, wo, ones, gn_g, gn_b, ln_g, ln_b,
               w1, w2, ln_g2, ln_b2):
    vec = lambda a, n: a.reshape(1, n)
    return pl.pallas_call(
        _merge_ffn_kernel,
        grid=(st.tiles,),
        in_specs=[
            _tile(D_MODEL), _mod_spec(st, layer), _tile(A_Q, oa_col), _tile(B_W, ob_col), _tile(C_WIDTH),
            _tile(C_WIDTH), _tile(C_WIDTH), _tile(C_WIDTH), _tile(GATE_COLS),
            _resident((A_Q, D_MODEL), layer), _resident((B_W, D_MODEL), layer), _resident((C_WIDTH, D_MODEL), layer),
            _resident((D_MODEL, D_MODEL), layer), _resident((LANES, LANES)),
            _resident((1, C_WIDTH)), _resident((1, C_WIDTH)), _resident((1, D_MODEL)), _resident((1, D_MODEL)),
            _resident((D_MODEL, 2 * D_FF), layer), _resident((D_FF, D_MODEL), layer),
            _resident((1, D_MODEL)), _resident((1, D_MODEL)),
        ],
        out_specs=_tile(D_MODEL),
        out_shape=jax.ShapeDtypeStruct((st.n_tok, D_MODEL), F32),
        compiler_params=_params("parallel"),
        name=f"merge_ffn2_{st.name}",
    )(x, mods, oa, ob, yf, yb, bonus, g, sg, pa, pb, pc, wo, ones,
      vec(gn_g, C_WIDTH), vec(gn_b, C_WIDTH), vec(ln_g, D_MODEL), vec(ln_b, D_MODEL),
      w1, w2, vec(ln_g2, D_MODEL), vec(ln_b2, D_MODEL))


def _block_diag2(a, b):
    za = jnp.zeros((a.shape[0], b.shape[1]), a.dtype)
    zb = jnp.zeros((b.shape[0], a.shape[1]), b.dtype)
    return jnp.concatenate([jnp.concatenate([a, za], axis=1), jnp.concatenate([zb, b], axis=1)], axis=0)


def kernel(x_prompt, x_sample, cache_attn_k, cache_attn_v, cache_na_k, cache_na_v, state_rwkv, c, c_ctx, w_ada, b_ada, ffn1_w_in, ffn1_w_out, ffn2_w_in, ffn2_w_out, w_in, w_shift, attn_sink, na_rpb, decay_w0, decay_up, iclr_a0, iclr_up, gate_up, k_k, k_a, r_k, gn_g, gn_b, proj_a, proj_b, proj_c, w_out, ln_g, ln_b):
    xs = {CTX: x_prompt.reshape(CTX.n_tok, D_MODEL), LAT: x_sample.reshape(LAT.n_tok, D_MODEL)}
    c_rows = jnp.zeros((MOD_ROWS, D_MODEL), F32).at[:DEC_BATCH].set(c).at[DEC_BATCH].set(c_ctx)
    mods_all = _ada_mods(c_rows, w_ada, b_ada)
    cos_t, sin_t = _rope_tables()
    head_of = np.arange(LANES) // HEAD_DIM
    ones_bd = jnp.asarray(head_of[:, None] == head_of[None, :], BF16)
    ctx_groups = BATCH // SCAN_NB
    zero_state = jnp.zeros((ctx_groups, HEAD_DIM, HEAD_DIM, LANES), F32)

    bf = lambda w: w.astype(BF16)
    ffn1_w, ffn2_w = (bf(ffn1_w_in), bf(ffn1_w_out)), (bf(ffn2_w_in), bf(ffn2_w_out))
    w_in_bf, gate_up_bf = bf(w_in), bf(gate_up)
    proj_w = (bf(proj_a), bf(proj_b), bf(proj_c), bf(w_out))

    new_ka, new_va, new_kb, new_vb, new_st = [], [], [], [], []
    for l in range(DEPTH):
        mods = mods_all
        ffn1 = (*ffn1_w, ln_g[l, 0], ln_b[l, 0], 0)
        ffn2 = (*ffn2_w, ln_g[l, 2], ln_b[l, 2], 6)
        w_in_l = w_in_bf
        prep_w = (w_shift[l],
                  _block_diag2(decay_up[l, 0], decay_up[l, 1]).astype(BF16),
                  _block_diag2(iclr_up[l, 0], iclr_up[l, 1]).astype(BF16),
                  gate_up_bf,
                  decay_w0[l].reshape(1, 2 * C_WIDTH), iclr_a0[l].reshape(1, 2 * C_WIDTH),
                  k_k[l].reshape(1, C_WIDTH), k_a[l].reshape(1, C_WIDTH), r_k[l].reshape(1, C_WIDTH), ones_bd)
        merge_w = (*proj_w, ones_bd, gn_g[l], gn_b[l], ln_g[l, 1], ln_b[l, 1])
        s0_lat = state_rwkv[:, l].transpose(3, 4, 2, 1, 0).reshape(1, HEAD_DIM, HEAD_DIM, LANES)

        for st in (CTX, LAT):
            x = _ffn(st, l, xs[st], mods, *ffn1)
            za, zb, sg, r, v, nkk, w, b, k, bonus, g = _mixer_in(st, l, x, mods, w_in_l, *prep_w)
            if st is CTX:
                o_ab = _ctx_attn(za, zb, attn_sink[l])
                attn = (o_ab, 0, o_ab, 1)
                zac = za.reshape(BATCH, SEQ, A_COLS)
                zbc = zb.reshape(BATCH, SEQ, B_COLS)
                new_ka.append(zac[..., A_Q:A_Q + A_KV].reshape(BATCH, SEQ, A_KV_HEADS, HEAD_DIM))
                new_va.append(zac[..., A_Q + A_KV:].reshape(BATCH, SEQ, A_KV_HEADS, HEAD_DIM))
                new_kb.append(zbc[..., B_W:2 * B_W].reshape(BATCH, SEQ, B_HEADS, HEAD_DIM))
                new_vb.append(zbc[..., 2 * B_W:].reshape(BATCH, SEQ, B_HEADS, HEAD_DIM))
            else:
                oa = _win_attn(za, cache_attn_k[:, l].reshape(DEC_BATCH, PAST_LEN, A_KV),
                               cache_attn_v[:, l].reshape(DEC_BATCH, PAST_LEN, A_KV), attn_sink[l], cos_t, sin_t)
                ob = _na_attn(zb, cache_na_k[:, l].reshape(DEC_BATCH, PAST_LEN, B_W),
                              cache_na_v[:, l].reshape(DEC_BATCH, PAST_LEN, B_W), na_rpb[l].reshape(-1))
                attn = (oa, 0, ob, 0)
            yf, yb, s_t = _scan(st, w, b, k, nkk, r, v, zero_state if st is CTX else s0_lat)
            if st is CTX:
                s_t = s_t.reshape(ctx_groups, HEAD_DIM, HEAD_DIM, C_HEADS, 2, SCAN_NB)
                new_st.append(s_t.transpose(0, 5, 4, 3, 1, 2).reshape(BATCH, 2, C_HEADS, HEAD_DIM, HEAD_DIM))
            xs[st] = _merge_ffn(st, l, x, mods, *attn, yf, yb, bonus, g, sg, *merge_w, *ffn2[:4])

    y_p = xs[CTX].reshape(BATCH, SEQ, D_MODEL)
    y_s = xs[LAT].reshape(DEC_BATCH, DEC_SEQ, D_MODEL)
    return (y_p, y_s, jnp.stack(new_ka, axis=1), jnp.stack(new_va, axis=1), jnp.stack(new_kb, axis=1),
            jnp.stack(new_vb, axis=1), jnp.stack(new_st, axis=1))
```

```python
import collections
import functools

import jax
import jax.numpy as jnp
import numpy as np
from jax import lax
from jax.experimental import pallas as pl
from jax.experimental.pallas import tpu as pltpu

D_MODEL = 1024
BATCH = 16
SEQ = 256
DEPTH = 2
DEC_BATCH = 8
DEC_SEQ = 1024
PAST_LEN = 256
GRID_W = 64
GRID_ROWS = DEC_SEQ // GRID_W
HEAD_DIM = 64
SCALE = HEAD_DIM ** -0.5
A_HEADS = 8
A_KV_HEADS = 2
A_GROUP = A_HEADS // A_KV_HEADS
A_WINDOW = 128
A_BLOCK = 128
A_Q = A_HEADS * HEAD_DIM
A_KV = A_KV_HEADS * HEAD_DIM
A_COLS = A_Q + 2 * A_KV
B_HEADS = 8
B_W = B_HEADS * HEAD_DIM
B_COLS = 3 * B_W
NA_ROWS = 8
NA_COLS = 16
C_HEADS = 8
C_WIDTH = C_HEADS * HEAD_DIM
DECAY_RANK = 64
ICLR_RANK = 64
GATE_RANK = 128
C_COLS = 3 * C_WIDTH + 2 * DECAY_RANK + 2 * ICLR_RANK + GATE_RANK
GN_EPS = 64e-5
GATE_COLS = 3 * D_MODEL
IN_COLS = A_COLS + B_COLS + C_COLS + GATE_COLS
D_FF = 2816
N_MOD = 9
ROPE_BASE = 10000.0
LN_EPS = 1e-5
NEG_INF = -1e30
ALPHA = (2 * DEPTH) ** 0.25

LANES = 128
SUBLANES = 8
TM = 256
MOD_ROWS = 16
SCAN_TB = 32
SCAN_NB = LANES // (2 * C_HEADS)
VMEM_LIMIT = 56 * 1024 * 1024

F32 = jnp.float32
BF16 = jnp.bfloat16

Stream = collections.namedtuple("Stream", "name nb seq n_tok tiles tiles_per_seq mod_row")


def _make_stream(name, nb, seq, mod_row):
    return Stream(name, nb, seq, nb * seq, nb * seq // TM, seq // TM, mod_row)


CTX = _make_stream("ctx", BATCH, SEQ, lambda i: DEC_BATCH)
LAT = _make_stream("lat", DEC_BATCH, DEC_SEQ, lambda i: i // (DEC_SEQ // TM))


def _sigmoid(x):
    return 1.0 / (1.0 + jnp.exp(-x))


def _layer_norm(y, g, b):
    mu = jnp.mean(y, axis=-1, keepdims=True)
    d = y - mu
    var = jnp.mean(d * d, axis=-1, keepdims=True)
    return d * lax.rsqrt(var + LN_EPS) * g + b


def _dot(a, b):
    return jnp.dot(a, b, preferred_element_type=F32)


def _dot_nt(a, b):
    return lax.dot_general(a, b, (((1,), (1,)), ((), ())), preferred_element_type=F32)


def _head_sum(x, same_head):
    hi = x.astype(BF16)
    lo = (x - hi.astype(F32)).astype(BF16)
    slabs = [_dot(hi[:, j:j + LANES], same_head) + _dot(lo[:, j:j + LANES], same_head)
             for j in range(0, x.shape[-1], LANES)]
    return jnp.concatenate(slabs, axis=-1)


def _resident(shape, layer=None):
    zeros = (0,) * len(shape)
    if layer is None:
        return pl.BlockSpec(shape, lambda *_: zeros, pipeline_mode=pl.Buffered(1))
    return pl.BlockSpec((None,) + tuple(shape), lambda *_: (layer,) + zeros, pipeline_mode=pl.Buffered(1))


def _tile(width, col=0):
    return pl.BlockSpec((TM, width), lambda i: (i, col))


def _mod_spec(st, layer):
    return pl.BlockSpec((None, 1, N_MOD, D_MODEL), lambda i: (layer, st.mod_row(i), 0, 0))


def _params(*sem):
    return pltpu.CompilerParams(dimension_semantics=sem, vmem_limit_bytes=VMEM_LIMIT)


def _ada_kernel(c_ref, w_ref, b_ref, o_ref):
    c = c_ref[...]
    s = (c * _sigmoid(c)).astype(BF16)
    o_ref[0] = _dot(s, w_ref[0].astype(BF16)) + b_ref[0]


def _ada_mods(c_rows, w_ada, b_ada):
    out = pl.pallas_call(
        _ada_kernel,
        grid=(DEPTH, N_MOD),
        in_specs=[
            pl.BlockSpec((MOD_ROWS, D_MODEL), lambda l, j: (0, 0)),
            pl.BlockSpec((1, D_MODEL, D_MODEL), lambda l, j: (l, 0, j)),
            pl.BlockSpec((1, 1, D_MODEL), lambda l, j: (l, 0, j)),
        ],
        out_specs=pl.BlockSpec((1, MOD_ROWS, D_MODEL), lambda l, j: (l, 0, j)),
        out_shape=jax.ShapeDtypeStruct((DEPTH, MOD_ROWS, N_MOD * D_MODEL), F32),
        compiler_params=_params("parallel", "parallel"),
        name="ada_mods",
    )(c_rows, w_ada, b_ada.reshape(DEPTH, 1, N_MOD * D_MODEL))
    return out.reshape(DEPTH, MOD_ROWS, N_MOD, D_MODEL)


def _ffn_kernel(x_ref, mod_ref, w1_ref, w2_ref, g_ref, b_ref, o_ref, *, base):
    x = x_ref[...]
    shift = mod_ref[0, base:base + 1, :]
    scale = mod_ref[0, base + 1:base + 2, :]
    gate = mod_ref[0, base + 2:base + 3, :]
    h = (x * (1.0 + scale) + shift).astype(BF16)
    a = _dot(h, w1_ref[:, :D_FF])
    g = _dot(h, w1_ref[:, D_FF:])
    u = (a * _sigmoid(a) * g).astype(BF16)
    f = _dot(u, w2_ref[...])
    o_ref[...] = _layer_norm(ALPHA * x + 0.5 * gate * f, g_ref[...], b_ref[...])


def _ffn(st, layer, x, mods, w1, w2, ln_g, ln_b, base):
    return pl.pallas_call(
        functools.partial(_ffn_kernel, base=base),
        grid=(st.tiles,),
        in_specs=[
            _tile(D_MODEL), _mod_spec(st, layer),
            _resident((D_MODEL, 2 * D_FF), layer), _resident((D_FF, D_MODEL), layer),
            _resident((1, D_MODEL)), _resident((1, D_MODEL)),
        ],
        out_specs=_tile(D_MODEL),
        out_shape=jax.ShapeDtypeStruct((st.n_tok, D_MODEL), F32),
        compiler_params=_params("parallel"),
        name=f"ffn{base // 6 + 1}_{st.name}",
    )(x, mods, w1, w2, ln_g.reshape(1, D_MODEL), ln_b.reshape(1, D_MODEL))


_COL_A = 0
_COL_B = A_COLS
_COL_C = A_COLS + B_COLS
_COL_G = A_COLS + B_COLS + C_COLS


def _softmax_pv(scores, values, sink=None):
    m = scores[0].max(axis=-1, keepdims=True)
    for s in scores[1:]:
        m = jnp.maximum(m, s.max(axis=-1, keepdims=True))
    if sink is not None:
        m = jnp.maximum(m, sink)
    den = jnp.exp(sink - m) if sink is not None else 0.0
    acc = None
    for s, v in zip(scores, values):
        p = jnp.exp(s - m)
        den = den + p.sum(axis=-1, keepdims=True)
        pv = _dot(p.astype(BF16), v)
        acc = pv if acc is None else acc + pv
    return acc / den


_PW = 2 * HEAD_DIM


def _left_half(rows):
    return lax.broadcasted_iota(jnp.int32, (rows, _PW), 1) < HEAD_DIM


def _split_heads(x, left):
    return jnp.concatenate([jnp.where(left, x, 0.0), jnp.where(left, 0.0, x)], axis=0).astype(BF16)


def _both_halves(slab, swapped, left, head):
    return (jnp.where(left, slab, swapped) if head == 0 else jnp.where(left, swapped, slab)).astype(BF16)


def _ctx_attn_kernel(sink_ref, za_ref, zb_ref, o_ref):
    left = _left_half(SEQ)
    slab = lambda ref, j: ref[:, j * _PW:(j + 1) * _PW]
    k_slab = slab(za_ref, A_Q // _PW)
    v_slab = slab(za_ref, (A_Q + A_KV) // _PW)
    k_swap = pltpu.roll(k_slab, HEAD_DIM, 1)
    v_swap = pltpu.roll(v_slab, HEAD_DIM, 1)
    row = lax.broadcasted_iota(jnp.int32, (A_GROUP * SEQ, 1), 0) // SEQ
    for g in range(A_KV_HEADS):
        slabs = range(g * A_GROUP // 2, (g + 1) * A_GROUP // 2)
        q4 = jnp.concatenate([_split_heads(slab(za_ref, j), left) for j in slabs], axis=0)
        sink = jnp.zeros((A_GROUP * SEQ, 1), F32)
        for j in range(A_GROUP):
            sink = jnp.where(row == j, sink_ref[A_GROUP * g + j], sink)
        o4 = _softmax_pv([_dot_nt(q4, _both_halves(k_slab, k_swap, left, g)) * SCALE],
                         [_both_halves(v_slab, v_swap, left, g)], sink)
        for i, j in enumerate(slabs):
            o_ref[:, j * _PW:(j + 1) * _PW] = jnp.where(left, o4[2 * i * SEQ:(2 * i + 1) * SEQ],
                                                        o4[(2 * i + 1) * SEQ:(2 * i + 2) * SEQ])
    for j in range(B_HEADS // 2):
        k2 = slab(zb_ref, B_HEADS // 2 + j).astype(BF16)
        v2 = slab(zb_ref, B_HEADS + j).astype(BF16)
        o2 = _softmax_pv([_dot_nt(_split_heads(slab(zb_ref, j), left), k2) * SCALE], [v2])
        o_ref[:, A_Q + j * _PW:A_Q + (j + 1) * _PW] = jnp.where(left, o2[:SEQ], o2[SEQ:])


def _ctx_attn(za, zb, sink):
    return pl.pallas_call(
        _ctx_attn_kernel,
        grid=(BATCH,),
        in_specs=[
            pl.BlockSpec(memory_space=pltpu.SMEM),
            pl.BlockSpec((SEQ, A_COLS), lambda b: (b, 0)),
            pl.BlockSpec((SEQ, B_COLS), lambda b: (b, 0)),
        ],
        out_specs=pl.BlockSpec((SEQ, A_Q + B_W), lambda b: (b, 0)),
        out_shape=jax.ShapeDtypeStruct((CTX.n_tok, A_Q + B_W), F32),
        compiler_params=_params("parallel"),
        name="ctx_attn",
    )(sink, za, zb)


def _rope(x, cos, sin_signed):
    w = x.shape[-1]
    half = HEAD_DIM // 2
    lane = lax.broadcasted_iota(jnp.int32, x.shape, 1)
    partner = jnp.where(lane % HEAD_DIM < half, pltpu.roll(x, w - half, 1), pltpu.roll(x, half, 1))
    return x * cos + partner * sin_signed


def _win_attn_kernel(sink_ref, za_ref, ck_ref, cv_ref, band_ref, o_ref):
    n = pl.program_id(1)
    nb = DEC_SEQ // A_BLOCK
    q0 = pl.multiple_of(n * A_BLOCK, A_BLOCK)
    q = za_ref[pl.ds(q0, A_BLOCK), :A_Q]
    ks, vs = [], []
    for j in range(3):
        kb = jnp.clip(n - 1 + j, 0, nb - 1)
        k0 = pl.multiple_of(kb * A_BLOCK, A_BLOCK)
        ks.append(za_ref[pl.ds(k0, A_BLOCK), A_Q:A_Q + A_KV])
        vs.append(za_ref[pl.ds(k0, A_BLOCK), A_Q + A_KV:])
    left = _left_half(A_BLOCK)
    bands = []
    for x in (jnp.concatenate(ks, axis=0), jnp.concatenate(vs, axis=0), ck_ref[0], cv_ref[0]):
        bands.append((x, pltpu.roll(x, HEAD_DIM, 1), _left_half(x.shape[0])))
    m4 = A_GROUP * A_BLOCK
    kpos = (n - 1) * A_BLOCK + lax.broadcasted_iota(jnp.int32, (1, 3 * A_BLOCK), 1)
    band = band_ref[...] + jnp.where((kpos >= 0) & (kpos < DEC_SEQ), 0.0, NEG_INF)
    row = lax.broadcasted_iota(jnp.int32, (m4, 1), 0) // A_BLOCK
    for g in range(A_KV_HEADS):
        kband, vband, ck, cv = (_both_halves(x, swapped, lft, g) for x, swapped, lft in bands)
        slabs = range(g * A_GROUP // 2, (g + 1) * A_GROUP // 2)
        q4 = jnp.concatenate([_split_heads(q[:, j * _PW:(j + 1) * _PW], left) for j in slabs], axis=0)
        s_loc = _dot_nt(q4, kband) * SCALE + band
        s_ctx = _dot_nt(q4, ck) * SCALE
        sink = jnp.zeros((m4, 1), F32)
        for j in range(A_GROUP):
            sink = jnp.where(row == j, sink_ref[A_GROUP * g + j], sink)
        o4 = _softmax_pv([s_loc, s_ctx], [vband, cv], sink)
        for i, j in enumerate(slabs):
            o_ref[:, j * _PW:(j + 1) * _PW] = jnp.where(left, o4[2 * i * A_BLOCK:(2 * i + 1) * A_BLOCK],
                                                        o4[(2 * i + 1) * A_BLOCK:(2 * i + 2) * A_BLOCK])


def _rope_tables():
    t = np.arange(DEC_SEQ)
    n_freq = HEAD_DIM // 4
    inv = ROPE_BASE ** (-jnp.arange(n_freq, dtype=F32) / n_freq)
    rows = jnp.asarray(t // GRID_W, F32)
    cols = jnp.asarray(t % GRID_W, F32)
    ang = jnp.concatenate([rows[:, None] * inv, cols[:, None] * inv], axis=-1)
    cos, sin = jnp.cos(ang), jnp.sin(ang)
    cos_h = jnp.concatenate([cos, cos], axis=-1)
    sin_h = jnp.concatenate([-sin, sin], axis=-1)
    return jnp.tile(cos_h, (1, A_HEADS)), jnp.tile(sin_h, (1, A_HEADS))


def _window_band():
    qi = np.arange(A_GROUP * A_BLOCK)[:, None] % A_BLOCK
    kj = np.arange(3 * A_BLOCK)[None, :]
    return jnp.asarray(np.where(np.abs(kj - A_BLOCK - qi) <= A_WINDOW, 0.0, NEG_INF), F32)


def _win_attn(za, ck, cv, sink):
    nb = DEC_SEQ // A_BLOCK
    return pl.pallas_call(
        _win_attn_kernel,
        grid=(DEC_BATCH, nb),
        in_specs=[
            pl.BlockSpec(memory_space=pltpu.SMEM),
            pl.BlockSpec((DEC_SEQ, A_COLS), lambda b, n: (b, 0)),
            pl.BlockSpec((1, PAST_LEN, A_KV), lambda b, n: (b, 0, 0)),
            pl.BlockSpec((1, PAST_LEN, A_KV), lambda b, n: (b, 0, 0)),
            pl.BlockSpec((A_GROUP * A_BLOCK, 3 * A_BLOCK), lambda b, n: (0, 0)),
        ],
        out_specs=pl.BlockSpec((A_BLOCK, A_Q), lambda b, n: (b * nb + n, 0)),
        out_shape=jax.ShapeDtypeStruct((LAT.n_tok, A_Q), F32),
        compiler_params=_params("parallel", "arbitrary"),
        name="win_attn",
    )(sink, za, ck, cv, _window_band())


_NA_QBLK = 256
_RPB_I = 2 * NA_ROWS - 1
_RPB_J = 2 * NA_COLS - 1


def _na_build_bias(rpb_ref, head, bias_ref, slot):
    shape = (GRID_W, 2 * GRID_W)
    qc = lax.broadcasted_iota(jnp.int32, shape, 0)
    lane = lax.broadcasted_iota(jnp.int32, shape, 1)
    kc = lane % GRID_W
    left = lane < GRID_W
    cs = jnp.clip(qc - NA_COLS // 2, 0, GRID_W - NA_COLS)
    idx = jnp.where((kc >= cs) & (kc < cs + NA_COLS), jnp.clip(kc - qc + NA_COLS - 1, 0, _RPB_J - 1), -1)
    base = head * (_RPB_I * _RPB_J)
    tables = []
    for i in range(_RPB_I):
        t = jnp.full(shape, NEG_INF, F32)
        for j in range(_RPB_J):
            t = jnp.where(idx == j, rpb_ref[base + i * _RPB_J + j], t)
        tables.append(t)
    neg = jnp.full(shape, NEG_INF, F32)
    half = NA_ROWS // 2
    for qr in range(GRID_ROWS):
        rs = min(max(qr - half, 0), GRID_ROWS - NA_ROWS)
        for m in range(GRID_ROWS // 2):
            kr0, kr1 = 2 * m, 2 * m + 1
            v0 = rs <= kr0 < rs + NA_ROWS
            v1 = rs <= kr1 < rs + NA_ROWS
            i0 = kr0 - qr + NA_ROWS - 1
            if v0 and v1:
                piece = jnp.where(left, tables[i0], tables[i0 + 1])
            elif v0:
                piece = jnp.where(left, tables[i0], NEG_INF)
            elif v1:
                piece = jnp.where(left, NEG_INF, tables[i0 + 1])
            else:
                piece = neg
            bias_ref[slot, qr * GRID_W:(qr + 1) * GRID_W, m * 2 * GRID_W:(m + 1) * 2 * GRID_W] = piece


def _na_attn_kernel(rpb_ref, q_ref, k_ref, v_ref, ck_ref, cv_ref, o_ref, bias_ref):
    hp = pl.program_id(0)

    @pl.when(pl.program_id(1) == 0)
    def _():
        for half in range(2):
            _na_build_bias(rpb_ref, 2 * hp + half, bias_ref, half)

    ck2 = ck_ref[0].astype(BF16)
    cv2 = cv_ref[0].astype(BF16)
    left = _left_half(_NA_QBLK)
    rows_per_blk = _NA_QBLK // GRID_W
    for qb in range(DEC_SEQ // _NA_QBLK):
        rows = slice(qb * _NA_QBLK, (qb + 1) * _NA_QBLK)
        r_lo = min(max(qb * rows_per_blk - NA_ROWS // 2, 0), GRID_ROWS - NA_ROWS)
        r_hi = min(max((qb + 1) * rows_per_blk - 1 - NA_ROWS // 2, 0), GRID_ROWS - NA_ROWS) + NA_ROWS
        keys = slice(r_lo // 2 * 2 * GRID_W, -(-r_hi // 2) * 2 * GRID_W)
        q2 = _split_heads(q_ref[rows, :], left)
        bias = jnp.concatenate([bias_ref[0, rows, keys], bias_ref[1, rows, keys]], axis=0)
        s_loc = _dot_nt(q2, k_ref[keys, :].astype(BF16)) * SCALE + bias
        s_ctx = _dot_nt(q2, ck2) * SCALE
        o2 = _softmax_pv([s_loc, s_ctx], [v_ref[keys, :].astype(BF16), cv2])
        o_ref[rows, :] = jnp.where(left, o2[:_NA_QBLK], o2[_NA_QBLK:])


def _na_attn(zb, ck, cv, rpb_flat):
    pairs = B_HEADS // 2
    pw = _PW
    return pl.pallas_call(
        _na_attn_kernel,
        grid=(pairs, DEC_BATCH),
        in_specs=[
            pl.BlockSpec(memory_space=pltpu.SMEM),
            pl.BlockSpec((DEC_SEQ, pw), lambda hp, b: (b, hp)),
            pl.BlockSpec((DEC_SEQ, pw), lambda hp, b: (b, pairs + hp)),
            pl.BlockSpec((DEC_SEQ, pw), lambda hp, b: (b, 2 * pairs + hp)),
            pl.BlockSpec((1, PAST_LEN, pw), lambda hp, b: (b, 0, hp)),
            pl.BlockSpec((1, PAST_LEN, pw), lambda hp, b: (b, 0, hp)),
        ],
        out_specs=pl.BlockSpec((DEC_SEQ, pw), lambda hp, b: (b, hp)),
        out_shape=jax.ShapeDtypeStruct((LAT.n_tok, B_W), F32),
        scratch_shapes=[pltpu.VMEM((2, DEC_SEQ, DEC_SEQ), F32)],
        compiler_params=_params("arbitrary", "arbitrary"),
        name="na_attn",
    )(rpb_flat, zb, zb, zb, ck, cv)


_C_R, _C_K, _C_V = 0, C_WIDTH, 2 * C_WIDTH
_C_WLO = 3 * C_WIDTH
_C_ALO = _C_WLO + 2 * DECAY_RANK
_C_GLO = _C_ALO + 2 * ICLR_RANK
_HALO = SUBLANES


def _mixer_in_kernel(*refs, tiles_per_seq, rope):
    (x_ref, xp_ref, xn_ref, mod_ref, win_ref, wsh_ref, dup_ref, aup_ref, gup_ref, w0_ref, a0_ref,
     kk_ref, ka_ref, rk_ref, ones_ref) = refs[:15]
    rope_refs = refs[15:17] if rope else ()
    (za_ref, zb_ref, sg_ref, r_ref, v_ref, nkk_ref, w_ref, b_ref, k_ref, bonus_ref, g_ref) = refs[15 + len(rope_refs):]
    j = pl.program_id(0) % tiles_per_seq
    scale = 1.0 + mod_ref[0, 4:5, :]
    shift = mod_ref[0, 3:4, :]
    x = x_ref[...]
    h = (x * scale + shift).astype(BF16)
    za = _dot(h, win_ref[:, _COL_A:_COL_B])
    if rope:
        cos_ref, sin_ref = rope_refs
        za_ref[:, :A_Q] = _rope(za[:, :A_Q], cos_ref[...], sin_ref[...])
        za_ref[:, A_Q:A_Q + A_KV] = _rope(za[:, A_Q:A_Q + A_KV], cos_ref[:, :A_KV], sin_ref[:, :A_KV])
        za_ref[:, A_Q + A_KV:] = za[:, A_Q + A_KV:]
    else:
        za_ref[...] = za
    zb_ref[...] = _dot(h, win_ref[:, _COL_B:_COL_C])
    sg_ref[...] = _sigmoid(_dot(h, win_ref[:, _COL_G:]))

    x_ext = jnp.concatenate([xp_ref[...], x, xn_ref[...]], axis=0)
    z_ext = _dot((x_ext * scale + shift).astype(BF16), win_ref[:, _COL_C:_COL_G])
    row = lax.broadcasted_iota(jnp.int32, (TM, C_COLS), 0)
    z = z_ext[_HALO:_HALO + TM]
    z_prev = jnp.where(jnp.logical_and(row == 0, j == 0), 0.0, z_ext[_HALO - 1:_HALO - 1 + TM])
    z_next = jnp.where(jnp.logical_and(row == TM - 1, j == tiles_per_seq - 1), 0.0, z_ext[_HALO + 1:_HALO + 1 + TM])
    zc = z_prev * wsh_ref[0:1, :] + z * wsh_ref[1:2, :] + z_next * wsh_ref[2:3, :]

    r = zc[:, _C_R:_C_R + C_WIDTH]
    k = zc[:, _C_K:_C_K + C_WIDTH]
    v = zc[:, _C_V:_C_V + C_WIDTH]
    wlo = zc[:, _C_WLO:_C_WLO + 2 * DECAY_RANK]
    alo = zc[:, _C_ALO:_C_ALO + 2 * ICLR_RANK]
    glo = zc[:, _C_GLO:_C_GLO + GATE_RANK]
    ones = ones_ref[...]

    g_ref[...] = _dot(_sigmoid(glo).astype(BF16), gup_ref[...])
    kk = k * kk_ref[...]
    norm = jnp.sqrt(_head_sum(kk * kk, ones))
    kk = kk / jnp.maximum(norm, 1e-12)
    xw = w0_ref[...] + _dot(jnp.tanh(wlo).astype(BF16), dup_ref[...])
    logw = -(jnp.maximum(-xw, 0.0) + jnp.log(1.0 + jnp.exp(-jnp.abs(xw)))) - 0.5
    decay = jnp.exp(-jnp.exp(logw))
    a = _sigmoid(a0_ref[...] + _dot(alo.astype(BF16), aup_ref[...]))
    ka = ka_ref[...]
    kd_sum = None
    for d in range(2):
        a_d = a[:, d * C_WIDTH:(d + 1) * C_WIDTH]
        k_d = k * (1.0 + (a_d - 1.0) * ka)
        w_ref[d] = decay[:, d * C_WIDTH:(d + 1) * C_WIDTH]
        b_ref[d] = kk * a_d
        k_ref[d] = k_d
        kd_sum = k_d if kd_sum is None else kd_sum + k_d
    bonus_ref[...] = _head_sum(r * kd_sum * rk_ref[...], ones) * v
    r_ref[0] = r
    v_ref[0] = v
    nkk_ref[0] = -kk


def _mixer_in(st, layer, x, mods, w_in, w_shift, dec_up, iclr_up, gate_up, w0, a0, k_k, k_a, r_k, ones_bd, rope=None):
    tiles_per_halo = TM // _HALO
    n_halo = st.n_tok // _HALO
    row = lambda n: _resident((1, n))
    stacked = lambda d: (pl.BlockSpec((d, TM, C_WIDTH), lambda i: (0, i, 0)),
                         jax.ShapeDtypeStruct((d, st.n_tok, C_WIDTH), F32))
    flat = lambda w: (_tile(w), jax.ShapeDtypeStruct((st.n_tok, w), F32))
    outs = [flat(A_COLS), flat(B_COLS), flat(GATE_COLS)] + [stacked(1)] * 3 + [stacked(2)] * 3 + [flat(C_WIDTH)] * 2
    rope_specs = [pl.BlockSpec((TM, A_Q), lambda i: (i % st.tiles_per_seq, 0))] * 2 if rope else []
    return pl.pallas_call(
        functools.partial(_mixer_in_kernel, tiles_per_seq=st.tiles_per_seq, rope=bool(rope)),
        grid=(st.tiles,),
        in_specs=[
            _tile(D_MODEL),
            pl.BlockSpec((_HALO, D_MODEL), lambda i: (jnp.maximum(i * tiles_per_halo - 1, 0), 0)),
            pl.BlockSpec((_HALO, D_MODEL), lambda i: (jnp.minimum((i + 1) * tiles_per_halo, n_halo - 1), 0)),
            _mod_spec(st, layer),
            _resident((D_MODEL, IN_COLS), layer),
            _resident((3, C_COLS)),
            _resident((2 * DECAY_RANK, 2 * C_WIDTH)),
            _resident((2 * ICLR_RANK, 2 * C_WIDTH)),
            _resident((GATE_RANK, C_WIDTH), layer),
            row(2 * C_WIDTH), row(2 * C_WIDTH), row(C_WIDTH), row(C_WIDTH), row(C_WIDTH),
            _resident((LANES, LANES)),
        ] + rope_specs,
        out_specs=[o[0] for o in outs],
        out_shape=[o[1] for o in outs],
        compiler_params=_params("parallel"),
        name=f"mixer_in_{st.name}",
    )(x, x, x, mods, w_in, w_shift, dec_up, iclr_up, gate_up, w0, a0, k_k, k_a, r_k, ones_bd, *(rope or ()))


_PB = 2 * SCAN_NB


def _scan_kernel(wf, wb, af, ab, bf, bb, kf, kb, rf, rb, vf, vb, s0_ref, yf_ref, yb_ref, state,
                 ops_a, ops_b, red, wcum, y_a, y_b):
    @pl.when(pl.program_id(1) == 0)
    def _():
        state[...] = s0_ref[...]
        y_b[...] = jnp.zeros_like(y_b)

    wcum[...] = jnp.ones_like(wcum)
    left = lax.broadcasted_iota(jnp.int32, (_PB, LANES), 1) < HEAD_DIM
    low = lax.broadcasted_iota(jnp.int32, (SUBLANES, LANES), 0) < SUBLANES // 2

    def load_ops(i, ops):
        ir = SCAN_TB - 1 - i
        for p, (pf, pb, qf, qb) in enumerate(((wf, wb, af, ab), (bf, bb, kf, kb), (rf, rb, vf, vb))):
            xp = jnp.concatenate([pf[0, :, i, :], pb[0, :, ir, :]], axis=0)
            xq = jnp.concatenate([qf[0, :, i, :], qb[0, :, ir, :]], axis=0)
            blocks = []
            for j in range(C_HEADS // 2):
                pv = xp[:, j * LANES:(j + 1) * LANES]
                qv = xq[:, j * LANES:(j + 1) * LANES]
                blocks.append(jnp.where(left, pv, pltpu.roll(qv, HEAD_DIM, 1)))
                blocks.append(jnp.where(left, pltpu.roll(pv, HEAD_DIM, 1), qv))
            ops[p] = jnp.concatenate(blocks, axis=0).T

    def fold(x):
        return x.reshape(HEAD_DIM // SUBLANES, SUBLANES, LANES).sum(axis=0)

    def compute(ops, ybuf):
        w_prev = wcum[...]
        w_now = w_prev * ops[0, 0:HEAD_DIM]
        wcum[...] = w_now
        w_inv = 1.0 / w_now
        a = ops[0, HEAD_DIM:] * w_prev
        b = ops[1, 0:HEAD_DIM]
        k = ops[1, HEAD_DIM:]
        r = ops[2, 0:HEAD_DIM]
        br = jnp.sum(b * r, axis=0, keepdims=True)
        kr = jnp.sum(k * r, axis=0, keepdims=True)
        b = b * w_inv
        k = k * w_inv
        r = r * w_now
        for vi in range(HEAD_DIM):
            z = state[0, vi]
            p1 = fold(z * a)
            p2 = fold(z * r)
            t = jnp.where(low, p1, p2) + pltpu.roll(jnp.where(low, p2, p1), SUBLANES // 2, 0)
            t = t + pltpu.roll(t, SUBLANES - 2, 0)
            red[vi] = t + pltpu.roll(t, SUBLANES - 1, 0)
        for vi in range(HEAD_DIM):
            state[0, vi] = state[0, vi] + red[vi, 0:1] * b + ops[2, HEAD_DIM + vi:HEAD_DIM + vi + 1] * k
        ybuf[...] = red[:, SUBLANES // 2, :] + red[:, 0, :] * br + ops[2, HEAD_DIM:] * kr

    def emit(i, ybuf):
        y = ybuf[...]
        t = jnp.concatenate([y, y], axis=0).T
        cols = [jnp.where(left, t[(2 * j) * _PB:(2 * j + 1) * _PB], t[(2 * j + 1) * _PB:(2 * j + 2) * _PB])
                for j in range(C_HEADS // 2)]
        out = jnp.concatenate(cols, axis=1)
        yf_ref[:, i, :] = out[:SCAN_NB]
        yb_ref[:, SCAN_TB - 1 - i, :] = out[SCAN_NB:]

    load_ops(0, ops_a)

    def body(m, carry):
        i0 = 2 * m
        load_ops(i0 + 1, ops_b)
        compute(ops_a, y_a)
        emit(jnp.maximum(i0 - 1, 0), y_b)
        load_ops(jnp.minimum(i0 + 2, SCAN_TB - 1), ops_a)
        compute(ops_b, y_b)
        emit(i0, y_a)
        return carry

    lax.fori_loop(0, SCAN_TB // 2, body, 0)
    emit(SCAN_TB - 1, y_b)
    w_end = wcum[...]
    for vi in range(HEAD_DIM):
        state[0, vi] = state[0, vi] * w_end


def _scan(st, w, b, k, nkk, r, v, s0):
    groups = st.nb // SCAN_NB
    n_t = st.seq // SCAN_TB
    view = lambda x: x.reshape(x.shape[0], st.nb, st.seq, C_WIDTH)
    blk = (1, SCAN_NB, SCAN_TB, C_WIDTH)
    fwd = lambda d: pl.BlockSpec(blk, lambda g, t: (d, g, t, 0))
    bwd = lambda d: pl.BlockSpec(blk, lambda g, t: (d, g, n_t - 1 - t, 0))
    st_blk = (1, HEAD_DIM, HEAD_DIM, LANES)
    w, b, k, nkk, r, v = (view(x) for x in (w, b, k, nkk, r, v))
    tile = pltpu.VMEM((3, LANES, LANES), F32)
    ybuf = pltpu.VMEM((HEAD_DIM, LANES), F32)
    yblk = (SCAN_NB, SCAN_TB, C_WIDTH)
    y_shape = jax.ShapeDtypeStruct((st.nb, st.seq, C_WIDTH), F32)
    yf, yb, s_t = pl.pallas_call(
        _scan_kernel,
        grid=(groups, n_t),
        in_specs=[fwd(0), bwd(1), fwd(0), bwd(0), fwd(0), bwd(1), fwd(0), bwd(1), fwd(0), bwd(0), fwd(0), bwd(0),
                  pl.BlockSpec(st_blk, lambda g, t: (g, 0, 0, 0), pipeline_mode=pl.Buffered(1))],
        out_specs=[pl.BlockSpec(yblk, lambda g, t: (g, t, 0)), pl.BlockSpec(yblk, lambda g, t: (g, n_t - 1 - t, 0)),
                   pl.BlockSpec(st_blk, lambda g, t: (g, 0, 0, 0))],
        out_shape=[y_shape, y_shape, jax.ShapeDtypeStruct((groups,) + st_blk[1:], F32)],
        scratch_shapes=[tile, tile, pltpu.VMEM((HEAD_DIM, SUBLANES, LANES), F32), ybuf, ybuf, ybuf],
        compiler_params=_params("arbitrary", "arbitrary"),
        name=f"rwkv_scan_{st.name}",
    )(w, w, nkk, nkk, b, b, k, k, r, r, v, v, s0)
    return yf.reshape(st.n_tok, C_WIDTH), yb.reshape(st.n_tok, C_WIDTH), s_t


def _merge_ffn_kernel(x_ref, mod_ref, oa_ref, ob_ref, yf_ref, yb_ref, bonus_ref, g_ref, sg_ref, pa_ref, pb_ref, pc_ref,
                      wo_ref, ones_ref, gng_ref, gnb_ref, lng_ref, lnb_ref, w1_ref, w2_ref, lng2_ref, lnb2_ref, o_ref):
    y = yf_ref[...] + yb_ref[...]
    ones = ones_ref[...]
    mu = _head_sum(y, ones) * (1.0 / HEAD_DIM)
    d = y - mu
    var = _head_sum(d * d, ones) * (1.0 / HEAD_DIM)
    yn = d * lax.rsqrt(var + GN_EPS) * gng_ref[...] + gnb_ref[...]
    oc = ((yn + bonus_ref[...]) * g_ref[...]).astype(BF16)
    merged = (sg_ref[:, :D_MODEL] * _dot(oa_ref[...].astype(BF16), pa_ref[...])
              + sg_ref[:, D_MODEL:2 * D_MODEL] * _dot(ob_ref[...].astype(BF16), pb_ref[...])
              + sg_ref[:, 2 * D_MODEL:] * _dot(oc, pc_ref[...]))
    o = _dot(merged.astype(BF16), wo_ref[...])
    x = _layer_norm(ALPHA * x_ref[...] + mod_ref[0, 5:6, :] * o, lng_ref[...], lnb_ref[...])

    h = (x * (1.0 + mod_ref[0, 7:8, :]) + mod_ref[0, 6:7, :]).astype(BF16)
    a = _dot(h, w1_ref[:, :D_FF])
    g = _dot(h, w1_ref[:, D_FF:])
    u = (a * _sigmoid(a) * g).astype(BF16)
    f = _dot(u, w2_ref[...])
    o_ref[...] = _layer_norm(ALPHA * x + 0.5 * mod_ref[0, 8:9, :] * f, lng2_ref[...], lnb2_ref[...])


def _merge_ffn(st, layer, x, mods, oa, oa_col, ob, ob_col, yf, yb, bonus, g, sg, pa, pb, pc, wo, ones, gn_g, gn_b, ln_g, ln_b,
               w1, w2, ln_g2, ln_b2):
    vec = lambda a, n: a.reshape(1, n)
    return pl.pallas_call(
        _merge_ffn_kernel,
        grid=(st.tiles,),
        in_specs=[
            _tile(D_MODEL), _mod_spec(st, layer), _tile(A_Q, oa_col), _tile(B_W, ob_col), _tile(C_WIDTH),
            _tile(C_WIDTH), _tile(C_WIDTH), _tile(C_WIDTH), _tile(GATE_COLS),
            _resident((A_Q, D_MODEL), layer), _resident((B_W, D_MODEL), layer), _resident((C_WIDTH, D_MODEL), layer),
            _resident((D_MODEL, D_MODEL), layer), _resident((LANES, LANES)),
            _resident((1, C_WIDTH)), _resident((1, C_WIDTH)), _resident((1, D_MODEL)), _resident((1, D_MODEL)),
            _resident((D_MODEL, 2 * D_FF), layer), _resident((D_FF, D_MODEL), layer),
            _resident((1, D_MODEL)), _resident((1, D_MODEL)),
        ],
        out_specs=_tile(D_MODEL),
        out_shape=jax.ShapeDtypeStruct((st.n_tok, D_MODEL), F32),
        compiler_params=_params("parallel"),
        name=f"merge_ffn2_{st.name}",
    )(x, mods, oa, ob, yf, yb, bonus, g, sg, pa, pb, pc, wo, ones,
      vec(gn_g, C_WIDTH), vec(gn_b, C_WIDTH), vec(ln_g, D_MODEL), vec(ln_b, D_MODEL),
      w1, w2, vec(ln_g2, D_MODEL), vec(ln_b2, D_MODEL))


def _block_diag2(a, b):
    za = jnp.zeros((a.shape[0], b.shape[1]), a.dtype)
    zb = jnp.zeros((b.shape[0], a.shape[1]), b.dtype)
    return jnp.concatenate([jnp.concatenate([a, za], axis=1), jnp.concatenate([zb, b], axis=1)], axis=0)


def kernel(x_prompt, x_sample, cache_attn_k, cache_attn_v, cache_na_k, cache_na_v, state_rwkv, c, c_ctx, w_ada, b_ada, ffn1_w_in, ffn1_w_out, ffn2_w_in, ffn2_w_out, w_in, w_shift, attn_sink, na_rpb, decay_w0, decay_up, iclr_a0, iclr_up, gate_up, k_k, k_a, r_k, gn_g, gn_b, proj_a, proj_b, proj_c, w_out, ln_g, ln_b):
    xs = {CTX: x_prompt.reshape(CTX.n_tok, D_MODEL), LAT: x_sample.reshape(LAT.n_tok, D_MODEL)}
    c_rows = jnp.zeros((MOD_ROWS, D_MODEL), F32).at[:DEC_BATCH].set(c).at[DEC_BATCH].set(c_ctx)
    mods_all = _ada_mods(c_rows, w_ada, b_ada)
    cos_t, sin_t = _rope_tables()
    head_of = np.arange(LANES) // HEAD_DIM
    ones_bd = jnp.asarray(head_of[:, None] == head_of[None, :], BF16)
    ctx_groups = BATCH // SCAN_NB
    zero_state = jnp.zeros((ctx_groups, HEAD_DIM, HEAD_DIM, LANES), F32)

    bf = lambda w: w.astype(BF16)
    ffn1_w, ffn2_w = (bf(ffn1_w_in), bf(ffn1_w_out)), (bf(ffn2_w_in), bf(ffn2_w_out))
    w_in_bf, gate_up_bf = bf(w_in), bf(gate_up)
    proj_w = (bf(proj_a), bf(proj_b), bf(proj_c), bf(w_out))

    new_ka, new_va, new_kb, new_vb, new_st = [], [], [], [], []
    for l in range(DEPTH):
        mods = mods_all
        ffn1 = (*ffn1_w, ln_g[l, 0], ln_b[l, 0], 0)
        ffn2 = (*ffn2_w, ln_g[l, 2], ln_b[l, 2], 6)
        w_in_l = w_in_bf
        prep_w = (w_shift[l],
                  _block_diag2(decay_up[l, 0], decay_up[l, 1]).astype(BF16),
                  _block_diag2(iclr_up[l, 0], iclr_up[l, 1]).astype(BF16),
                  gate_up_bf,
                  decay_w0[l].reshape(1, 2 * C_WIDTH), iclr_a0[l].reshape(1, 2 * C_WIDTH),
                  k_k[l].reshape(1, C_WIDTH), k_a[l].reshape(1, C_WIDTH), r_k[l].reshape(1, C_WIDTH), ones_bd)
        merge_w = (*proj_w, ones_bd, gn_g[l], gn_b[l], ln_g[l, 1], ln_b[l, 1])
        s0_lat = state_rwkv[:, l].transpose(3, 4, 2, 1, 0).reshape(1, HEAD_DIM, HEAD_DIM, LANES)

        for st in (CTX, LAT):
            x = _ffn(st, l, xs[st], mods, *ffn1)
            za, zb, sg, r, v, nkk, w, b, k, bonus, g = _mixer_in(st, l, x, mods, w_in_l, *prep_w,
                                                                 rope=None if st is CTX else (cos_t, sin_t))
            if st is CTX:
                o_ab = _ctx_attn(za, zb, attn_sink[l])
                attn = (o_ab, 0, o_ab, 1)
                zac = za.reshape(BATCH, SEQ, A_COLS)
                zbc = zb.reshape(BATCH, SEQ, B_COLS)
                new_ka.append(zac[..., A_Q:A_Q + A_KV].reshape(BATCH, SEQ, A_KV_HEADS, HEAD_DIM))
                new_va.append(zac[..., A_Q + A_KV:].reshape(BATCH, SEQ, A_KV_HEADS, HEAD_DIM))
                new_kb.append(zbc[..., B_W:2 * B_W].reshape(BATCH, SEQ, B_HEADS, HEAD_DIM))
                new_vb.append(zbc[..., 2 * B_W:].reshape(BATCH, SEQ, B_HEADS, HEAD_DIM))
            else:
                oa = _win_attn(za, cache_attn_k[:, l].reshape(DEC_BATCH, PAST_LEN, A_KV),
                               cache_attn_v[:, l].reshape(DEC_BATCH, PAST_LEN, A_KV), attn_sink[l])
                ob = _na_attn(zb, cache_na_k[:, l].reshape(DEC_BATCH, PAST_LEN, B_W),
                              cache_na_v[:, l].reshape(DEC_BATCH, PAST_LEN, B_W), na_rpb[l].reshape(-1))
                attn = (oa, 0, ob, 0)
            yf, yb, s_t = _scan(st, w, b, k, nkk, r, v, zero_state if st is CTX else s0_lat)
            if st is CTX:
                s_t = s_t.reshape(ctx_groups, HEAD_DIM, HEAD_DIM, C_HEADS, 2, SCAN_NB)
                new_st.append(s_t.transpose(0, 5, 4, 3, 1, 2).reshape(BATCH, 2, C_HEADS, HEAD_DIM, HEAD_DIM))
            xs[st] = _merge_ffn(st, l, x, mods, *attn, yf, yb, bonus, g, sg, *merge_w, *ffn2[:4])

    y_p = xs[CTX].reshape(BATCH, SEQ, D_MODEL)
    y_s = xs[LAT].reshape(DEC_BATCH, DEC_SEQ, D_MODEL)
    return (y_p, y_s, jnp.stack(new_ka, axis=1), jnp.stack(new_va, axis=1), jnp.stack(new_kb, axis=1),
            jnp.stack(new_vb, axis=1), jnp.stack(new_st, axis=1))
```

```python
import collections
import functools

import jax
import jax.numpy as jnp
import numpy as np
from jax import lax
from jax.experimental import pallas as pl
from jax.experimental.pallas import tpu as pltpu

D_MODEL = 1024
BATCH = 16
SEQ = 256
DEPTH = 2
DEC_BATCH = 8
DEC_SEQ = 1024
PAST_LEN = 256
GRID_W = 64
GRID_ROWS = DEC_SEQ // GRID_W
HEAD_DIM = 64
SCALE = HEAD_DIM ** -0.5
A_HEADS = 8
A_KV_HEADS = 2
A_GROUP = A_HEADS // A_KV_HEADS
A_WINDOW = 128
A_BLOCK = 128
A_Q = A_HEADS * HEAD_DIM
A_KV = A_KV_HEADS * HEAD_DIM
A_COLS = A_Q + 2 * A_KV
B_HEADS = 8
B_W = B_HEADS * HEAD_DIM
B_COLS = 3 * B_W
NA_ROWS = 8
NA_COLS = 16
C_HEADS = 8
C_WIDTH = C_HEADS * HEAD_DIM
DECAY_RANK = 64
ICLR_RANK = 64
GATE_RANK = 128
C_COLS = 3 * C_WIDTH + 2 * DECAY_RANK + 2 * ICLR_RANK + GATE_RANK
GN_EPS = 64e-5
GATE_COLS = 3 * D_MODEL
IN_COLS = A_COLS + B_COLS + C_COLS + GATE_COLS
D_FF = 2816
N_MOD = 9
ROPE_BASE = 10000.0
LN_EPS = 1e-5
NEG_INF = -1e30
ALPHA = (2 * DEPTH) ** 0.25

LANES = 128
SUBLANES = 8
TM = 256
MOD_ROWS = 16
SCAN_TB = 64
SCAN_NB = LANES // (2 * C_HEADS)
VMEM_LIMIT = 56 * 1024 * 1024

F32 = jnp.float32
BF16 = jnp.bfloat16

Stream = collections.namedtuple("Stream", "name nb seq n_tok tiles tiles_per_seq mod_row")


def _make_stream(name, nb, seq, mod_row):
    return Stream(name, nb, seq, nb * seq, nb * seq // TM, seq // TM, mod_row)


CTX = _make_stream("ctx", BATCH, SEQ, lambda i: DEC_BATCH)
LAT = _make_stream("lat", DEC_BATCH, DEC_SEQ, lambda i: i // (DEC_SEQ // TM))


def _sigmoid(x):
    return 1.0 / (1.0 + jnp.exp(-x))


def _layer_norm(y, g, b):
    mu = jnp.mean(y, axis=-1, keepdims=True)
    d = y - mu
    var = jnp.mean(d * d, axis=-1, keepdims=True)
    return d * lax.rsqrt(var + LN_EPS) * g + b


def _dot(a, b):
    return jnp.dot(a, b, preferred_element_type=F32)


def _dot_nt(a, b):
    return lax.dot_general(a, b, (((1,), (1,)), ((), ())), preferred_element_type=F32)


def _head_sum(x, same_head):
    hi = x.astype(BF16)
    lo = (x - hi.astype(F32)).astype(BF16)
    slabs = [_dot(hi[:, j:j + LANES], same_head) + _dot(lo[:, j:j + LANES], same_head)
             for j in range(0, x.shape[-1], LANES)]
    return jnp.concatenate(slabs, axis=-1)


def _resident(shape, layer=None):
    zeros = (0,) * len(shape)
    if layer is None:
        return pl.BlockSpec(shape, lambda *_: zeros, pipeline_mode=pl.Buffered(1))
    return pl.BlockSpec((None,) + tuple(shape), lambda *_: (layer,) + zeros, pipeline_mode=pl.Buffered(1))


def _tile(width, col=0):
    return pl.BlockSpec((TM, width), lambda i: (i, col))


def _mod_spec(st, layer):
    return pl.BlockSpec((None, 1, N_MOD, D_MODEL), lambda i: (layer, st.mod_row(i), 0, 0))


def _params(*sem):
    return pltpu.CompilerParams(dimension_semantics=sem, vmem_limit_bytes=VMEM_LIMIT)


def _ada_kernel(c_ref, w_ref, b_ref, o_ref):
    c = c_ref[...]
    s = (c * _sigmoid(c)).astype(BF16)
    o_ref[0] = _dot(s, w_ref[0].astype(BF16)) + b_ref[0]


def _ada_mods(c_rows, w_ada, b_ada):
    out = pl.pallas_call(
        _ada_kernel,
        grid=(DEPTH, N_MOD),
        in_specs=[
            pl.BlockSpec((MOD_ROWS, D_MODEL), lambda l, j: (0, 0)),
            pl.BlockSpec((1, D_MODEL, D_MODEL), lambda l, j: (l, 0, j)),
            pl.BlockSpec((1, 1, D_MODEL), lambda l, j: (l, 0, j)),
        ],
        out_specs=pl.BlockSpec((1, MOD_ROWS, D_MODEL), lambda l, j: (l, 0, j)),
        out_shape=jax.ShapeDtypeStruct((DEPTH, MOD_ROWS, N_MOD * D_MODEL), F32),
        compiler_params=_params("parallel", "parallel"),
        name="ada_mods",
    )(c_rows, w_ada, b_ada.reshape(DEPTH, 1, N_MOD * D_MODEL))
    return out.reshape(DEPTH, MOD_ROWS, N_MOD, D_MODEL)


def _ffn_kernel(x_ref, mod_ref, w1_ref, w2_ref, g_ref, b_ref, o_ref, *, base):
    x = x_ref[...]
    shift = mod_ref[0, base:base + 1, :]
    scale = mod_ref[0, base + 1:base + 2, :]
    gate = mod_ref[0, base + 2:base + 3, :]
    h = (x * (1.0 + scale) + shift).astype(BF16)
    a = _dot(h, w1_ref[:, :D_FF])
    g = _dot(h, w1_ref[:, D_FF:])
    u = (a * _sigmoid(a) * g).astype(BF16)
    f = _dot(u, w2_ref[...])
    o_ref[...] = _layer_norm(ALPHA * x + 0.5 * gate * f, g_ref[...], b_ref[...])


def _ffn(st, layer, x, mods, w1, w2, ln_g, ln_b, base):
    return pl.pallas_call(
        functools.partial(_ffn_kernel, base=base),
        grid=(st.tiles,),
        in_specs=[
            _tile(D_MODEL), _mod_spec(st, layer),
            _resident((D_MODEL, 2 * D_FF), layer), _resident((D_FF, D_MODEL), layer),
            _resident((1, D_MODEL)), _resident((1, D_MODEL)),
        ],
        out_specs=_tile(D_MODEL),
        out_shape=jax.ShapeDtypeStruct((st.n_tok, D_MODEL), F32),
        compiler_params=_params("parallel"),
        name=f"ffn{base // 6 + 1}_{st.name}",
    )(x, mods, w1, w2, ln_g.reshape(1, D_MODEL), ln_b.reshape(1, D_MODEL))


_COL_A = 0
_COL_B = A_COLS
_COL_C = A_COLS + B_COLS
_COL_G = A_COLS + B_COLS + C_COLS


def _softmax_pv(scores, values, sink=None):
    m = scores[0].max(axis=-1, keepdims=True)
    for s in scores[1:]:
        m = jnp.maximum(m, s.max(axis=-1, keepdims=True))
    if sink is not None:
        m = jnp.maximum(m, sink)
    den = jnp.exp(sink - m) if sink is not None else 0.0
    acc = None
    for s, v in zip(scores, values):
        p = jnp.exp(s - m)
        den = den + p.sum(axis=-1, keepdims=True)
        pv = _dot(p.astype(BF16), v)
        acc = pv if acc is None else acc + pv
    return acc / den


_PW = 2 * HEAD_DIM


def _left_half(rows):
    return lax.broadcasted_iota(jnp.int32, (rows, _PW), 1) < HEAD_DIM


assert SCALE == 2.0 ** -3


def _split_heads(x, left):
    x = x * SCALE
    return jnp.concatenate([jnp.where(left, x, 0.0), jnp.where(left, 0.0, x)], axis=0).astype(BF16)


def _both_halves(slab, swapped, left, head):
    return (jnp.where(left, slab, swapped) if head == 0 else jnp.where(left, swapped, slab)).astype(BF16)


def _ctx_attn_kernel(sink_ref, za_ref, zb_ref, o_ref):
    left = _left_half(SEQ)
    slab = lambda ref, j: ref[:, j * _PW:(j + 1) * _PW]
    k_slab = slab(za_ref, A_Q // _PW)
    v_slab = slab(za_ref, (A_Q + A_KV) // _PW)
    k_swap = pltpu.roll(k_slab, HEAD_DIM, 1)
    v_swap = pltpu.roll(v_slab, HEAD_DIM, 1)
    row = lax.broadcasted_iota(jnp.int32, (A_GROUP * SEQ, 1), 0) // SEQ
    for g in range(A_KV_HEADS):
        slabs = range(g * A_GROUP // 2, (g + 1) * A_GROUP // 2)
        q4 = jnp.concatenate([_split_heads(slab(za_ref, j), left) for j in slabs], axis=0)
        sink = jnp.zeros((A_GROUP * SEQ, 1), F32)
        for j in range(A_GROUP):
            sink = jnp.where(row == j, sink_ref[A_GROUP * g + j], sink)
        o4 = _softmax_pv([_dot_nt(q4, _both_halves(k_slab, k_swap, left, g))],
                         [_both_halves(v_slab, v_swap, left, g)], sink)
        for i, j in enumerate(slabs):
            o_ref[:, j * _PW:(j + 1) * _PW] = jnp.where(left, o4[2 * i * SEQ:(2 * i + 1) * SEQ],
                                                        o4[(2 * i + 1) * SEQ:(2 * i + 2) * SEQ])
    for j in range(B_HEADS // 2):
        k2 = slab(zb_ref, B_HEADS // 2 + j).astype(BF16)
        v2 = slab(zb_ref, B_HEADS + j).astype(BF16)
        o2 = _softmax_pv([_dot_nt(_split_heads(slab(zb_ref, j), left), k2)], [v2])
        o_ref[:, A_Q + j * _PW:A_Q + (j + 1) * _PW] = jnp.where(left, o2[:SEQ], o2[SEQ:])


def _ctx_attn(za, zb, sink):
    return pl.pallas_call(
        _ctx_attn_kernel,
        grid=(BATCH,),
        in_specs=[
            pl.BlockSpec(memory_space=pltpu.SMEM),
            pl.BlockSpec((SEQ, A_COLS), lambda b: (b, 0)),
            pl.BlockSpec((SEQ, B_COLS), lambda b: (b, 0)),
        ],
        out_specs=pl.BlockSpec((SEQ, A_Q + B_W), lambda b: (b, 0)),
        out_shape=jax.ShapeDtypeStruct((CTX.n_tok, A_Q + B_W), F32),
        compiler_params=_params("parallel"),
        name="ctx_attn",
    )(sink, za, zb)


def _rope(x, cos, sin_signed):
    w = x.shape[-1]
    half = HEAD_DIM // 2
    lane = lax.broadcasted_iota(jnp.int32, x.shape, 1)
    partner = jnp.where(lane % HEAD_DIM < half, pltpu.roll(x, w - half, 1), pltpu.roll(x, half, 1))
    return x * cos + partner * sin_signed


def _win_attn_kernel(sink_ref, za_ref, ck_ref, cv_ref, band_ref, o_ref):
    n = pl.program_id(1)
    nb = DEC_SEQ // A_BLOCK
    q0 = pl.multiple_of(n * A_BLOCK, A_BLOCK)
    q = za_ref[pl.ds(q0, A_BLOCK), :A_Q]
    ks, vs = [], []
    for j in range(3):
        kb = jnp.clip(n - 1 + j, 0, nb - 1)
        k0 = pl.multiple_of(kb * A_BLOCK, A_BLOCK)
        ks.append(za_ref[pl.ds(k0, A_BLOCK), A_Q:A_Q + A_KV])
        vs.append(za_ref[pl.ds(k0, A_BLOCK), A_Q + A_KV:])
    left = _left_half(A_BLOCK)
    bands = []
    for x in (jnp.concatenate(ks, axis=0), jnp.concatenate(vs, axis=0), ck_ref[0], cv_ref[0]):
        bands.append((x, pltpu.roll(x, HEAD_DIM, 1), _left_half(x.shape[0])))
    m4 = A_GROUP * A_BLOCK
    kpos = (n - 1) * A_BLOCK + lax.broadcasted_iota(jnp.int32, (1, 3 * A_BLOCK), 1)
    band = band_ref[...] + jnp.where((kpos >= 0) & (kpos < DEC_SEQ), 0.0, NEG_INF)
    row = lax.broadcasted_iota(jnp.int32, (m4, 1), 0) // A_BLOCK
    for g in range(A_KV_HEADS):
        kband, vband, ck, cv = (_both_halves(x, swapped, lft, g) for x, swapped, lft in bands)
        slabs = range(g * A_GROUP // 2, (g + 1) * A_GROUP // 2)
        q4 = jnp.concatenate([_split_heads(q[:, j * _PW:(j + 1) * _PW], left) for j in slabs], axis=0)
        s_loc = _dot_nt(q4, kband) + band
        s_ctx = _dot_nt(q4, ck)
        sink = jnp.zeros((m4, 1), F32)
        for j in range(A_GROUP):
            sink = jnp.where(row == j, sink_ref[A_GROUP * g + j], sink)
        o4 = _softmax_pv([s_loc, s_ctx], [vband, cv], sink)
        for i, j in enumerate(slabs):
            o_ref[:, j * _PW:(j + 1) * _PW] = jnp.where(left, o4[2 * i * A_BLOCK:(2 * i + 1) * A_BLOCK],
                                                        o4[(2 * i + 1) * A_BLOCK:(2 * i + 2) * A_BLOCK])


def _rope_tables():
    t = np.arange(DEC_SEQ)
    n_freq = HEAD_DIM // 4
    inv = ROPE_BASE ** (-jnp.arange(n_freq, dtype=F32) / n_freq)
    rows = jnp.asarray(t // GRID_W, F32)
    cols = jnp.asarray(t % GRID_W, F32)
    ang = jnp.concatenate([rows[:, None] * inv, cols[:, None] * inv], axis=-1)
    cos, sin = jnp.cos(ang), jnp.sin(ang)
    cos_h = jnp.concatenate([cos, cos], axis=-1)
    sin_h = jnp.concatenate([-sin, sin], axis=-1)
    return jnp.tile(cos_h, (1, A_HEADS)), jnp.tile(sin_h, (1, A_HEADS))


def _window_band():
    qi = np.arange(A_GROUP * A_BLOCK)[:, None] % A_BLOCK
    kj = np.arange(3 * A_BLOCK)[None, :]
    return jnp.asarray(np.where(np.abs(kj - A_BLOCK - qi) <= A_WINDOW, 0.0, NEG_INF), F32)


def _win_attn(za, ck, cv, sink):
    nb = DEC_SEQ // A_BLOCK
    return pl.pallas_call(
        _win_attn_kernel,
        grid=(DEC_BATCH, nb),
        in_specs=[
            pl.BlockSpec(memory_space=pltpu.SMEM),
            pl.BlockSpec((DEC_SEQ, A_COLS), lambda b, n: (b, 0)),
            pl.BlockSpec((1, PAST_LEN, A_KV), lambda b, n: (b, 0, 0)),
            pl.BlockSpec((1, PAST_LEN, A_KV), lambda b, n: (b, 0, 0)),
            pl.BlockSpec((A_GROUP * A_BLOCK, 3 * A_BLOCK), lambda b, n: (0, 0)),
        ],
        out_specs=pl.BlockSpec((A_BLOCK, A_Q), lambda b, n: (b * nb + n, 0)),
        out_shape=jax.ShapeDtypeStruct((LAT.n_tok, A_Q), F32),
        compiler_params=_params("parallel", "arbitrary"),
        name="win_attn",
    )(sink, za, ck, cv, _window_band())


_NA_QBLK = 256
_RPB_I = 2 * NA_ROWS - 1
_RPB_J = 2 * NA_COLS - 1


def _na_build_bias(rpb_ref, head, bias_ref, slot):
    shape = (GRID_W, 2 * GRID_W)
    qc = lax.broadcasted_iota(jnp.int32, shape, 0)
    lane = lax.broadcasted_iota(jnp.int32, shape, 1)
    kc = lane % GRID_W
    left = lane < GRID_W
    cs = jnp.clip(qc - NA_COLS // 2, 0, GRID_W - NA_COLS)
    idx = jnp.where((kc >= cs) & (kc < cs + NA_COLS), jnp.clip(kc - qc + NA_COLS - 1, 0, _RPB_J - 1), -1)
    base = head * (_RPB_I * _RPB_J)
    tables = []
    for i in range(_RPB_I):
        t = jnp.full(shape, NEG_INF, F32)
        for j in range(_RPB_J):
            t = jnp.where(idx == j, rpb_ref[base + i * _RPB_J + j], t)
        tables.append(t)
    neg = jnp.full(shape, NEG_INF, F32)
    half = NA_ROWS // 2
    for qr in range(GRID_ROWS):
        rs = min(max(qr - half, 0), GRID_ROWS - NA_ROWS)
        for m in range(GRID_ROWS // 2):
            kr0, kr1 = 2 * m, 2 * m + 1
            v0 = rs <= kr0 < rs + NA_ROWS
            v1 = rs <= kr1 < rs + NA_ROWS
            i0 = kr0 - qr + NA_ROWS - 1
            if v0 and v1:
                piece = jnp.where(left, tables[i0], tables[i0 + 1])
            elif v0:
                piece = jnp.where(left, tables[i0], NEG_INF)
            elif v1:
                piece = jnp.where(left, NEG_INF, tables[i0 + 1])
            else:
                piece = neg
            bias_ref[slot, qr * GRID_W:(qr + 1) * GRID_W, m * 2 * GRID_W:(m + 1) * 2 * GRID_W] = piece


def _na_attn_kernel(rpb_ref, q_ref, k_ref, v_ref, ck_ref, cv_ref, o_ref, bias_ref):
    hp = pl.program_id(0)

    @pl.when(pl.program_id(1) == 0)
    def _():
        for half in range(2):
            _na_build_bias(rpb_ref, 2 * hp + half, bias_ref, half)

    ck2 = ck_ref[0].astype(BF16)
    cv2 = cv_ref[0].astype(BF16)
    left = _left_half(_NA_QBLK)
    rows_per_blk = _NA_QBLK // GRID_W
    for qb in range(DEC_SEQ // _NA_QBLK):
        rows = slice(qb * _NA_QBLK, (qb + 1) * _NA_QBLK)
        r_lo = min(max(qb * rows_per_blk - NA_ROWS // 2, 0), GRID_ROWS - NA_ROWS)
        r_hi = min(max((qb + 1) * rows_per_blk - 1 - NA_ROWS // 2, 0), GRID_ROWS - NA_ROWS) + NA_ROWS
        keys = slice(r_lo // 2 * 2 * GRID_W, -(-r_hi // 2) * 2 * GRID_W)
        q2 = _split_heads(q_ref[rows, :], left)
        bias = jnp.concatenate([bias_ref[0, rows, keys], bias_ref[1, rows, keys]], axis=0)
        s_loc = _dot_nt(q2, k_ref[keys, :].astype(BF16)) + bias
        s_ctx = _dot_nt(q2, ck2)
        o2 = _softmax_pv([s_loc, s_ctx], [v_ref[keys, :].astype(BF16), cv2])
        o_ref[rows, :] = jnp.where(left, o2[:_NA_QBLK], o2[_NA_QBLK:])


def _na_attn(zb, ck, cv, rpb_flat):
    pairs = B_HEADS // 2
    pw = _PW
    return pl.pallas_call(
        _na_attn_kernel,
        grid=(pairs, DEC_BATCH),
        in_specs=[
            pl.BlockSpec(memory_space=pltpu.SMEM),
            pl.BlockSpec((DEC_SEQ, pw), lambda hp, b: (b, hp)),
            pl.BlockSpec((DEC_SEQ, pw), lambda hp, b: (b, pairs + hp)),
            pl.BlockSpec((DEC_SEQ, pw), lambda hp, b: (b, 2 * pairs + hp)),
            pl.BlockSpec((1, PAST_LEN, pw), lambda hp, b: (b, 0, hp)),
            pl.BlockSpec((1, PAST_LEN, pw), lambda hp, b: (b, 0, hp)),
        ],
        out_specs=pl.BlockSpec((DEC_SEQ, pw), lambda hp, b: (b, hp)),
        out_shape=jax.ShapeDtypeStruct((LAT.n_tok, B_W), F32),
        scratch_shapes=[pltpu.VMEM((2, DEC_SEQ, DEC_SEQ), F32)],
        compiler_params=_params("arbitrary", "arbitrary"),
        name="na_attn",
    )(rpb_flat, zb, zb, zb, ck, cv)


_C_R, _C_K, _C_V = 0, C_WIDTH, 2 * C_WIDTH
_C_WLO = 3 * C_WIDTH
_C_ALO = _C_WLO + 2 * DECAY_RANK
_C_GLO = _C_ALO + 2 * ICLR_RANK
_HALO = SUBLANES


def _mixer_in_kernel(*refs, tiles_per_seq, rope):
    (x_ref, xp_ref, xn_ref, mod_ref, win_ref, wsh_ref, dup_ref, aup_ref, gup_ref, w0_ref, a0_ref,
     kk_ref, ka_ref, rk_ref, ones_ref) = refs[:15]
    rope_refs = refs[15:17] if rope else ()
    (za_ref, zb_ref, sg_ref, r_ref, v_ref, nkk_ref, w_ref, b_ref, k_ref, bonus_ref, g_ref) = refs[15 + len(rope_refs):]
    j = pl.program_id(0) % tiles_per_seq
    scale = 1.0 + mod_ref[0, 4:5, :]
    shift = mod_ref[0, 3:4, :]
    x = x_ref[...]
    h = (x * scale + shift).astype(BF16)
    za = _dot(h, win_ref[:, _COL_A:_COL_B])
    if rope:
        cos_ref, sin_ref = rope_refs
        za_ref[:, :A_Q] = _rope(za[:, :A_Q], cos_ref[...], sin_ref[...])
        za_ref[:, A_Q:A_Q + A_KV] = _rope(za[:, A_Q:A_Q + A_KV], cos_ref[:, :A_KV], sin_ref[:, :A_KV])
        za_ref[:, A_Q + A_KV:] = za[:, A_Q + A_KV:]
    else:
        za_ref[...] = za
    zb_ref[...] = _dot(h, win_ref[:, _COL_B:_COL_C])
    sg_ref[...] = _sigmoid(_dot(h, win_ref[:, _COL_G:]))

    x_ext = jnp.concatenate([xp_ref[...], x, xn_ref[...]], axis=0)
    z_ext = _dot((x_ext * scale + shift).astype(BF16), win_ref[:, _COL_C:_COL_G])
    row = lax.broadcasted_iota(jnp.int32, (TM, C_COLS), 0)
    z = z_ext[_HALO:_HALO + TM]
    z_prev = jnp.where(jnp.logical_and(row == 0, j == 0), 0.0, z_ext[_HALO - 1:_HALO - 1 + TM])
    z_next = jnp.where(jnp.logical_and(row == TM - 1, j == tiles_per_seq - 1), 0.0, z_ext[_HALO + 1:_HALO + 1 + TM])
    zc = z_prev * wsh_ref[0:1, :] + z * wsh_ref[1:2, :] + z_next * wsh_ref[2:3, :]

    r = zc[:, _C_R:_C_R + C_WIDTH]
    k = zc[:, _C_K:_C_K + C_WIDTH]
    v = zc[:, _C_V:_C_V + C_WIDTH]
    wlo = zc[:, _C_WLO:_C_WLO + 2 * DECAY_RANK]
    alo = zc[:, _C_ALO:_C_ALO + 2 * ICLR_RANK]
    glo = zc[:, _C_GLO:_C_GLO + GATE_RANK]
    ones = ones_ref[...]

    g_ref[...] = _dot(_sigmoid(glo).astype(BF16), gup_ref[...])
    kk = k * kk_ref[...]
    norm = jnp.sqrt(_head_sum(kk * kk, ones))
    kk = kk / jnp.maximum(norm, 1e-12)
    xw = w0_ref[...] + _dot(jnp.tanh(wlo).astype(BF16), dup_ref[...])
    logw = -(jnp.maximum(-xw, 0.0) + jnp.log(1.0 + jnp.exp(-jnp.abs(xw)))) - 0.5
    decay = jnp.exp(-jnp.exp(logw))
    a = _sigmoid(a0_ref[...] + _dot(alo.astype(BF16), aup_ref[...]))
    ka = ka_ref[...]
    kd_sum = None
    for d in range(2):
        a_d = a[:, d * C_WIDTH:(d + 1) * C_WIDTH]
        k_d = k * (1.0 + (a_d - 1.0) * ka)
        w_ref[d] = decay[:, d * C_WIDTH:(d + 1) * C_WIDTH]
        b_ref[d] = kk * a_d
        k_ref[d] = k_d
        kd_sum = k_d if kd_sum is None else kd_sum + k_d
    bonus_ref[...] = _head_sum(r * kd_sum * rk_ref[...], ones) * v
    r_ref[0] = r
    v_ref[0] = v
    nkk_ref[0] = -kk


def _mixer_in(st, layer, x, mods, w_in, w_shift, dec_up, iclr_up, gate_up, w0, a0, k_k, k_a, r_k, ones_bd, rope=None):
    tiles_per_halo = TM // _HALO
    n_halo = st.n_tok // _HALO
    row = lambda n: _resident((1, n))
    stacked = lambda d: (pl.BlockSpec((d, TM, C_WIDTH), lambda i: (0, i, 0)),
                         jax.ShapeDtypeStruct((d, st.n_tok, C_WIDTH), F32))
    flat = lambda w: (_tile(w), jax.ShapeDtypeStruct((st.n_tok, w), F32))
    outs = [flat(A_COLS), flat(B_COLS), flat(GATE_COLS)] + [stacked(1)] * 3 + [stacked(2)] * 3 + [flat(C_WIDTH)] * 2
    rope_specs = [pl.BlockSpec((TM, A_Q), lambda i: (i % st.tiles_per_seq, 0))] * 2 if rope else []
    return pl.pallas_call(
        functools.partial(_mixer_in_kernel, tiles_per_seq=st.tiles_per_seq, rope=bool(rope)),
        grid=(st.tiles,),
        in_specs=[
            _tile(D_MODEL),
            pl.BlockSpec((_HALO, D_MODEL), lambda i: (jnp.maximum(i * tiles_per_halo - 1, 0), 0)),
            pl.BlockSpec((_HALO, D_MODEL), lambda i: (jnp.minimum((i + 1) * tiles_per_halo, n_halo - 1), 0)),
            _mod_spec(st, layer),
            _resident((D_MODEL, IN_COLS), layer),
            _resident((3, C_COLS)),
            _resident((2 * DECAY_RANK, 2 * C_WIDTH)),
            _resident((2 * ICLR_RANK, 2 * C_WIDTH)),
            _resident((GATE_RANK, C_WIDTH), layer),
            row(2 * C_WIDTH), row(2 * C_WIDTH), row(C_WIDTH), row(C_WIDTH), row(C_WIDTH),
            _resident((LANES, LANES)),
        ] + rope_specs,
        out_specs=[o[0] for o in outs],
        out_shape=[o[1] for o in outs],
        compiler_params=_params("parallel"),
        name=f"mixer_in_{st.name}",
    )(x, x, x, mods, w_in, w_shift, dec_up, iclr_up, gate_up, w0, a0, k_k, k_a, r_k, ones_bd, *(rope or ()))


_PB = 2 * SCAN_NB


def _scan_kernel(wf, wb, af, ab, bf, bb, kf, kb, rf, rb, vf, vb, s0_ref, yf_ref, yb_ref, state,
                 ops_a, ops_b, red, wcum, y_a, y_b):
    @pl.when(pl.program_id(1) == 0)
    def _():
        state[...] = s0_ref[...]
        y_b[...] = jnp.zeros_like(y_b)

    wcum[...] = jnp.ones_like(wcum)
    left = lax.broadcasted_iota(jnp.int32, (_PB, LANES), 1) < HEAD_DIM
    low = lax.broadcasted_iota(jnp.int32, (SUBLANES, LANES), 0) < SUBLANES // 2

    def load_ops(i, ops):
        ir = SCAN_TB - 1 - i
        for p, (pf, pb, qf, qb) in enumerate(((wf, wb, af, ab), (bf, bb, kf, kb), (rf, rb, vf, vb))):
            xp = jnp.concatenate([pf[0, :, i, :], pb[0, :, ir, :]], axis=0)
            xq = jnp.concatenate([qf[0, :, i, :], qb[0, :, ir, :]], axis=0)
            blocks = []
            for j in range(C_HEADS // 2):
                pv = xp[:, j * LANES:(j + 1) * LANES]
                qv = xq[:, j * LANES:(j + 1) * LANES]
                blocks.append(jnp.where(left, pv, pltpu.roll(qv, HEAD_DIM, 1)))
                blocks.append(jnp.where(left, pltpu.roll(pv, HEAD_DIM, 1), qv))
            ops[p] = jnp.concatenate(blocks, axis=0).T

    def fold(x):
        return x.reshape(HEAD_DIM // SUBLANES, SUBLANES, LANES).sum(axis=0)

    def compute(ops, ybuf):
        w_prev = wcum[...]
        w_now = w_prev * ops[0, 0:HEAD_DIM]
        wcum[...] = w_now
        w_inv = 1.0 / w_now
        a = ops[0, HEAD_DIM:] * w_prev
        b = ops[1, 0:HEAD_DIM]
        k = ops[1, HEAD_DIM:]
        r = ops[2, 0:HEAD_DIM]
        br = jnp.sum(b * r, axis=0, keepdims=True)
        kr = jnp.sum(k * r, axis=0, keepdims=True)
        b = b * w_inv
        k = k * w_inv
        r = r * w_now
        for vi in range(HEAD_DIM):
            z = state[0, vi]
            p1 = fold(z * a)
            p2 = fold(z * r)
            t = jnp.where(low, p1, p2) + pltpu.roll(jnp.where(low, p2, p1), SUBLANES // 2, 0)
            t = t + pltpu.roll(t, SUBLANES - 2, 0)
            red[vi] = t + pltpu.roll(t, SUBLANES - 1, 0)
        for vi in range(HEAD_DIM):
            state[0, vi] = state[0, vi] + red[vi, 0:1] * b + ops[2, HEAD_DIM + vi:HEAD_DIM + vi + 1] * k
        ybuf[...] = red[:, SUBLANES // 2, :] + red[:, 0, :] * br + ops[2, HEAD_DIM:] * kr

    def emit(i, ybuf):
        y = ybuf[...]
        t = jnp.concatenate([y, y], axis=0).T
        cols = [jnp.where(left, t[(2 * j) * _PB:(2 * j + 1) * _PB], t[(2 * j + 1) * _PB:(2 * j + 2) * _PB])
                for j in range(C_HEADS // 2)]
        out = jnp.concatenate(cols, axis=1)
        yf_ref[:, i, :] = out[:SCAN_NB]
        yb_ref[:, SCAN_TB - 1 - i, :] = out[SCAN_NB:]

    load_ops(0, ops_a)

    def body(m, carry):
        i0 = 2 * m
        load_ops(i0 + 1, ops_b)
        compute(ops_a, y_a)
        emit(jnp.maximum(i0 - 1, 0), y_b)
        load_ops(jnp.minimum(i0 + 2, SCAN_TB - 1), ops_a)
        compute(ops_b, y_b)
        emit(i0, y_a)
        return carry

    lax.fori_loop(0, SCAN_TB // 2, body, 0)
    emit(SCAN_TB - 1, y_b)
    w_end = wcum[...]
    for vi in range(HEAD_DIM):
        state[0, vi] = state[0, vi] * w_end


def _scan(st, w, b, k, nkk, r, v, s0):
    groups = st.nb // SCAN_NB
    n_t = st.seq // SCAN_TB
    view = lambda x: x.reshape(x.shape[0], st.nb, st.seq, C_WIDTH)
    blk = (1, SCAN_NB, SCAN_TB, C_WIDTH)
    fwd = lambda d: pl.BlockSpec(blk, lambda g, t: (d, g, t, 0))
    bwd = lambda d: pl.BlockSpec(blk, lambda g, t: (d, g, n_t - 1 - t, 0))
    st_blk = (1, HEAD_DIM, HEAD_DIM, LANES)
    w, b, k, nkk, r, v = (view(x) for x in (w, b, k, nkk, r, v))
    tile = pltpu.VMEM((3, LANES, LANES), F32)
    ybuf = pltpu.VMEM((HEAD_DIM, LANES), F32)
    yblk = (SCAN_NB, SCAN_TB, C_WIDTH)
    y_shape = jax.ShapeDtypeStruct((st.nb, st.seq, C_WIDTH), F32)
    yf, yb, s_t = pl.pallas_call(
        _scan_kernel,
        grid=(groups, n_t),
        in_specs=[fwd(0), bwd(1), fwd(0), bwd(0), fwd(0), bwd(1), fwd(0), bwd(1), fwd(0), bwd(0), fwd(0), bwd(0),
                  pl.BlockSpec(st_blk, lambda g, t: (g, 0, 0, 0), pipeline_mode=pl.Buffered(1))],
        out_specs=[pl.BlockSpec(yblk, lambda g, t: (g, t, 0)), pl.BlockSpec(yblk, lambda g, t: (g, n_t - 1 - t, 0)),
                   pl.BlockSpec(st_blk, lambda g, t: (g, 0, 0, 0))],
        out_shape=[y_shape, y_shape, jax.ShapeDtypeStruct((groups,) + st_blk[1:], F32)],
        scratch_shapes=[tile, tile, pltpu.VMEM((HEAD_DIM, SUBLANES, LANES), F32), ybuf, ybuf, ybuf],
        compiler_params=_params("arbitrary", "arbitrary"),
        name=f"rwkv_scan_{st.name}",
    )(w, w, nkk, nkk, b, b, k, k, r, r, v, v, s0)
    return yf.reshape(st.n_tok, C_WIDTH), yb.reshape(st.n_tok, C_WIDTH), s_t


def _merge_ffn_kernel(x_ref, mod_ref, oa_ref, ob_ref, yf_ref, yb_ref, bonus_ref, g_ref, sg_ref, pa_ref, pb_ref, pc_ref,
                      wo_ref, ones_ref, gng_ref, gnb_ref, lng_ref, lnb_ref, w1_ref, w2_ref, lng2_ref, lnb2_ref, o_ref):
    y = yf_ref[...] + yb_ref[...]
    ones = ones_ref[...]
    mu = _head_sum(y, ones) * (1.0 / HEAD_DIM)
    d = y - mu
    var = _head_sum(d * d, ones) * (1.0 / HEAD_DIM)
    yn = d * lax.rsqrt(var + GN_EPS) * gng_ref[...] + gnb_ref[...]
    oc = ((yn + bonus_ref[...]) * g_ref[...]).astype(BF16)
    merged = (sg_ref[:, :D_MODEL] * _dot(oa_ref[...].astype(BF16), pa_ref[...])
              + sg_ref[:, D_MODEL:2 * D_MODEL] * _dot(ob_ref[...].astype(BF16), pb_ref[...])
              + sg_ref[:, 2 * D_MODEL:] * _dot(oc, pc_ref[...]))
    o = _dot(merged.astype(BF16), wo_ref[...])
    x = _layer_norm(ALPHA * x_ref[...] + mod_ref[0, 5:6, :] * o, lng_ref[...], lnb_ref[...])

    h = (x * (1.0 + mod_ref[0, 7:8, :]) + mod_ref[0, 6:7, :]).astype(BF16)
    a = _dot(h, w1_ref[:, :D_FF])
    g = _dot(h, w1_ref[:, D_FF:])
    u = (a * _sigmoid(a) * g).astype(BF16)
    f = _dot(u, w2_ref[...])
    o_ref[...] = _layer_norm(ALPHA * x + 0.5 * mod_ref[0, 8:9, :] * f, lng2_ref[...], lnb2_ref[...])


def _merge_ffn(st, layer, x, mods, oa, oa_col, ob, ob_col, yf, yb, bonus, g, sg, pa, pb, pc, wo, ones, gn_g, gn_b, ln_g, ln_b,
               w1, w2, ln_g2, ln_b2):
    vec = lambda a, n: a.reshape(1, n)
    return pl.pallas_call(
        _merge_ffn_kernel,
        grid=(st.tiles,),
        in_specs=[
            _tile(D_MODEL), _mod_spec(st, layer), _tile(A_Q, oa_col), _tile(B_W, ob_col), _tile(C_WIDTH),
            _tile(C_WIDTH), _tile(C_WIDTH), _tile(C_WIDTH), _tile(GATE_COLS),
            _resident((A_Q, D_MODEL), layer), _resident((B_W, D_MODEL), layer), _resident((C_WIDTH, D_MODEL), layer),
            _resident((D_MODEL, D_MODEL), layer), _resident((LANES, LANES)),
            _resident((1, C_WIDTH)), _resident((1, C_WIDTH)), _resident((1, D_MODEL)), _resident((1, D_MODEL)),
            _resident((D_MODEL, 2 * D_FF), layer), _resident((D_FF, D_MODEL), layer),
            _resident((1, D_MODEL)), _resident((1, D_MODEL)),
        ],
        out_specs=_tile(D_MODEL),
        out_shape=jax.ShapeDtypeStruct((st.n_tok, D_MODEL), F32),
        compiler_params=_params("parallel"),
        name=f"merge_ffn2_{st.name}",
    )(x, mods, oa, ob, yf, yb, bonus, g, sg, pa, pb, pc, wo, ones,
      vec(gn_g, C_WIDTH), vec(gn_b, C_WIDTH), vec(ln_g, D_MODEL), vec(ln_b, D_MODEL),
      w1, w2, vec(ln_g2, D_MODEL), vec(ln_b2, D_MODEL))


def _block_diag2(a, b):
    za = jnp.zeros((a.shape[0], b.shape[1]), a.dtype)
    zb = jnp.zeros((b.shape[0], a.shape[1]), b.dtype)
    return jnp.concatenate([jnp.concatenate([a, za], axis=1), jnp.concatenate([zb, b], axis=1)], axis=0)


def kernel(x_prompt, x_sample, cache_attn_k, cache_attn_v, cache_na_k, cache_na_v, state_rwkv, c, c_ctx, w_ada, b_ada, ffn1_w_in, ffn1_w_out, ffn2_w_in, ffn2_w_out, w_in, w_shift, attn_sink, na_rpb, decay_w0, decay_up, iclr_a0, iclr_up, gate_up, k_k, k_a, r_k, gn_g, gn_b, proj_a, proj_b, proj_c, w_out, ln_g, ln_b):
    xs = {CTX: x_prompt.reshape(CTX.n_tok, D_MODEL), LAT: x_sample.reshape(LAT.n_tok, D_MODEL)}
    c_rows = jnp.zeros((MOD_ROWS, D_MODEL), F32).at[:DEC_BATCH].set(c).at[DEC_BATCH].set(c_ctx)
    mods_all = _ada_mods(c_rows, w_ada, b_ada)
    cos_t, sin_t = _rope_tables()
    head_of = np.arange(LANES) // HEAD_DIM
    ones_bd = jnp.asarray(head_of[:, None] == head_of[None, :], BF16)
    ctx_groups = BATCH // SCAN_NB
    zero_state = jnp.zeros((ctx_groups, HEAD_DIM, HEAD_DIM, LANES), F32)

    bf = lambda w: w.astype(BF16)
    ffn1_w, ffn2_w = (bf(ffn1_w_in), bf(ffn1_w_out)), (bf(ffn2_w_in), bf(ffn2_w_out))
    w_in_bf, gate_up_bf = bf(w_in), bf(gate_up)
    proj_w = (bf(proj_a), bf(proj_b), bf(proj_c), bf(w_out))

    new_ka, new_va, new_kb, new_vb, new_st = [], [], [], [], []
    for l in range(DEPTH):
        mods = mods_all
        ffn1 = (*ffn1_w, ln_g[l, 0], ln_b[l, 0], 0)
        ffn2 = (*ffn2_w, ln_g[l, 2], ln_b[l, 2], 6)
        w_in_l = w_in_bf
        prep_w = (w_shift[l],
                  _block_diag2(decay_up[l, 0], decay_up[l, 1]).astype(BF16),
                  _block_diag2(iclr_up[l, 0], iclr_up[l, 1]).astype(BF16),
                  gate_up_bf,
                  decay_w0[l].reshape(1, 2 * C_WIDTH), iclr_a0[l].reshape(1, 2 * C_WIDTH),
                  k_k[l].reshape(1, C_WIDTH), k_a[l].reshape(1, C_WIDTH), r_k[l].reshape(1, C_WIDTH), ones_bd)
        merge_w = (*proj_w, ones_bd, gn_g[l], gn_b[l], ln_g[l, 1], ln_b[l, 1])
        s0_lat = state_rwkv[:, l].transpose(3, 4, 2, 1, 0).reshape(1, HEAD_DIM, HEAD_DIM, LANES)

        for st in (CTX, LAT):
            x = _ffn(st, l, xs[st], mods, *ffn1)
            za, zb, sg, r, v, nkk, w, b, k, bonus, g = _mixer_in(st, l, x, mods, w_in_l, *prep_w,
                                                                 rope=None if st is CTX else (cos_t, sin_t))
            if st is CTX:
                o_ab = _ctx_attn(za, zb, attn_sink[l])
                attn = (o_ab, 0, o_ab, 1)
                zac = za.reshape(BATCH, SEQ, A_COLS)
                zbc = zb.reshape(BATCH, SEQ, B_COLS)
                new_ka.append(zac[..., A_Q:A_Q + A_KV].reshape(BATCH, SEQ, A_KV_HEADS, HEAD_DIM))
                new_va.append(zac[..., A_Q + A_KV:].reshape(BATCH, SEQ, A_KV_HEADS, HEAD_DIM))
                new_kb.append(zbc[..., B_W:2 * B_W].reshape(BATCH, SEQ, B_HEADS, HEAD_DIM))
                new_vb.append(zbc[..., 2 * B_W:].reshape(BATCH, SEQ, B_HEADS, HEAD_DIM))
            else:
                oa = _win_attn(za, cache_attn_k[:, l].reshape(DEC_BATCH, PAST_LEN, A_KV),
                               cache_attn_v[:, l].reshape(DEC_BATCH, PAST_LEN, A_KV), attn_sink[l])
                ob = _na_attn(zb, cache_na_k[:, l].reshape(DEC_BATCH, PAST_LEN, B_W),
                              cache_na_v[:, l].reshape(DEC_BATCH, PAST_LEN, B_W), na_rpb[l].reshape(-1))
                attn = (oa, 0, ob, 0)
            yf, yb, s_t = _scan(st, w, b, k, nkk, r, v, zero_state if st is CTX else s0_lat)
            if st is CTX:
                s_t = s_t.reshape(ctx_groups, HEAD_DIM, HEAD_DIM, C_HEADS, 2, SCAN_NB)
                new_st.append(s_t.transpose(0, 5, 4, 3, 1, 2).reshape(BATCH, 2, C_HEADS, HEAD_DIM, HEAD_DIM))
            xs[st] = _merge_ffn(st, l, x, mods, *attn, yf, yb, bonus, g, sg, *merge_w, *ffn2[:4])

    y_p = xs[CTX].reshape(BATCH, SEQ, D_MODEL)
    y_s = xs[LAT].reshape(DEC_BATCH, DEC_SEQ, D_MODEL)
    return (y_p, y_s, jnp.stack(new_ka, axis=1), jnp.stack(new_va, axis=1), jnp.stack(new_kb, axis=1),
            jnp.stack(new_vb, axis=1), jnp.stack(new_st, axis=1))
```

```python
import collections
import functools

import jax
import jax.numpy as jnp
import numpy as np
from jax import lax
from jax.experimental import pallas as pl
from jax.experimental.pallas import tpu as pltpu

D_MODEL = 1024
BATCH = 16
SEQ = 256
DEPTH = 2
DEC_BATCH = 8
DEC_SEQ = 1024
PAST_LEN = 256
GRID_W = 64
GRID_ROWS = DEC_SEQ // GRID_W
HEAD_DIM = 64
SCALE = HEAD_DIM ** -0.5
A_HEADS = 8
A_KV_HEADS = 2
A_GROUP = A_HEADS // A_KV_HEADS
A_WINDOW = 128
A_BLOCK = 128
A_Q = A_HEADS * HEAD_DIM
A_KV = A_KV_HEADS * HEAD_DIM
A_COLS = A_Q + 2 * A_KV
B_HEADS = 8
B_W = B_HEADS * HEAD_DIM
B_COLS = 3 * B_W
NA_ROWS = 8
NA_COLS = 16
C_HEADS = 8
C_WIDTH = C_HEADS * HEAD_DIM
DECAY_RANK = 64
ICLR_RANK = 64
GATE_RANK = 128
C_COLS = 3 * C_WIDTH + 2 * DECAY_RANK + 2 * ICLR_RANK + GATE_RANK
GN_EPS = 64e-5
GATE_COLS = 3 * D_MODEL
IN_COLS = A_COLS + B_COLS + C_COLS + GATE_COLS
D_FF = 2816
N_MOD = 9
ROPE_BASE = 10000.0
LN_EPS = 1e-5
NEG_INF = -1e30
ALPHA = (2 * DEPTH) ** 0.25

LANES = 128
SUBLANES = 8
TM = 256
MOD_ROWS = 16
SCAN_TB = 64
SCAN_NB = LANES // (2 * C_HEADS)
VMEM_LIMIT = 56 * 1024 * 1024

F32 = jnp.float32
BF16 = jnp.bfloat16

Stream = collections.namedtuple("Stream", "name nb seq n_tok tiles tiles_per_seq mod_row")


def _make_stream(name, nb, seq, mod_row):
    return Stream(name, nb, seq, nb * seq, nb * seq // TM, seq // TM, mod_row)


CTX = _make_stream("ctx", BATCH, SEQ, lambda i: DEC_BATCH)
LAT = _make_stream("lat", DEC_BATCH, DEC_SEQ, lambda i: i // (DEC_SEQ // TM))


def _sigmoid(x):
    return 1.0 / (1.0 + jnp.exp(-x))


def _layer_norm(y, g, b):
    mu = jnp.mean(y, axis=-1, keepdims=True)
    d = y - mu
    var = jnp.mean(d * d, axis=-1, keepdims=True)
    return d * lax.rsqrt(var + LN_EPS) * g + b


def _dot(a, b):
    return jnp.dot(a, b, preferred_element_type=F32)


def _dot_nt(a, b):
    return lax.dot_general(a, b, (((1,), (1,)), ((), ())), preferred_element_type=F32)


def _head_sum(x, same_head):
    hi = x.astype(BF16)
    lo = (x - hi.astype(F32)).astype(BF16)
    slabs = [_dot(hi[:, j:j + LANES], same_head) + _dot(lo[:, j:j + LANES], same_head)
             for j in range(0, x.shape[-1], LANES)]
    return jnp.concatenate(slabs, axis=-1)


def _resident(shape, layer=None):
    zeros = (0,) * len(shape)
    if layer is None:
        return pl.BlockSpec(shape, lambda *_: zeros, pipeline_mode=pl.Buffered(1))
    return pl.BlockSpec((None,) + tuple(shape), lambda *_: (layer,) + zeros, pipeline_mode=pl.Buffered(1))


def _tile(width, col=0):
    return pl.BlockSpec((TM, width), lambda i: (i, col))


def _mod_spec(st, layer):
    return pl.BlockSpec((None, 1, N_MOD, D_MODEL), lambda i: (layer, st.mod_row(i), 0, 0))


def _params(*sem):
    return pltpu.CompilerParams(dimension_semantics=sem, vmem_limit_bytes=VMEM_LIMIT)


def _ada_kernel(c_ref, w_ref, b_ref, o_ref):
    c = c_ref[...]
    s = (c * _sigmoid(c)).astype(BF16)
    o_ref[0] = _dot(s, w_ref[0].astype(BF16)) + b_ref[0]


def _ada_mods(c_rows, w_ada, b_ada):
    out = pl.pallas_call(
        _ada_kernel,
        grid=(DEPTH, N_MOD),
        in_specs=[
            pl.BlockSpec((MOD_ROWS, D_MODEL), lambda l, j: (0, 0)),
            pl.BlockSpec((1, D_MODEL, D_MODEL), lambda l, j: (l, 0, j)),
            pl.BlockSpec((1, 1, D_MODEL), lambda l, j: (l, 0, j)),
        ],
        out_specs=pl.BlockSpec((1, MOD_ROWS, D_MODEL), lambda l, j: (l, 0, j)),
        out_shape=jax.ShapeDtypeStruct((DEPTH, MOD_ROWS, N_MOD * D_MODEL), F32),
        compiler_params=_params("parallel", "parallel"),
        name="ada_mods",
    )(c_rows, w_ada, b_ada.reshape(DEPTH, 1, N_MOD * D_MODEL))
    return out.reshape(DEPTH, MOD_ROWS, N_MOD, D_MODEL)


def _ffn_kernel(x_ref, mod_ref, w1_ref, w2_ref, g_ref, b_ref, o_ref, *, base):
    x = x_ref[...]
    shift = mod_ref[0, base:base + 1, :]
    scale = mod_ref[0, base + 1:base + 2, :]
    gate = mod_ref[0, base + 2:base + 3, :]
    h = (x * (1.0 + scale) + shift).astype(BF16)
    a = _dot(h, w1_ref[:, :D_FF])
    g = _dot(h, w1_ref[:, D_FF:])
    u = (a * _sigmoid(a) * g).astype(BF16)
    f = _dot(u, w2_ref[...])
    o_ref[...] = _layer_norm(ALPHA * x + 0.5 * gate * f, g_ref[...], b_ref[...])


def _ffn(st, layer, x, mods, w1, w2, ln_g, ln_b, base):
    return pl.pallas_call(
        functools.partial(_ffn_kernel, base=base),
        grid=(st.tiles,),
        in_specs=[
            _tile(D_MODEL), _mod_spec(st, layer),
            _resident((D_MODEL, 2 * D_FF), layer), _resident((D_FF, D_MODEL), layer),
            _resident((1, D_MODEL)), _resident((1, D_MODEL)),
        ],
        out_specs=_tile(D_MODEL),
        out_shape=jax.ShapeDtypeStruct((st.n_tok, D_MODEL), F32),
        compiler_params=_params("parallel"),
        name=f"ffn{base // 6 + 1}_{st.name}",
    )(x, mods, w1, w2, ln_g.reshape(1, D_MODEL), ln_b.reshape(1, D_MODEL))


_COL_A = 0
_COL_B = A_COLS
_COL_C = A_COLS + B_COLS
_COL_G = A_COLS + B_COLS + C_COLS


def _softmax_pv(scores, values, sink=None):
    m = scores[0].max(axis=-1, keepdims=True)
    for s in scores[1:]:
        m = jnp.maximum(m, s.max(axis=-1, keepdims=True))
    if sink is not None:
        m = jnp.maximum(m, sink)
    den = jnp.exp(sink - m) if sink is not None else 0.0
    acc = None
    for s, v in zip(scores, values):
        p = jnp.exp(s - m)
        den = den + p.sum(axis=-1, keepdims=True)
        pv = _dot(p.astype(BF16), v)
        acc = pv if acc is None else acc + pv
    return acc / den


_PW = 2 * HEAD_DIM


def _left_half(rows):
    return lax.broadcasted_iota(jnp.int32, (rows, _PW), 1) < HEAD_DIM


assert SCALE == 2.0 ** -3


def _split_heads(x, left):
    x = x * SCALE
    return jnp.concatenate([jnp.where(left, x, 0.0), jnp.where(left, 0.0, x)], axis=0).astype(BF16)


def _both_halves(slab, swapped, left, head):
    return (jnp.where(left, slab, swapped) if head == 0 else jnp.where(left, swapped, slab)).astype(BF16)


def _ctx_attn_kernel(sink_ref, za_ref, zb_ref, o_ref):
    left = _left_half(SEQ)
    slab = lambda ref, j: ref[:, j * _PW:(j + 1) * _PW]
    k_slab = slab(za_ref, A_Q // _PW)
    v_slab = slab(za_ref, (A_Q + A_KV) // _PW)
    k_swap = pltpu.roll(k_slab, HEAD_DIM, 1)
    v_swap = pltpu.roll(v_slab, HEAD_DIM, 1)
    row = lax.broadcasted_iota(jnp.int32, (A_GROUP * SEQ, 1), 0) // SEQ
    for g in range(A_KV_HEADS):
        slabs = range(g * A_GROUP // 2, (g + 1) * A_GROUP // 2)
        q4 = jnp.concatenate([_split_heads(slab(za_ref, j), left) for j in slabs], axis=0)
        sink = jnp.zeros((A_GROUP * SEQ, 1), F32)
        for j in range(A_GROUP):
            sink = jnp.where(row == j, sink_ref[A_GROUP * g + j], sink)
        o4 = _softmax_pv([_dot_nt(q4, _both_halves(k_slab, k_swap, left, g))],
                         [_both_halves(v_slab, v_swap, left, g)], sink)
        for i, j in enumerate(slabs):
            o_ref[:, j * _PW:(j + 1) * _PW] = jnp.where(left, o4[2 * i * SEQ:(2 * i + 1) * SEQ],
                                                        o4[(2 * i + 1) * SEQ:(2 * i + 2) * SEQ])
    for j in range(B_HEADS // 2):
        k2 = slab(zb_ref, B_HEADS // 2 + j).astype(BF16)
        v2 = slab(zb_ref, B_HEADS + j).astype(BF16)
        o2 = _softmax_pv([_dot_nt(_split_heads(slab(zb_ref, j), left), k2)], [v2])
        o_ref[:, A_Q + j * _PW:A_Q + (j + 1) * _PW] = jnp.where(left, o2[:SEQ], o2[SEQ:])


def _rope(x, cos, sin_signed):
    w = x.shape[-1]
    half = HEAD_DIM // 2
    lane = lax.broadcasted_iota(jnp.int32, x.shape, 1)
    partner = jnp.where(lane % HEAD_DIM < half, pltpu.roll(x, w - half, 1), pltpu.roll(x, half, 1))
    return x * cos + partner * sin_signed


def _win_attn_kernel(sink_ref, za_ref, ck_ref, cv_ref, band_ref, o_ref):
    n = pl.program_id(1)
    nb = DEC_SEQ // A_BLOCK
    q0 = pl.multiple_of(n * A_BLOCK, A_BLOCK)
    q = za_ref[pl.ds(q0, A_BLOCK), :A_Q]
    ks, vs = [], []
    for j in range(3):
        kb = jnp.clip(n - 1 + j, 0, nb - 1)
        k0 = pl.multiple_of(kb * A_BLOCK, A_BLOCK)
        ks.append(za_ref[pl.ds(k0, A_BLOCK), A_Q:A_Q + A_KV])
        vs.append(za_ref[pl.ds(k0, A_BLOCK), A_Q + A_KV:])
    left = _left_half(A_BLOCK)
    bands = []
    for x in (jnp.concatenate(ks, axis=0), jnp.concatenate(vs, axis=0), ck_ref[0], cv_ref[0]):
        bands.append((x, pltpu.roll(x, HEAD_DIM, 1), _left_half(x.shape[0])))
    m4 = A_GROUP * A_BLOCK
    kpos = (n - 1) * A_BLOCK + lax.broadcasted_iota(jnp.int32, (1, 3 * A_BLOCK), 1)
    band = band_ref[...] + jnp.where((kpos >= 0) & (kpos < DEC_SEQ), 0.0, NEG_INF)
    row = lax.broadcasted_iota(jnp.int32, (m4, 1), 0) // A_BLOCK
    for g in range(A_KV_HEADS):
        kband, vband, ck, cv = (_both_halves(x, swapped, lft, g) for x, swapped, lft in bands)
        slabs = range(g * A_GROUP // 2, (g + 1) * A_GROUP // 2)
        q4 = jnp.concatenate([_split_heads(q[:, j * _PW:(j + 1) * _PW], left) for j in slabs], axis=0)
        s_loc = _dot_nt(q4, kband) + band
        s_ctx = _dot_nt(q4, ck)
        sink = jnp.zeros((m4, 1), F32)
        for j in range(A_GROUP):
            sink = jnp.where(row == j, sink_ref[A_GROUP * g + j], sink)
        o4 = _softmax_pv([s_loc, s_ctx], [vband, cv], sink)
        for i, j in enumerate(slabs):
            o_ref[:, j * _PW:(j + 1) * _PW] = jnp.where(left, o4[2 * i * A_BLOCK:(2 * i + 1) * A_BLOCK],
                                                        o4[(2 * i + 1) * A_BLOCK:(2 * i + 2) * A_BLOCK])


def _rope_tables():
    t = np.arange(DEC_SEQ)
    n_freq = HEAD_DIM // 4
    inv = ROPE_BASE ** (-jnp.arange(n_freq, dtype=F32) / n_freq)
    rows = jnp.asarray(t // GRID_W, F32)
    cols = jnp.asarray(t % GRID_W, F32)
    ang = jnp.concatenate([rows[:, None] * inv, cols[:, None] * inv], axis=-1)
    cos, sin = jnp.cos(ang), jnp.sin(ang)
    cos_h = jnp.concatenate([cos, cos], axis=-1)
    sin_h = jnp.concatenate([-sin, sin], axis=-1)
    return jnp.tile(cos_h, (1, A_HEADS)), jnp.tile(sin_h, (1, A_HEADS))


def _window_band():
    qi = np.arange(A_GROUP * A_BLOCK)[:, None] % A_BLOCK
    kj = np.arange(3 * A_BLOCK)[None, :]
    return jnp.asarray(np.where(np.abs(kj - A_BLOCK - qi) <= A_WINDOW, 0.0, NEG_INF), F32)


def _win_attn(za, ck, cv, sink):
    nb = DEC_SEQ // A_BLOCK
    return pl.pallas_call(
        _win_attn_kernel,
        grid=(DEC_BATCH, nb),
        in_specs=[
            pl.BlockSpec(memory_space=pltpu.SMEM),
            pl.BlockSpec((DEC_SEQ, A_COLS), lambda b, n: (b, 0)),
            pl.BlockSpec((1, PAST_LEN, A_KV), lambda b, n: (b, 0, 0)),
            pl.BlockSpec((1, PAST_LEN, A_KV), lambda b, n: (b, 0, 0)),
            pl.BlockSpec((A_GROUP * A_BLOCK, 3 * A_BLOCK), lambda b, n: (0, 0)),
        ],
        out_specs=pl.BlockSpec((A_BLOCK, A_Q), lambda b, n: (b * nb + n, 0)),
        out_shape=jax.ShapeDtypeStruct((LAT.n_tok, A_Q), F32),
        compiler_params=_params("parallel", "arbitrary"),
        name="win_attn",
    )(sink, za, ck, cv, _window_band())


_NA_QBLK = 256
_RPB_I = 2 * NA_ROWS - 1
_RPB_J = 2 * NA_COLS - 1


def _na_build_bias(rpb_ref, head, bias_ref, slot):
    shape = (GRID_W, 2 * GRID_W)
    qc = lax.broadcasted_iota(jnp.int32, shape, 0)
    lane = lax.broadcasted_iota(jnp.int32, shape, 1)
    kc = lane % GRID_W
    left = lane < GRID_W
    cs = jnp.clip(qc - NA_COLS // 2, 0, GRID_W - NA_COLS)
    idx = jnp.where((kc >= cs) & (kc < cs + NA_COLS), jnp.clip(kc - qc + NA_COLS - 1, 0, _RPB_J - 1), -1)
    base = head * (_RPB_I * _RPB_J)
    tables = []
    for i in range(_RPB_I):
        t = jnp.full(shape, NEG_INF, F32)
        for j in range(_RPB_J):
            t = jnp.where(idx == j, rpb_ref[base + i * _RPB_J + j], t)
        tables.append(t)
    neg = jnp.full(shape, NEG_INF, F32)
    half = NA_ROWS // 2
    for qr in range(GRID_ROWS):
        rs = min(max(qr - half, 0), GRID_ROWS - NA_ROWS)
        for m in range(GRID_ROWS // 2):
            kr0, kr1 = 2 * m, 2 * m + 1
            v0 = rs <= kr0 < rs + NA_ROWS
            v1 = rs <= kr1 < rs + NA_ROWS
            i0 = kr0 - qr + NA_ROWS - 1
            if v0 and v1:
                piece = jnp.where(left, tables[i0], tables[i0 + 1])
            elif v0:
                piece = jnp.where(left, tables[i0], NEG_INF)
            elif v1:
                piece = jnp.where(left, NEG_INF, tables[i0 + 1])
            else:
                piece = neg
            bias_ref[slot, qr * GRID_W:(qr + 1) * GRID_W, m * 2 * GRID_W:(m + 1) * 2 * GRID_W] = piece


def _na_attn_kernel(rpb_ref, q_ref, k_ref, v_ref, ck_ref, cv_ref, o_ref, bias_ref):
    hp = pl.program_id(0)

    @pl.when(pl.program_id(1) == 0)
    def _():
        for half in range(2):
            _na_build_bias(rpb_ref, 2 * hp + half, bias_ref, half)

    ck2 = ck_ref[0].astype(BF16)
    cv2 = cv_ref[0].astype(BF16)
    left = _left_half(_NA_QBLK)
    rows_per_blk = _NA_QBLK // GRID_W
    for qb in range(DEC_SEQ // _NA_QBLK):
        rows = slice(qb * _NA_QBLK, (qb + 1) * _NA_QBLK)
        r_lo = min(max(qb * rows_per_blk - NA_ROWS // 2, 0), GRID_ROWS - NA_ROWS)
        r_hi = min(max((qb + 1) * rows_per_blk - 1 - NA_ROWS // 2, 0), GRID_ROWS - NA_ROWS) + NA_ROWS
        keys = slice(r_lo // 2 * 2 * GRID_W, -(-r_hi // 2) * 2 * GRID_W)
        q2 = _split_heads(q_ref[rows, :], left)
        bias = jnp.concatenate([bias_ref[0, rows, keys], bias_ref[1, rows, keys]], axis=0)
        s_loc = _dot_nt(q2, k_ref[keys, :].astype(BF16)) + bias
        s_ctx = _dot_nt(q2, ck2)
        o2 = _softmax_pv([s_loc, s_ctx], [v_ref[keys, :].astype(BF16), cv2])
        o_ref[rows, :] = jnp.where(left, o2[:_NA_QBLK], o2[_NA_QBLK:])


def _na_attn(zb, ck, cv, rpb_flat):
    pairs = B_HEADS // 2
    pw = _PW
    return pl.pallas_call(
        _na_attn_kernel,
        grid=(pairs, DEC_BATCH),
        in_specs=[
            pl.BlockSpec(memory_space=pltpu.SMEM),
            pl.BlockSpec((DEC_SEQ, pw), lambda hp, b: (b, hp)),
            pl.BlockSpec((DEC_SEQ, pw), lambda hp, b: (b, pairs + hp)),
            pl.BlockSpec((DEC_SEQ, pw), lambda hp, b: (b, 2 * pairs + hp)),
            pl.BlockSpec((1, PAST_LEN, pw), lambda hp, b: (b, 0, hp)),
            pl.BlockSpec((1, PAST_LEN, pw), lambda hp, b: (b, 0, hp)),
        ],
        out_specs=pl.BlockSpec((DEC_SEQ, pw), lambda hp, b: (b, hp)),
        out_shape=jax.ShapeDtypeStruct((LAT.n_tok, B_W), F32),
        scratch_shapes=[pltpu.VMEM((2, DEC_SEQ, DEC_SEQ), F32)],
        compiler_params=_params("arbitrary", "arbitrary"),
        name="na_attn",
    )(rpb_flat, zb, zb, zb, ck, cv)


_C_R, _C_K, _C_V = 0, C_WIDTH, 2 * C_WIDTH
_C_WLO = 3 * C_WIDTH
_C_ALO = _C_WLO + 2 * DECAY_RANK
_C_GLO = _C_ALO + 2 * ICLR_RANK
_HALO = SUBLANES


def _mixer_in_kernel(*refs, tiles_per_seq, rope, attend):
    (x_ref, xp_ref, xn_ref, mod_ref, win_ref, wsh_ref, dup_ref, aup_ref, gup_ref, w0_ref, a0_ref,
     kk_ref, ka_ref, rk_ref, ones_ref) = refs[:15]
    extra = refs[15:15 + 2 * rope + attend]
    outs = refs[15 + len(extra):]
    (za_ref, zb_ref, sg_ref, r_ref, v_ref, nkk_ref, w_ref, b_ref, k_ref, bonus_ref, g_ref) = outs[:11]
    rope_refs = extra[:2] if rope else ()
    j = pl.program_id(0) % tiles_per_seq
    scale = 1.0 + mod_ref[0, 4:5, :]
    shift = mod_ref[0, 3:4, :]
    x = x_ref[...]
    h = (x * scale + shift).astype(BF16)
    za = _dot(h, win_ref[:, _COL_A:_COL_B])
    if rope:
        cos_ref, sin_ref = rope_refs
        za_ref[:, :A_Q] = _rope(za[:, :A_Q], cos_ref[...], sin_ref[...])
        za_ref[:, A_Q:A_Q + A_KV] = _rope(za[:, A_Q:A_Q + A_KV], cos_ref[:, :A_KV], sin_ref[:, :A_KV])
        za_ref[:, A_Q + A_KV:] = za[:, A_Q + A_KV:]
    else:
        za_ref[...] = za
    zb_ref[...] = _dot(h, win_ref[:, _COL_B:_COL_C])
    if attend:
        _ctx_attn_kernel(extra[-1], za_ref, zb_ref, outs[11])
    sg_ref[...] = _sigmoid(_dot(h, win_ref[:, _COL_G:]))

    x_ext = jnp.concatenate([xp_ref[...], x, xn_ref[...]], axis=0)
    z_ext = _dot((x_ext * scale + shift).astype(BF16), win_ref[:, _COL_C:_COL_G])
    row = lax.broadcasted_iota(jnp.int32, (TM, C_COLS), 0)
    z = z_ext[_HALO:_HALO + TM]
    z_prev = jnp.where(jnp.logical_and(row == 0, j == 0), 0.0, z_ext[_HALO - 1:_HALO - 1 + TM])
    z_next = jnp.where(jnp.logical_and(row == TM - 1, j == tiles_per_seq - 1), 0.0, z_ext[_HALO + 1:_HALO + 1 + TM])
    zc = z_prev * wsh_ref[0:1, :] + z * wsh_ref[1:2, :] + z_next * wsh_ref[2:3, :]

    r = zc[:, _C_R:_C_R + C_WIDTH]
    k = zc[:, _C_K:_C_K + C_WIDTH]
    v = zc[:, _C_V:_C_V + C_WIDTH]
    wlo = zc[:, _C_WLO:_C_WLO + 2 * DECAY_RANK]
    alo = zc[:, _C_ALO:_C_ALO + 2 * ICLR_RANK]
    glo = zc[:, _C_GLO:_C_GLO + GATE_RANK]
    ones = ones_ref[...]

    g_ref[...] = _dot(_sigmoid(glo).astype(BF16), gup_ref[...])
    kk = k * kk_ref[...]
    norm = jnp.sqrt(_head_sum(kk * kk, ones))
    kk = kk / jnp.maximum(norm, 1e-12)
    xw = w0_ref[...] + _dot(jnp.tanh(wlo).astype(BF16), dup_ref[...])
    logw = -(jnp.maximum(-xw, 0.0) + jnp.log(1.0 + jnp.exp(-jnp.abs(xw)))) - 0.5
    decay = jnp.exp(-jnp.exp(logw))
    a = _sigmoid(a0_ref[...] + _dot(alo.astype(BF16), aup_ref[...]))
    ka = ka_ref[...]
    kd_sum = None
    for d in range(2):
        a_d = a[:, d * C_WIDTH:(d + 1) * C_WIDTH]
        k_d = k * (1.0 + (a_d - 1.0) * ka)
        w_ref[d] = decay[:, d * C_WIDTH:(d + 1) * C_WIDTH]
        b_ref[d] = kk * a_d
        k_ref[d] = k_d
        kd_sum = k_d if kd_sum is None else kd_sum + k_d
    bonus_ref[...] = _head_sum(r * kd_sum * rk_ref[...], ones) * v
    r_ref[0] = r
    v_ref[0] = v
    nkk_ref[0] = -kk


def _mixer_in(st, layer, x, mods, w_in, w_shift, dec_up, iclr_up, gate_up, w0, a0, k_k, k_a, r_k, ones_bd, rope=None,
              sink=None):
    tiles_per_halo = TM // _HALO
    n_halo = st.n_tok // _HALO
    row = lambda n: _resident((1, n))
    stacked = lambda d: (pl.BlockSpec((d, TM, C_WIDTH), lambda i: (0, i, 0)),
                         jax.ShapeDtypeStruct((d, st.n_tok, C_WIDTH), F32))
    flat = lambda w: (_tile(w), jax.ShapeDtypeStruct((st.n_tok, w), F32))
    outs = [flat(A_COLS), flat(B_COLS), flat(GATE_COLS)] + [stacked(1)] * 3 + [stacked(2)] * 3 + [flat(C_WIDTH)] * 2
    rope_specs = [pl.BlockSpec((TM, A_Q), lambda i: (i % st.tiles_per_seq, 0))] * 2 if rope else []
    attend = sink is not None
    if attend:
        assert st.seq == TM == SEQ
        outs.append(flat(A_Q + B_W))
    sink_specs = [pl.BlockSpec(memory_space=pltpu.SMEM)] if attend else []
    return pl.pallas_call(
        functools.partial(_mixer_in_kernel, tiles_per_seq=st.tiles_per_seq, rope=bool(rope), attend=attend),
        grid=(st.tiles,),
        in_specs=[
            _tile(D_MODEL),
            pl.BlockSpec((_HALO, D_MODEL), lambda i: (jnp.maximum(i * tiles_per_halo - 1, 0), 0)),
            pl.BlockSpec((_HALO, D_MODEL), lambda i: (jnp.minimum((i + 1) * tiles_per_halo, n_halo - 1), 0)),
            _mod_spec(st, layer),
            _resident((D_MODEL, IN_COLS), layer),
            _resident((3, C_COLS)),
            _resident((2 * DECAY_RANK, 2 * C_WIDTH)),
            _resident((2 * ICLR_RANK, 2 * C_WIDTH)),
            _resident((GATE_RANK, C_WIDTH), layer),
            row(2 * C_WIDTH), row(2 * C_WIDTH), row(C_WIDTH), row(C_WIDTH), row(C_WIDTH),
            _resident((LANES, LANES)),
        ] + rope_specs + sink_specs,
        out_specs=[o[0] for o in outs],
        out_shape=[o[1] for o in outs],
        compiler_params=_params("parallel"),
        name=f"mixer_in_{st.name}",
    )(x, x, x, mods, w_in, w_shift, dec_up, iclr_up, gate_up, w0, a0, k_k, k_a, r_k, ones_bd, *(rope or ()), *((sink,) if attend else ()))


_PB = 2 * SCAN_NB


def _scan_kernel(wf, wb, af, ab, bf, bb, kf, kb, rf, rb, vf, vb, s0_ref, yf_ref, yb_ref, state,
                 ops_a, ops_b, red, wcum, y_a, y_b):
    @pl.when(pl.program_id(1) == 0)
    def _():
        state[...] = s0_ref[...]
        y_b[...] = jnp.zeros_like(y_b)

    wcum[...] = jnp.ones_like(wcum)
    left = lax.broadcasted_iota(jnp.int32, (_PB, LANES), 1) < HEAD_DIM
    low = lax.broadcasted_iota(jnp.int32, (SUBLANES, LANES), 0) < SUBLANES // 2

    def load_ops(i, ops):
        ir = SCAN_TB - 1 - i
        for p, (pf, pb, qf, qb) in enumerate(((wf, wb, af, ab), (bf, bb, kf, kb), (rf, rb, vf, vb))):
            xp = jnp.concatenate([pf[0, :, i, :], pb[0, :, ir, :]], axis=0)
            xq = jnp.concatenate([qf[0, :, i, :], qb[0, :, ir, :]], axis=0)
            blocks = []
            for j in range(C_HEADS // 2):
                pv = xp[:, j * LANES:(j + 1) * LANES]
                qv = xq[:, j * LANES:(j + 1) * LANES]
                blocks.append(jnp.where(left, pv, pltpu.roll(qv, HEAD_DIM, 1)))
                blocks.append(jnp.where(left, pltpu.roll(pv, HEAD_DIM, 1), qv))
            ops[p] = jnp.concatenate(blocks, axis=0).T

    def fold(x):
        return x.reshape(HEAD_DIM // SUBLANES, SUBLANES, LANES).sum(axis=0)

    def compute(ops, ybuf):
        w_prev = wcum[...]
        w_now = w_prev * ops[0, 0:HEAD_DIM]
        wcum[...] = w_now
        w_inv = 1.0 / w_now
        a = ops[0, HEAD_DIM:] * w_prev
        b = ops[1, 0:HEAD_DIM]
        k = ops[1, HEAD_DIM:]
        r = ops[2, 0:HEAD_DIM]
        br = jnp.sum(b * r, axis=0, keepdims=True)
        kr = jnp.sum(k * r, axis=0, keepdims=True)
        b = b * w_inv
        k = k * w_inv
        r = r * w_now
        for vi in range(HEAD_DIM):
            z = state[0, vi]
            p1 = fold(z * a)
            p2 = fold(z * r)
            t = jnp.where(low, p1, p2) + pltpu.roll(jnp.where(low, p2, p1), SUBLANES // 2, 0)
            t = t + pltpu.roll(t, SUBLANES - 2, 0)
            red[vi] = t + pltpu.roll(t, SUBLANES - 1, 0)
        for vi in range(HEAD_DIM):
            state[0, vi] = state[0, vi] + red[vi, 0:1] * b + ops[2, HEAD_DIM + vi:HEAD_DIM + vi + 1] * k
        ybuf[...] = red[:, SUBLANES // 2, :] + red[:, 0, :] * br + ops[2, HEAD_DIM:] * kr

    def emit(i, ybuf):
        y = ybuf[...]
        t = jnp.concatenate([y, y], axis=0).T
        cols = [jnp.where(left, t[(2 * j) * _PB:(2 * j + 1) * _PB], t[(2 * j + 1) * _PB:(2 * j + 2) * _PB])
                for j in range(C_HEADS // 2)]
        out = jnp.concatenate(cols, axis=1)
        yf_ref[:, i, :] = out[:SCAN_NB]
        yb_ref[:, SCAN_TB - 1 - i, :] = out[SCAN_NB:]

    load_ops(0, ops_a)

    def body(m, carry):
        i0 = 2 * m
        load_ops(i0 + 1, ops_b)
        compute(ops_a, y_a)
        emit(jnp.maximum(i0 - 1, 0), y_b)
        load_ops(jnp.minimum(i0 + 2, SCAN_TB - 1), ops_a)
        compute(ops_b, y_b)
        emit(i0, y_a)
        return carry

    lax.fori_loop(0, SCAN_TB // 2, body, 0)
    emit(SCAN_TB - 1, y_b)
    w_end = wcum[...]
    for vi in range(HEAD_DIM):
        state[0, vi] = state[0, vi] * w_end


def _scan(st, w, b, k, nkk, r, v, s0):
    groups = st.nb // SCAN_NB
    n_t = st.seq // SCAN_TB
    view = lambda x: x.reshape(x.shape[0], st.nb, st.seq, C_WIDTH)
    blk = (1, SCAN_NB, SCAN_TB, C_WIDTH)
    fwd = lambda d: pl.BlockSpec(blk, lambda g, t: (d, g, t, 0))
    bwd = lambda d: pl.BlockSpec(blk, lambda g, t: (d, g, n_t - 1 - t, 0))
    st_blk = (1, HEAD_DIM, HEAD_DIM, LANES)
    w, b, k, nkk, r, v = (view(x) for x in (w, b, k, nkk, r, v))
    tile = pltpu.VMEM((3, LANES, LANES), F32)
    ybuf = pltpu.VMEM((HEAD_DIM, LANES), F32)
    yblk = (SCAN_NB, SCAN_TB, C_WIDTH)
    y_shape = jax.ShapeDtypeStruct((st.nb, st.seq, C_WIDTH), F32)
    yf, yb, s_t = pl.pallas_call(
        _scan_kernel,
        grid=(groups, n_t),
        in_specs=[fwd(0), bwd(1), fwd(0), bwd(0), fwd(0), bwd(1), fwd(0), bwd(1), fwd(0), bwd(0), fwd(0), bwd(0),
                  pl.BlockSpec(st_blk, lambda g, t: (g, 0, 0, 0), pipeline_mode=pl.Buffered(1))],
        out_specs=[pl.BlockSpec(yblk, lambda g, t: (g, t, 0)), pl.BlockSpec(yblk, lambda g, t: (g, n_t - 1 - t, 0)),
                   pl.BlockSpec(st_blk, lambda g, t: (g, 0, 0, 0))],
        out_shape=[y_shape, y_shape, jax.ShapeDtypeStruct((groups,) + st_blk[1:], F32)],
        scratch_shapes=[tile, tile, pltpu.VMEM((HEAD_DIM, SUBLANES, LANES), F32), ybuf, ybuf, ybuf],
        compiler_params=_params("arbitrary", "arbitrary"),
        name=f"rwkv_scan_{st.name}",
    )(w, w, nkk, nkk, b, b, k, k, r, r, v, v, s0)
    return yf.reshape(st.n_tok, C_WIDTH), yb.reshape(st.n_tok, C_WIDTH), s_t


def _merge_ffn_kernel(x_ref, mod_ref, oa_ref, ob_ref, yf_ref, yb_ref, bonus_ref, g_ref, sg_ref, pa_ref, pb_ref, pc_ref,
                      wo_ref, ones_ref, gng_ref, gnb_ref, lng_ref, lnb_ref, w1_ref, w2_ref, lng2_ref, lnb2_ref, o_ref):
    y = yf_ref[...] + yb_ref[...]
    ones = ones_ref[...]
    mu = _head_sum(y, ones) * (1.0 / HEAD_DIM)
    d = y - mu
    var = _head_sum(d * d, ones) * (1.0 / HEAD_DIM)
    yn = d * lax.rsqrt(var + GN_EPS) * gng_ref[...] + gnb_ref[...]
    oc = ((yn + bonus_ref[...]) * g_ref[...]).astype(BF16)
    merged = (sg_ref[:, :D_MODEL] * _dot(oa_ref[...].astype(BF16), pa_ref[...])
              + sg_ref[:, D_MODEL:2 * D_MODEL] * _dot(ob_ref[...].astype(BF16), pb_ref[...])
              + sg_ref[:, 2 * D_MODEL:] * _dot(oc, pc_ref[...]))
    o = _dot(merged.astype(BF16), wo_ref[...])
    x = _layer_norm(ALPHA * x_ref[...] + mod_ref[0, 5:6, :] * o, lng_ref[...], lnb_ref[...])

    h = (x * (1.0 + mod_ref[0, 7:8, :]) + mod_ref[0, 6:7, :]).astype(BF16)
    a = _dot(h, w1_ref[:, :D_FF])
    g = _dot(h, w1_ref[:, D_FF:])
    u = (a * _sigmoid(a) * g).astype(BF16)
    f = _dot(u, w2_ref[...])
    o_ref[...] = _layer_norm(ALPHA * x + 0.5 * mod_ref[0, 8:9, :] * f, lng2_ref[...], lnb2_ref[...])


def _merge_ffn(st, layer, x, mods, oa, oa_col, ob, ob_col, yf, yb, bonus, g, sg, pa, pb, pc, wo, ones, gn_g, gn_b, ln_g, ln_b,
               w1, w2, ln_g2, ln_b2):
    vec = lambda a, n: a.reshape(1, n)
    return pl.pallas_call(
        _merge_ffn_kernel,
        grid=(st.tiles,),
        in_specs=[
            _tile(D_MODEL), _mod_spec(st, layer), _tile(A_Q, oa_col), _tile(B_W, ob_col), _tile(C_WIDTH),
            _tile(C_WIDTH), _tile(C_WIDTH), _tile(C_WIDTH), _tile(GATE_COLS),
            _resident((A_Q, D_MODEL), layer), _resident((B_W, D_MODEL), layer), _resident((C_WIDTH, D_MODEL), layer),
            _resident((D_MODEL, D_MODEL), layer), _resident((LANES, LANES)),
            _resident((1, C_WIDTH)), _resident((1, C_WIDTH)), _resident((1, D_MODEL)), _resident((1, D_MODEL)),
            _resident((D_MODEL, 2 * D_FF), layer), _resident((D_FF, D_MODEL), layer),
            _resident((1, D_MODEL)), _resident((1, D_MODEL)),
        ],
        out_specs=_tile(D_MODEL),
        out_shape=jax.ShapeDtypeStruct((st.n_tok, D_MODEL), F32),
        compiler_params=_params("parallel"),
        name=f"merge_ffn2_{st.name}",
    )(x, mods, oa, ob, yf, yb, bonus, g, sg, pa, pb, pc, wo, ones,
      vec(gn_g, C_WIDTH), vec(gn_b, C_WIDTH), vec(ln_g, D_MODEL), vec(ln_b, D_MODEL),
      w1, w2, vec(ln_g2, D_MODEL), vec(ln_b2, D_MODEL))


def _block_diag2(a, b):
    za = jnp.zeros((a.shape[0], b.shape[1]), a.dtype)
    zb = jnp.zeros((b.shape[0], a.shape[1]), b.dtype)
    return jnp.concatenate([jnp.concatenate([a, za], axis=1), jnp.concatenate([zb, b], axis=1)], axis=0)


def kernel(x_prompt, x_sample, cache_attn_k, cache_attn_v, cache_na_k, cache_na_v, state_rwkv, c, c_ctx, w_ada, b_ada, ffn1_w_in, ffn1_w_out, ffn2_w_in, ffn2_w_out, w_in, w_shift, attn_sink, na_rpb, decay_w0, decay_up, iclr_a0, iclr_up, gate_up, k_k, k_a, r_k, gn_g, gn_b, proj_a, proj_b, proj_c, w_out, ln_g, ln_b):
    xs = {CTX: x_prompt.reshape(CTX.n_tok, D_MODEL), LAT: x_sample.reshape(LAT.n_tok, D_MODEL)}
    c_rows = jnp.zeros((MOD_ROWS, D_MODEL), F32).at[:DEC_BATCH].set(c).at[DEC_BATCH].set(c_ctx)
    mods_all = _ada_mods(c_rows, w_ada, b_ada)
    cos_t, sin_t = _rope_tables()
    head_of = np.arange(LANES) // HEAD_DIM
    ones_bd = jnp.asarray(head_of[:, None] == head_of[None, :], BF16)
    ctx_groups = BATCH // SCAN_NB
    zero_state = jnp.zeros((ctx_groups, HEAD_DIM, HEAD_DIM, LANES), F32)

    bf = lambda w: w.astype(BF16)
    ffn1_w, ffn2_w = (bf(ffn1_w_in), bf(ffn1_w_out)), (bf(ffn2_w_in), bf(ffn2_w_out))
    w_in_bf, gate_up_bf = bf(w_in), bf(gate_up)
    proj_w = (bf(proj_a), bf(proj_b), bf(proj_c), bf(w_out))

    new_ka, new_va, new_kb, new_vb, new_st = [], [], [], [], []
    for l in range(DEPTH):
        mods = mods_all
        ffn1 = (*ffn1_w, ln_g[l, 0], ln_b[l, 0], 0)
        ffn2 = (*ffn2_w, ln_g[l, 2], ln_b[l, 2], 6)
        w_in_l = w_in_bf
        prep_w = (w_shift[l],
                  _block_diag2(decay_up[l, 0], decay_up[l, 1]).astype(BF16),
                  _block_diag2(iclr_up[l, 0], iclr_up[l, 1]).astype(BF16),
                  gate_up_bf,
                  decay_w0[l].reshape(1, 2 * C_WIDTH), iclr_a0[l].reshape(1, 2 * C_WIDTH),
                  k_k[l].reshape(1, C_WIDTH), k_a[l].reshape(1, C_WIDTH), r_k[l].reshape(1, C_WIDTH), ones_bd)
        merge_w = (*proj_w, ones_bd, gn_g[l], gn_b[l], ln_g[l, 1], ln_b[l, 1])
        s0_lat = state_rwkv[:, l].transpose(3, 4, 2, 1, 0).reshape(1, HEAD_DIM, HEAD_DIM, LANES)

        for st in (CTX, LAT):
            x = _ffn(st, l, xs[st], mods, *ffn1)
            res = _mixer_in(st, l, x, mods, w_in_l, *prep_w, rope=None if st is CTX else (cos_t, sin_t),
                            sink=attn_sink[l] if st is CTX else None)
            za, zb, sg, r, v, nkk, w, b, k, bonus, g = res[:11]
            if st is CTX:
                o_ab = res[11]
                attn = (o_ab, 0, o_ab, 1)
                zac = za.reshape(BATCH, SEQ, A_COLS)
                zbc = zb.reshape(BATCH, SEQ, B_COLS)
                new_ka.append(zac[..., A_Q:A_Q + A_KV].reshape(BATCH, SEQ, A_KV_HEADS, HEAD_DIM))
                new_va.append(zac[..., A_Q + A_KV:].reshape(BATCH, SEQ, A_KV_HEADS, HEAD_DIM))
                new_kb.append(zbc[..., B_W:2 * B_W].reshape(BATCH, SEQ, B_HEADS, HEAD_DIM))
                new_vb.append(zbc[..., 2 * B_W:].reshape(BATCH, SEQ, B_HEADS, HEAD_DIM))
            else:
                oa = _win_attn(za, cache_attn_k[:, l].reshape(DEC_BATCH, PAST_LEN, A_KV),
                               cache_attn_v[:, l].reshape(DEC_BATCH, PAST_LEN, A_KV), attn_sink[l])
                ob = _na_attn(zb, cache_na_k[:, l].reshape(DEC_BATCH, PAST_LEN, B_W),
                              cache_na_v[:, l].reshape(DEC_BATCH, PAST_LEN, B_W), na_rpb[l].reshape(-1))
                attn = (oa, 0, ob, 0)
            yf, yb, s_t = _scan(st, w, b, k, nkk, r, v, zero_state if st is CTX else s0_lat)
            if st is CTX:
                s_t = s_t.reshape(ctx_groups, HEAD_DIM, HEAD_DIM, C_HEADS, 2, SCAN_NB)
                new_st.append(s_t.transpose(0, 5, 4, 3, 1, 2).reshape(BATCH, 2, C_HEADS, HEAD_DIM, HEAD_DIM))
            xs[st] = _merge_ffn(st, l, x, mods, *attn, yf, yb, bonus, g, sg, *merge_w, *ffn2[:4])

    y_p = xs[CTX].reshape(BATCH, SEQ, D_MODEL)
    y_s = xs[LAT].reshape(DEC_BATCH, DEC_SEQ, D_MODEL)
    return (y_p, y_s, jnp.stack(new_ka, axis=1), jnp.stack(new_va, axis=1), jnp.stack(new_kb, axis=1),
            jnp.stack(new_vb, axis=1), jnp.stack(new_st, axis=1))
```

```python
import collections
import functools

import jax
import jax.numpy as jnp
import numpy as np
from jax import lax
from jax.experimental import pallas as pl
from jax.experimental.pallas import tpu as pltpu

D_MODEL = 1024
BATCH = 16
SEQ = 256
DEPTH = 2
DEC_BATCH = 8
DEC_SEQ = 1024
PAST_LEN = 256
GRID_W = 64
GRID_ROWS = DEC_SEQ // GRID_W
HEAD_DIM = 64
SCALE = HEAD_DIM ** -0.5
A_HEADS = 8
A_KV_HEADS = 2
A_GROUP = A_HEADS // A_KV_HEADS
A_WINDOW = 128
A_BLOCK = 128
A_Q = A_HEADS * HEAD_DIM
A_KV = A_KV_HEADS * HEAD_DIM
A_COLS = A_Q + 2 * A_KV
B_HEADS = 8
B_W = B_HEADS * HEAD_DIM
B_COLS = 3 * B_W
NA_ROWS = 8
NA_COLS = 16
C_HEADS = 8
C_WIDTH = C_HEADS * HEAD_DIM
DECAY_RANK = 64
ICLR_RANK = 64
GATE_RANK = 128
C_COLS = 3 * C_WIDTH + 2 * DECAY_RANK + 2 * ICLR_RANK + GATE_RANK
GN_EPS = 64e-5
GATE_COLS = 3 * D_MODEL
IN_COLS = A_COLS + B_COLS + C_COLS + GATE_COLS
D_FF = 2816
N_MOD = 9
ROPE_BASE = 10000.0
LN_EPS = 1e-5
NEG_INF = -1e30
ALPHA = (2 * DEPTH) ** 0.25

LANES = 128
SUBLANES = 8
TM = 256
MOD_ROWS = 16
SCAN_TB = 64
SCAN_NB = LANES // (2 * C_HEADS)
VMEM_LIMIT = 56 * 1024 * 1024

F32 = jnp.float32
BF16 = jnp.bfloat16

Stream = collections.namedtuple("Stream", "name nb seq n_tok tiles tiles_per_seq mod_row")


def _make_stream(name, nb, seq, mod_row):
    return Stream(name, nb, seq, nb * seq, nb * seq // TM, seq // TM, mod_row)


CTX = _make_stream("ctx", BATCH, SEQ, lambda i: DEC_BATCH)
LAT = _make_stream("lat", DEC_BATCH, DEC_SEQ, lambda i: i // (DEC_SEQ // TM))


def _sigmoid(x):
    return 1.0 / (1.0 + jnp.exp(-x))


def _layer_norm(y, g, b):
    mu = jnp.mean(y, axis=-1, keepdims=True)
    d = y - mu
    var = jnp.mean(d * d, axis=-1, keepdims=True)
    return d * lax.rsqrt(var + LN_EPS) * g + b


def _dot(a, b):
    return jnp.dot(a, b, preferred_element_type=F32)


def _dot_nt(a, b):
    return lax.dot_general(a, b, (((1,), (1,)), ((), ())), preferred_element_type=F32)


def _head_sum(x, same_head):
    hi = x.astype(BF16)
    lo = (x - hi.astype(F32)).astype(BF16)
    slabs = [_dot(hi[:, j:j + LANES], same_head) + _dot(lo[:, j:j + LANES], same_head)
             for j in range(0, x.shape[-1], LANES)]
    return jnp.concatenate(slabs, axis=-1)


def _resident(shape, layer=None):
    zeros = (0,) * len(shape)
    if layer is None:
        return pl.BlockSpec(shape, lambda *_: zeros, pipeline_mode=pl.Buffered(1))
    return pl.BlockSpec((None,) + tuple(shape), lambda *_: (layer,) + zeros, pipeline_mode=pl.Buffered(1))


def _tile(width, col=0):
    return pl.BlockSpec((TM, width), lambda i: (i, col))


def _mod_spec(st, layer):
    return pl.BlockSpec((None, 1, N_MOD, D_MODEL), lambda i: (layer, st.mod_row(i), 0, 0))


def _params(*sem):
    return pltpu.CompilerParams(dimension_semantics=sem, vmem_limit_bytes=VMEM_LIMIT)


def _ada_kernel(c_ref, w_ref, b_ref, o_ref):
    c = c_ref[...]
    s = (c * _sigmoid(c)).astype(BF16)
    o_ref[0] = _dot(s, w_ref[0].astype(BF16)) + b_ref[0]


def _ada_mods(c_rows, w_ada, b_ada):
    out = pl.pallas_call(
        _ada_kernel,
        grid=(DEPTH, N_MOD),
        in_specs=[
            pl.BlockSpec((MOD_ROWS, D_MODEL), lambda l, j: (0, 0)),
            pl.BlockSpec((1, D_MODEL, D_MODEL), lambda l, j: (l, 0, j)),
            pl.BlockSpec((1, 1, D_MODEL), lambda l, j: (l, 0, j)),
        ],
        out_specs=pl.BlockSpec((1, MOD_ROWS, D_MODEL), lambda l, j: (l, 0, j)),
        out_shape=jax.ShapeDtypeStruct((DEPTH, MOD_ROWS, N_MOD * D_MODEL), F32),
        compiler_params=_params("parallel", "parallel"),
        name="ada_mods",
    )(c_rows, w_ada, b_ada.reshape(DEPTH, 1, N_MOD * D_MODEL))
    return out.reshape(DEPTH, MOD_ROWS, N_MOD, D_MODEL)


def _ffn_kernel(x_ref, mod_ref, w1_ref, w2_ref, g_ref, b_ref, o_ref, *, base):
    x = x_ref[...]
    shift = mod_ref[0, base:base + 1, :]
    scale = mod_ref[0, base + 1:base + 2, :]
    gate = mod_ref[0, base + 2:base + 3, :]
    h = (x * (1.0 + scale) + shift).astype(BF16)
    a = _dot(h, w1_ref[:, :D_FF])
    g = _dot(h, w1_ref[:, D_FF:])
    u = (a * _sigmoid(a) * g).astype(BF16)
    f = _dot(u, w2_ref[...])
    o_ref[...] = _layer_norm(ALPHA * x + 0.5 * gate * f, g_ref[...], b_ref[...])


_FFN_TILES = 2


def _ffn(st, layer, x, mods, w1, w2, ln_g, ln_b, base):
    assert st.seq % TM == 0 and (st.tiles_per_seq % _FFN_TILES == 0 or st is CTX)
    block = pl.BlockSpec((_FFN_TILES * TM, D_MODEL), lambda i: (i, 0))
    return pl.pallas_call(
        functools.partial(_ffn_kernel, base=base),
        grid=(st.tiles // _FFN_TILES,),
        in_specs=[
            block,
            pl.BlockSpec((None, 1, N_MOD, D_MODEL), lambda i: (layer, st.mod_row(i * _FFN_TILES), 0, 0)),
            _resident((D_MODEL, 2 * D_FF), layer), _resident((D_FF, D_MODEL), layer),
            _resident((1, D_MODEL)), _resident((1, D_MODEL)),
        ],
        out_specs=block,
        out_shape=jax.ShapeDtypeStruct((st.n_tok, D_MODEL), F32),
        compiler_params=_params("parallel"),
        name=f"ffn{base // 6 + 1}_{st.name}",
    )(x, mods, w1, w2, ln_g.reshape(1, D_MODEL), ln_b.reshape(1, D_MODEL))


_COL_A = 0
_COL_B = A_COLS
_COL_C = A_COLS + B_COLS
_COL_G = A_COLS + B_COLS + C_COLS


def _softmax_pv(scores, values, sink=None):
    m = scores[0].max(axis=-1, keepdims=True)
    for s in scores[1:]:
        m = jnp.maximum(m, s.max(axis=-1, keepdims=True))
    if sink is not None:
        m = jnp.maximum(m, sink)
    den = jnp.exp(sink - m) if sink is not None else 0.0
    acc = None
    for s, v in zip(scores, values):
        p = jnp.exp(s - m)
        den = den + p.sum(axis=-1, keepdims=True)
        pv = _dot(p.astype(BF16), v)
        acc = pv if acc is None else acc + pv
    return acc / den


_PW = 2 * HEAD_DIM


def _left_half(rows):
    return lax.broadcasted_iota(jnp.int32, (rows, _PW), 1) < HEAD_DIM


assert SCALE == 2.0 ** -3


def _split_heads(x, left):
    x = x * SCALE
    return jnp.concatenate([jnp.where(left, x, 0.0), jnp.where(left, 0.0, x)], axis=0).astype(BF16)


def _both_halves(slab, swapped, left, head):
    return (jnp.where(left, slab, swapped) if head == 0 else jnp.where(left, swapped, slab)).astype(BF16)


def _ctx_attn_kernel(sink_ref, za_ref, zb_ref, o_ref):
    left = _left_half(SEQ)
    slab = lambda ref, j: ref[:, j * _PW:(j + 1) * _PW]
    k_slab = slab(za_ref, A_Q // _PW)
    v_slab = slab(za_ref, (A_Q + A_KV) // _PW)
    k_swap = pltpu.roll(k_slab, HEAD_DIM, 1)
    v_swap = pltpu.roll(v_slab, HEAD_DIM, 1)
    row = lax.broadcasted_iota(jnp.int32, (A_GROUP * SEQ, 1), 0) // SEQ
    for g in range(A_KV_HEADS):
        slabs = range(g * A_GROUP // 2, (g + 1) * A_GROUP // 2)
        q4 = jnp.concatenate([_split_heads(slab(za_ref, j), left) for j in slabs], axis=0)
        sink = jnp.zeros((A_GROUP * SEQ, 1), F32)
        for j in range(A_GROUP):
            sink = jnp.where(row == j, sink_ref[A_GROUP * g + j], sink)
        o4 = _softmax_pv([_dot_nt(q4, _both_halves(k_slab, k_swap, left, g))],
                         [_both_halves(v_slab, v_swap, left, g)], sink)
        for i, j in enumerate(slabs):
            o_ref[:, j * _PW:(j + 1) * _PW] = jnp.where(left, o4[2 * i * SEQ:(2 * i + 1) * SEQ],
                                                        o4[(2 * i + 1) * SEQ:(2 * i + 2) * SEQ])
    for j in range(B_HEADS // 2):
        k2 = slab(zb_ref, B_HEADS // 2 + j).astype(BF16)
        v2 = slab(zb_ref, B_HEADS + j).astype(BF16)
        o2 = _softmax_pv([_dot_nt(_split_heads(slab(zb_ref, j), left), k2)], [v2])
        o_ref[:, A_Q + j * _PW:A_Q + (j + 1) * _PW] = jnp.where(left, o2[:SEQ], o2[SEQ:])


def _rope(x, cos, sin_signed):
    w = x.shape[-1]
    half = HEAD_DIM // 2
    lane = lax.broadcasted_iota(jnp.int32, x.shape, 1)
    partner = jnp.where(lane % HEAD_DIM < half, pltpu.roll(x, w - half, 1), pltpu.roll(x, half, 1))
    return x * cos + partner * sin_signed


def _win_attn_kernel(sink_ref, za_ref, ck_ref, cv_ref, band_ref, o_ref):
    n = pl.program_id(1)
    nb = DEC_SEQ // A_BLOCK
    q0 = pl.multiple_of(n * A_BLOCK, A_BLOCK)
    q = za_ref[pl.ds(q0, A_BLOCK), :A_Q]
    ks, vs = [], []
    for j in range(3):
        kb = jnp.clip(n - 1 + j, 0, nb - 1)
        k0 = pl.multiple_of(kb * A_BLOCK, A_BLOCK)
        ks.append(za_ref[pl.ds(k0, A_BLOCK), A_Q:A_Q + A_KV])
        vs.append(za_ref[pl.ds(k0, A_BLOCK), A_Q + A_KV:])
    left = _left_half(A_BLOCK)
    bands = []
    for x in (jnp.concatenate(ks, axis=0), jnp.concatenate(vs, axis=0), ck_ref[0], cv_ref[0]):
        bands.append((x, pltpu.roll(x, HEAD_DIM, 1), _left_half(x.shape[0])))
    m4 = A_GROUP * A_BLOCK
    kpos = (n - 1) * A_BLOCK + lax.broadcasted_iota(jnp.int32, (1, 3 * A_BLOCK), 1)
    band = band_ref[...] + jnp.where((kpos >= 0) & (kpos < DEC_SEQ), 0.0, NEG_INF)
    row = lax.broadcasted_iota(jnp.int32, (m4, 1), 0) // A_BLOCK
    for g in range(A_KV_HEADS):
        kband, vband, ck, cv = (_both_halves(x, swapped, lft, g) for x, swapped, lft in bands)
        slabs = range(g * A_GROUP // 2, (g + 1) * A_GROUP // 2)
        q4 = jnp.concatenate([_split_heads(q[:, j * _PW:(j + 1) * _PW], left) for j in slabs], axis=0)
        s_loc = _dot_nt(q4, kband) + band
        s_ctx = _dot_nt(q4, ck)
        sink = jnp.zeros((m4, 1), F32)
        for j in range(A_GROUP):
            sink = jnp.where(row == j, sink_ref[A_GROUP * g + j], sink)
        o4 = _softmax_pv([s_loc, s_ctx], [vband, cv], sink)
        for i, j in enumerate(slabs):
            o_ref[:, j * _PW:(j + 1) * _PW] = jnp.where(left, o4[2 * i * A_BLOCK:(2 * i + 1) * A_BLOCK],
                                                        o4[(2 * i + 1) * A_BLOCK:(2 * i + 2) * A_BLOCK])


def _rope_tables():
    t = np.arange(DEC_SEQ)
    n_freq = HEAD_DIM // 4
    inv = ROPE_BASE ** (-jnp.arange(n_freq, dtype=F32) / n_freq)
    rows = jnp.asarray(t // GRID_W, F32)
    cols = jnp.asarray(t % GRID_W, F32)
    ang = jnp.concatenate([rows[:, None] * inv, cols[:, None] * inv], axis=-1)
    cos, sin = jnp.cos(ang), jnp.sin(ang)
    cos_h = jnp.concatenate([cos, cos], axis=-1)
    sin_h = jnp.concatenate([-sin, sin], axis=-1)
    return jnp.tile(cos_h, (1, A_HEADS)), jnp.tile(sin_h, (1, A_HEADS))


def _window_band():
    qi = np.arange(A_GROUP * A_BLOCK)[:, None] % A_BLOCK
    kj = np.arange(3 * A_BLOCK)[None, :]
    return jnp.asarray(np.where(np.abs(kj - A_BLOCK - qi) <= A_WINDOW, 0.0, NEG_INF), F32)


def _win_attn(za, ck, cv, sink):
    nb = DEC_SEQ // A_BLOCK
    return pl.pallas_call(
        _win_attn_kernel,
        grid=(DEC_BATCH, nb),
        in_specs=[
            pl.BlockSpec(memory_space=pltpu.SMEM),
            pl.BlockSpec((DEC_SEQ, A_COLS), lambda b, n: (b, 0)),
            pl.BlockSpec((1, PAST_LEN, A_KV), lambda b, n: (b, 0, 0)),
            pl.BlockSpec((1, PAST_LEN, A_KV), lambda b, n: (b, 0, 0)),
            pl.BlockSpec((A_GROUP * A_BLOCK, 3 * A_BLOCK), lambda b, n: (0, 0)),
        ],
        out_specs=pl.BlockSpec((A_BLOCK, A_Q), lambda b, n: (b * nb + n, 0)),
        out_shape=jax.ShapeDtypeStruct((LAT.n_tok, A_Q), F32),
        compiler_params=_params("parallel", "arbitrary"),
        name="win_attn",
    )(sink, za, ck, cv, _window_band())


_NA_QBLK = 256
_RPB_I = 2 * NA_ROWS - 1
_RPB_J = 2 * NA_COLS - 1


def _na_build_bias(rpb_ref, head, bias_ref, slot):
    shape = (GRID_W, 2 * GRID_W)
    qc = lax.broadcasted_iota(jnp.int32, shape, 0)
    lane = lax.broadcasted_iota(jnp.int32, shape, 1)
    kc = lane % GRID_W
    left = lane < GRID_W
    cs = jnp.clip(qc - NA_COLS // 2, 0, GRID_W - NA_COLS)
    idx = jnp.where((kc >= cs) & (kc < cs + NA_COLS), jnp.clip(kc - qc + NA_COLS - 1, 0, _RPB_J - 1), -1)
    base = head * (_RPB_I * _RPB_J)
    tables = []
    for i in range(_RPB_I):
        t = jnp.full(shape, NEG_INF, F32)
        for j in range(_RPB_J):
            t = jnp.where(idx == j, rpb_ref[base + i * _RPB_J + j], t)
        tables.append(t)
    neg = jnp.full(shape, NEG_INF, F32)
    half = NA_ROWS // 2
    for qr in range(GRID_ROWS):
        rs = min(max(qr - half, 0), GRID_ROWS - NA_ROWS)
        for m in range(GRID_ROWS // 2):
            kr0, kr1 = 2 * m, 2 * m + 1
            v0 = rs <= kr0 < rs + NA_ROWS
            v1 = rs <= kr1 < rs + NA_ROWS
            i0 = kr0 - qr + NA_ROWS - 1
            if v0 and v1:
                piece = jnp.where(left, tables[i0], tables[i0 + 1])
            elif v0:
                piece = jnp.where(left, tables[i0], NEG_INF)
            elif v1:
                piece = jnp.where(left, NEG_INF, tables[i0 + 1])
            else:
                piece = neg
            bias_ref[slot, qr * GRID_W:(qr + 1) * GRID_W, m * 2 * GRID_W:(m + 1) * 2 * GRID_W] = piece


def _na_attn_kernel(rpb_ref, q_ref, k_ref, v_ref, ck_ref, cv_ref, o_ref, bias_ref):
    hp = pl.program_id(0)

    @pl.when(pl.program_id(1) == 0)
    def _():
        for half in range(2):
            _na_build_bias(rpb_ref, 2 * hp + half, bias_ref, half)

    ck2 = ck_ref[0].astype(BF16)
    cv2 = cv_ref[0].astype(BF16)
    left = _left_half(_NA_QBLK)
    rows_per_blk = _NA_QBLK // GRID_W
    for qb in range(DEC_SEQ // _NA_QBLK):
        rows = slice(qb * _NA_QBLK, (qb + 1) * _NA_QBLK)
        r_lo = min(max(qb * rows_per_blk - NA_ROWS // 2, 0), GRID_ROWS - NA_ROWS)
        r_hi = min(max((qb + 1) * rows_per_blk - 1 - NA_ROWS // 2, 0), GRID_ROWS - NA_ROWS) + NA_ROWS
        keys = slice(r_lo // 2 * 2 * GRID_W, -(-r_hi // 2) * 2 * GRID_W)
        q2 = _split_heads(q_ref[rows, :], left)
        bias = jnp.concatenate([bias_ref[0, rows, keys], bias_ref[1, rows, keys]], axis=0)
        s_loc = _dot_nt(q2, k_ref[keys, :].astype(BF16)) + bias
        s_ctx = _dot_nt(q2, ck2)
        o2 = _softmax_pv([s_loc, s_ctx], [v_ref[keys, :].astype(BF16), cv2])
        o_ref[rows, :] = jnp.where(left, o2[:_NA_QBLK], o2[_NA_QBLK:])


def _na_attn(zb, ck, cv, rpb_flat):
    pairs = B_HEADS // 2
    pw = _PW
    return pl.pallas_call(
        _na_attn_kernel,
        grid=(pairs, DEC_BATCH),
        in_specs=[
            pl.BlockSpec(memory_space=pltpu.SMEM),
            pl.BlockSpec((DEC_SEQ, pw), lambda hp, b: (b, hp)),
            pl.BlockSpec((DEC_SEQ, pw), lambda hp, b: (b, pairs + hp)),
            pl.BlockSpec((DEC_SEQ, pw), lambda hp, b: (b, 2 * pairs + hp)),
            pl.BlockSpec((1, PAST_LEN, pw), lambda hp, b: (b, 0, hp)),
            pl.BlockSpec((1, PAST_LEN, pw), lambda hp, b: (b, 0, hp)),
        ],
        out_specs=pl.BlockSpec((DEC_SEQ, pw), lambda hp, b: (b, hp)),
        out_shape=jax.ShapeDtypeStruct((LAT.n_tok, B_W), F32),
        scratch_shapes=[pltpu.VMEM((2, DEC_SEQ, DEC_SEQ), F32)],
        compiler_params=_params("arbitrary", "arbitrary"),
        name="na_attn",
    )(rpb_flat, zb, zb, zb, ck, cv)


_C_R, _C_K, _C_V = 0, C_WIDTH, 2 * C_WIDTH
_C_WLO = 3 * C_WIDTH
_C_ALO = _C_WLO + 2 * DECAY_RANK
_C_GLO = _C_ALO + 2 * ICLR_RANK
_HALO = SUBLANES


def _mixer_in_kernel(*refs, tiles_per_seq, rope, attend):
    (x_ref, xp_ref, xn_ref, mod_ref, win_ref, wsh_ref, dup_ref, aup_ref, gup_ref, w0_ref, a0_ref,
     kk_ref, ka_ref, rk_ref, ones_ref) = refs[:15]
    extra = refs[15:15 + 2 * rope + attend]
    outs = refs[15 + len(extra):]
    (za_ref, zb_ref, sg_ref, r_ref, v_ref, nkk_ref, w_ref, b_ref, k_ref, bonus_ref, g_ref) = outs[:11]
    rope_refs = extra[:2] if rope else ()
    j = pl.program_id(0) % tiles_per_seq
    scale = 1.0 + mod_ref[0, 4:5, :]
    shift = mod_ref[0, 3:4, :]
    x = x_ref[...]
    h = (x * scale + shift).astype(BF16)
    za = _dot(h, win_ref[:, _COL_A:_COL_B])
    if rope:
        cos_ref, sin_ref = rope_refs
        za_ref[:, :A_Q] = _rope(za[:, :A_Q], cos_ref[...], sin_ref[...])
        za_ref[:, A_Q:A_Q + A_KV] = _rope(za[:, A_Q:A_Q + A_KV], cos_ref[:, :A_KV], sin_ref[:, :A_KV])
        za_ref[:, A_Q + A_KV:] = za[:, A_Q + A_KV:]
    else:
        za_ref[...] = za
    zb_ref[...] = _dot(h, win_ref[:, _COL_B:_COL_C])
    if attend:
        _ctx_attn_kernel(extra[-1], za_ref, zb_ref, outs[11])
    sg_ref[...] = _sigmoid(_dot(h, win_ref[:, _COL_G:]))

    x_ext = jnp.concatenate([xp_ref[...], x, xn_ref[...]], axis=0)
    z_ext = _dot((x_ext * scale + shift).astype(BF16), win_ref[:, _COL_C:_COL_G])
    row = lax.broadcasted_iota(jnp.int32, (TM, C_COLS), 0)
    z = z_ext[_HALO:_HALO + TM]
    z_prev = jnp.where(jnp.logical_and(row == 0, j == 0), 0.0, z_ext[_HALO - 1:_HALO - 1 + TM])
    z_next = jnp.where(jnp.logical_and(row == TM - 1, j == tiles_per_seq - 1), 0.0, z_ext[_HALO + 1:_HALO + 1 + TM])
    zc = z_prev * wsh_ref[0:1, :] + z * wsh_ref[1:2, :] + z_next * wsh_ref[2:3, :]

    r = zc[:, _C_R:_C_R + C_WIDTH]
    k = zc[:, _C_K:_C_K + C_WIDTH]
    v = zc[:, _C_V:_C_V + C_WIDTH]
    wlo = zc[:, _C_WLO:_C_WLO + 2 * DECAY_RANK]
    alo = zc[:, _C_ALO:_C_ALO + 2 * ICLR_RANK]
    glo = zc[:, _C_GLO:_C_GLO + GATE_RANK]
    ones = ones_ref[...]

    g_ref[...] = _dot(_sigmoid(glo).astype(BF16), gup_ref[...])
    kk = k * kk_ref[...]
    norm = jnp.sqrt(_head_sum(kk * kk, ones))
    kk = kk / jnp.maximum(norm, 1e-12)
    xw = w0_ref[...] + _dot(jnp.tanh(wlo).astype(BF16), dup_ref[...])
    logw = -(jnp.maximum(-xw, 0.0) + jnp.log(1.0 + jnp.exp(-jnp.abs(xw)))) - 0.5
    decay = jnp.exp(-jnp.exp(logw))
    a = _sigmoid(a0_ref[...] + _dot(alo.astype(BF16), aup_ref[...]))
    ka = ka_ref[...]
    kd_sum = None
    for d in range(2):
        a_d = a[:, d * C_WIDTH:(d + 1) * C_WIDTH]
        k_d = k * (1.0 + (a_d - 1.0) * ka)
        w_ref[d] = decay[:, d * C_WIDTH:(d + 1) * C_WIDTH]
        b_ref[d] = kk * a_d
        k_ref[d] = k_d
        kd_sum = k_d if kd_sum is None else kd_sum + k_d
    bonus_ref[...] = _head_sum(r * kd_sum * rk_ref[...], ones) * v
    r_ref[0] = r
    v_ref[0] = v
    nkk_ref[0] = -kk


def _mixer_in(st, layer, x, mods, w_in, w_shift, dec_up, iclr_up, gate_up, w0, a0, k_k, k_a, r_k, ones_bd, rope=None,
              sink=None):
    tiles_per_halo = TM // _HALO
    n_halo = st.n_tok // _HALO
    row = lambda n: _resident((1, n))
    stacked = lambda d: (pl.BlockSpec((d, TM, C_WIDTH), lambda i: (0, i, 0)),
                         jax.ShapeDtypeStruct((d, st.n_tok, C_WIDTH), F32))
    flat = lambda w: (_tile(w), jax.ShapeDtypeStruct((st.n_tok, w), F32))
    outs = [flat(A_COLS), flat(B_COLS), flat(GATE_COLS)] + [stacked(1)] * 3 + [stacked(2)] * 3 + [flat(C_WIDTH)] * 2
    rope_specs = [pl.BlockSpec((TM, A_Q), lambda i: (i % st.tiles_per_seq, 0))] * 2 if rope else []
    attend = sink is not None
    if attend:
        assert st.seq == TM == SEQ
        outs.append(flat(A_Q + B_W))
    sink_specs = [pl.BlockSpec(memory_space=pltpu.SMEM)] if attend else []
    return pl.pallas_call(
        functools.partial(_mixer_in_kernel, tiles_per_seq=st.tiles_per_seq, rope=bool(rope), attend=attend),
        grid=(st.tiles,),
        in_specs=[
            _tile(D_MODEL),
            pl.BlockSpec((_HALO, D_MODEL), lambda i: (jnp.maximum(i * tiles_per_halo - 1, 0), 0)),
            pl.BlockSpec((_HALO, D_MODEL), lambda i: (jnp.minimum((i + 1) * tiles_per_halo, n_halo - 1), 0)),
            _mod_spec(st, layer),
            _resident((D_MODEL, IN_COLS), layer),
            _resident((3, C_COLS)),
            _resident((2 * DECAY_RANK, 2 * C_WIDTH)),
            _resident((2 * ICLR_RANK, 2 * C_WIDTH)),
            _resident((GATE_RANK, C_WIDTH), layer),
            row(2 * C_WIDTH), row(2 * C_WIDTH), row(C_WIDTH), row(C_WIDTH), row(C_WIDTH),
            _resident((LANES, LANES)),
        ] + rope_specs + sink_specs,
        out_specs=[o[0] for o in outs],
        out_shape=[o[1] for o in outs],
        compiler_params=_params("parallel"),
        name=f"mixer_in_{st.name}",
    )(x, x, x, mods, w_in, w_shift, dec_up, iclr_up, gate_up, w0, a0, k_k, k_a, r_k, ones_bd, *(rope or ()), *((sink,) if attend else ()))


_PB = 2 * SCAN_NB


def _scan_kernel(wf, wb, af, ab, bf, bb, kf, kb, rf, rb, vf, vb, s0_ref, yf_ref, yb_ref, state,
                 ops_a, ops_b, red, wcum, y_a, y_b):
    @pl.when(pl.program_id(1) == 0)
    def _():
        state[...] = s0_ref[...]
        y_b[...] = jnp.zeros_like(y_b)

    wcum[...] = jnp.ones_like(wcum)
    left = lax.broadcasted_iota(jnp.int32, (_PB, LANES), 1) < HEAD_DIM
    low = lax.broadcasted_iota(jnp.int32, (SUBLANES, LANES), 0) < SUBLANES // 2

    def load_ops(i, ops):
        ir = SCAN_TB - 1 - i
        for p, (pf, pb, qf, qb) in enumerate(((wf, wb, af, ab), (bf, bb, kf, kb), (rf, rb, vf, vb))):
            xp = jnp.concatenate([pf[0, :, i, :], pb[0, :, ir, :]], axis=0)
            xq = jnp.concatenate([qf[0, :, i, :], qb[0, :, ir, :]], axis=0)
            blocks = []
            for j in range(C_HEADS // 2):
                pv = xp[:, j * LANES:(j + 1) * LANES]
                qv = xq[:, j * LANES:(j + 1) * LANES]
                blocks.append(jnp.where(left, pv, pltpu.roll(qv, HEAD_DIM, 1)))
                blocks.append(jnp.where(left, pltpu.roll(pv, HEAD_DIM, 1), qv))
            ops[p] = jnp.concatenate(blocks, axis=0).T

    def fold(x):
        return x.reshape(HEAD_DIM // SUBLANES, SUBLANES, LANES).sum(axis=0)

    def compute(ops, ybuf):
        w_prev = wcum[...]
        w_now = w_prev * ops[0, 0:HEAD_DIM]
        wcum[...] = w_now
        w_inv = 1.0 / w_now
        a = ops[0, HEAD_DIM:] * w_prev
        b = ops[1, 0:HEAD_DIM]
        k = ops[1, HEAD_DIM:]
        r = ops[2, 0:HEAD_DIM]
        br = jnp.sum(b * r, axis=0, keepdims=True)
        kr = jnp.sum(k * r, axis=0, keepdims=True)
        b = b * w_inv
        k = k * w_inv
        r = r * w_now
        for vi in range(HEAD_DIM):
            z = state[0, vi]
            p1 = fold(z * a)
            p2 = fold(z * r)
            t = jnp.where(low, p1, p2) + pltpu.roll(jnp.where(low, p2, p1), SUBLANES // 2, 0)
            t = t + pltpu.roll(t, SUBLANES - 2, 0)
            red[vi] = t + pltpu.roll(t, SUBLANES - 1, 0)
        for vi in range(HEAD_DIM):
            state[0, vi] = state[0, vi] + red[vi, 0:1] * b + ops[2, HEAD_DIM + vi:HEAD_DIM + vi + 1] * k
        ybuf[...] = red[:, SUBLANES // 2, :] + red[:, 0, :] * br + ops[2, HEAD_DIM:] * kr

    def emit(i, ybuf):
        y = ybuf[...]
        t = jnp.concatenate([y, y], axis=0).T
        cols = [jnp.where(left, t[(2 * j) * _PB:(2 * j + 1) * _PB], t[(2 * j + 1) * _PB:(2 * j + 2) * _PB])
                for j in range(C_HEADS // 2)]
        out = jnp.concatenate(cols, axis=1)
        yf_ref[:, i, :] = out[:SCAN_NB]
        yb_ref[:, SCAN_TB - 1 - i, :] = out[SCAN_NB:]

    load_ops(0, ops_a)

    def body(m, carry):
        i0 = 2 * m
        load_ops(i0 + 1, ops_b)
        compute(ops_a, y_a)
        emit(jnp.maximum(i0 - 1, 0), y_b)
        load_ops(jnp.minimum(i0 + 2, SCAN_TB - 1), ops_a)
        compute(ops_b, y_b)
        emit(i0, y_a)
        return carry

    lax.fori_loop(0, SCAN_TB // 2, body, 0)
    emit(SCAN_TB - 1, y_b)
    w_end = wcum[...]
    for vi in range(HEAD_DIM):
        state[0, vi] = state[0, vi] * w_end


def _scan(st, w, b, k, nkk, r, v, s0):
    groups = st.nb // SCAN_NB
    n_t = st.seq // SCAN_TB
    view = lambda x: x.reshape(x.shape[0], st.nb, st.seq, C_WIDTH)
    blk = (1, SCAN_NB, SCAN_TB, C_WIDTH)
    fwd = lambda d: pl.BlockSpec(blk, lambda g, t: (d, g, t, 0))
    bwd = lambda d: pl.BlockSpec(blk, lambda g, t: (d, g, n_t - 1 - t, 0))
    st_blk = (1, HEAD_DIM, HEAD_DIM, LANES)
    w, b, k, nkk, r, v = (view(x) for x in (w, b, k, nkk, r, v))
    tile = pltpu.VMEM((3, LANES, LANES), F32)
    ybuf = pltpu.VMEM((HEAD_DIM, LANES), F32)
    yblk = (SCAN_NB, SCAN_TB, C_WIDTH)
    y_shape = jax.ShapeDtypeStruct((st.nb, st.seq, C_WIDTH), F32)
    yf, yb, s_t = pl.pallas_call(
        _scan_kernel,
        grid=(groups, n_t),
        in_specs=[fwd(0), bwd(1), fwd(0), bwd(0), fwd(0), bwd(1), fwd(0), bwd(1), fwd(0), bwd(0), fwd(0), bwd(0),
                  pl.BlockSpec(st_blk, lambda g, t: (g, 0, 0, 0), pipeline_mode=pl.Buffered(1))],
        out_specs=[pl.BlockSpec(yblk, lambda g, t: (g, t, 0)), pl.BlockSpec(yblk, lambda g, t: (g, n_t - 1 - t, 0)),
                   pl.BlockSpec(st_blk, lambda g, t: (g, 0, 0, 0))],
        out_shape=[y_shape, y_shape, jax.ShapeDtypeStruct((groups,) + st_blk[1:], F32)],
        scratch_shapes=[tile, tile, pltpu.VMEM((HEAD_DIM, SUBLANES, LANES), F32), ybuf, ybuf, ybuf],
        compiler_params=_params("arbitrary", "arbitrary"),
        name=f"rwkv_scan_{st.name}",
    )(w, w, nkk, nkk, b, b, k, k, r, r, v, v, s0)
    return yf.reshape(st.n_tok, C_WIDTH), yb.reshape(st.n_tok, C_WIDTH), s_t


def _merge_ffn_kernel(x_ref, mod_ref, oa_ref, ob_ref, yf_ref, yb_ref, bonus_ref, g_ref, sg_ref, pa_ref, pb_ref, pc_ref,
                      wo_ref, ones_ref, gng_ref, gnb_ref, lng_ref, lnb_ref, w1_ref, w2_ref, lng2_ref, lnb2_ref, o_ref):
    y = yf_ref[...] + yb_ref[...]
    ones = ones_ref[...]
    mu = _head_sum(y, ones) * (1.0 / HEAD_DIM)
    d = y - mu
    var = _head_sum(d * d, ones) * (1.0 / HEAD_DIM)
    yn = d * lax.rsqrt(var + GN_EPS) * gng_ref[...] + gnb_ref[...]
    oc = ((yn + bonus_ref[...]) * g_ref[...]).astype(BF16)
    merged = (sg_ref[:, :D_MODEL] * _dot(oa_ref[...].astype(BF16), pa_ref[...])
              + sg_ref[:, D_MODEL:2 * D_MODEL] * _dot(ob_ref[...].astype(BF16), pb_ref[...])
              + sg_ref[:, 2 * D_MODEL:] * _dot(oc, pc_ref[...]))
    o = _dot(merged.astype(BF16), wo_ref[...])
    x = _layer_norm(ALPHA * x_ref[...] + mod_ref[0, 5:6, :] * o, lng_ref[...], lnb_ref[...])

    h = (x * (1.0 + mod_ref[0, 7:8, :]) + mod_ref[0, 6:7, :]).astype(BF16)
    a = _dot(h, w1_ref[:, :D_FF])
    g = _dot(h, w1_ref[:, D_FF:])
    u = (a * _sigmoid(a) * g).astype(BF16)
    f = _dot(u, w2_ref[...])
    o_ref[...] = _layer_norm(ALPHA * x + 0.5 * mod_ref[0, 8:9, :] * f, lng2_ref[...], lnb2_ref[...])


def _merge_ffn(st, layer, x, mods, oa, oa_col, ob, ob_col, yf, yb, bonus, g, sg, pa, pb, pc, wo, ones, gn_g, gn_b, ln_g, ln_b,
               w1, w2, ln_g2, ln_b2):
    vec = lambda a, n: a.reshape(1, n)
    return pl.pallas_call(
        _merge_ffn_kernel,
        grid=(st.tiles,),
        in_specs=[
            _tile(D_MODEL), _mod_spec(st, layer), _tile(A_Q, oa_col), _tile(B_W, ob_col), _tile(C_WIDTH),
            _tile(C_WIDTH), _tile(C_WIDTH), _tile(C_WIDTH), _tile(GATE_COLS),
            _resident((A_Q, D_MODEL), layer), _resident((B_W, D_MODEL), layer), _resident((C_WIDTH, D_MODEL), layer),
            _resident((D_MODEL, D_MODEL), layer), _resident((LANES, LANES)),
            _resident((1, C_WIDTH)), _resident((1, C_WIDTH)), _resident((1, D_MODEL)), _resident((1, D_MODEL)),
            _resident((D_MODEL, 2 * D_FF), layer), _resident((D_FF, D_MODEL), layer),
            _resident((1, D_MODEL)), _resident((1, D_MODEL)),
        ],
        out_specs=_tile(D_MODEL),
        out_shape=jax.ShapeDtypeStruct((st.n_tok, D_MODEL), F32),
        compiler_params=_params("parallel"),
        name=f"merge_ffn2_{st.name}",
    )(x, mods, oa, ob, yf, yb, bonus, g, sg, pa, pb, pc, wo, ones,
      vec(gn_g, C_WIDTH), vec(gn_b, C_WIDTH), vec(ln_g, D_MODEL), vec(ln_b, D_MODEL),
      w1, w2, vec(ln_g2, D_MODEL), vec(ln_b2, D_MODEL))


def _block_diag2(a, b):
    za = jnp.zeros((a.shape[0], b.shape[1]), a.dtype)
    zb = jnp.zeros((b.shape[0], a.shape[1]), b.dtype)
    return jnp.concatenate([jnp.concatenate([a, za], axis=1), jnp.concatenate([zb, b], axis=1)], axis=0)


def kernel(x_prompt, x_sample, cache_attn_k, cache_attn_v, cache_na_k, cache_na_v, state_rwkv, c, c_ctx, w_ada, b_ada, ffn1_w_in, ffn1_w_out, ffn2_w_in, ffn2_w_out, w_in, w_shift, attn_sink, na_rpb, decay_w0, decay_up, iclr_a0, iclr_up, gate_up, k_k, k_a, r_k, gn_g, gn_b, proj_a, proj_b, proj_c, w_out, ln_g, ln_b):
    xs = {CTX: x_prompt.reshape(CTX.n_tok, D_MODEL), LAT: x_sample.reshape(LAT.n_tok, D_MODEL)}
    c_rows = jnp.zeros((MOD_ROWS, D_MODEL), F32).at[:DEC_BATCH].set(c).at[DEC_BATCH].set(c_ctx)
    mods_all = _ada_mods(c_rows, w_ada, b_ada)
    cos_t, sin_t = _rope_tables()
    head_of = np.arange(LANES) // HEAD_DIM
    ones_bd = jnp.asarray(head_of[:, None] == head_of[None, :], BF16)
    ctx_groups = BATCH // SCAN_NB
    zero_state = jnp.zeros((ctx_groups, HEAD_DIM, HEAD_DIM, LANES), F32)

    bf = lambda w: w.astype(BF16)
    ffn1_w, ffn2_w = (bf(ffn1_w_in), bf(ffn1_w_out)), (bf(ffn2_w_in), bf(ffn2_w_out))
    w_in_bf, gate_up_bf = bf(w_in), bf(gate_up)
    proj_w = (bf(proj_a), bf(proj_b), bf(proj_c), bf(w_out))

    new_ka, new_va, new_kb, new_vb, new_st = [], [], [], [], []
    for l in range(DEPTH):
        mods = mods_all
        ffn1 = (*ffn1_w, ln_g[l, 0], ln_b[l, 0], 0)
        ffn2 = (*ffn2_w, ln_g[l, 2], ln_b[l, 2], 6)
        w_in_l = w_in_bf
        prep_w = (w_shift[l],
                  _block_diag2(decay_up[l, 0], decay_up[l, 1]).astype(BF16),
                  _block_diag2(iclr_up[l, 0], iclr_up[l, 1]).astype(BF16),
                  gate_up_bf,
                  decay_w0[l].reshape(1, 2 * C_WIDTH), iclr_a0[l].reshape(1, 2 * C_WIDTH),
                  k_k[l].reshape(1, C_WIDTH), k_a[l].reshape(1, C_WIDTH), r_k[l].reshape(1, C_WIDTH), ones_bd)
        merge_w = (*proj_w, ones_bd, gn_g[l], gn_b[l], ln_g[l, 1], ln_b[l, 1])
        s0_lat = state_rwkv[:, l].transpose(3, 4, 2, 1, 0).reshape(1, HEAD_DIM, HEAD_DIM, LANES)

        for st in (CTX, LAT):
            x = _ffn(st, l, xs[st], mods, *ffn1)
            res = _mixer_in(st, l, x, mods, w_in_l, *prep_w, rope=None if st is CTX else (cos_t, sin_t),
                            sink=attn_sink[l] if st is CTX else None)
            za, zb, sg, r, v, nkk, w, b, k, bonus, g = res[:11]
            if st is CTX:
                o_ab = res[11]
                attn = (o_ab, 0, o_ab, 1)
                zac = za.reshape(BATCH, SEQ, A_COLS)
                zbc = zb.reshape(BATCH, SEQ, B_COLS)
                new_ka.append(zac[..., A_Q:A_Q + A_KV].reshape(BATCH, SEQ, A_KV_HEADS, HEAD_DIM))
                new_va.append(zac[..., A_Q + A_KV:].reshape(BATCH, SEQ, A_KV_HEADS, HEAD_DIM))
                new_kb.append(zbc[..., B_W:2 * B_W].reshape(BATCH, SEQ, B_HEADS, HEAD_DIM))
                new_vb.append(zbc[..., 2 * B_W:].reshape(BATCH, SEQ, B_HEADS, HEAD_DIM))
            else:
                oa = _win_attn(za, cache_attn_k[:, l].reshape(DEC_BATCH, PAST_LEN, A_KV),
                               cache_attn_v[:, l].reshape(DEC_BATCH, PAST_LEN, A_KV), attn_sink[l])
                ob = _na_attn(zb, cache_na_k[:, l].reshape(DEC_BATCH, PAST_LEN, B_W),
                              cache_na_v[:, l].reshape(DEC_BATCH, PAST_LEN, B_W), na_rpb[l].reshape(-1))
                attn = (oa, 0, ob, 0)
            yf, yb, s_t = _scan(st, w, b, k, nkk, r, v, zero_state if st is CTX else s0_lat)
            if st is CTX:
                s_t = s_t.reshape(ctx_groups, HEAD_DIM, HEAD_DIM, C_HEADS, 2, SCAN_NB)
                new_st.append(s_t.transpose(0, 5, 4, 3, 1, 2).reshape(BATCH, 2, C_HEADS, HEAD_DIM, HEAD_DIM))
            xs[st] = _merge_ffn(st, l, x, mods, *attn, yf, yb, bonus, g, sg, *merge_w, *ffn2[:4])

    y_p = xs[CTX].reshape(BATCH, SEQ, D_MODEL)
    y_s = xs[LAT].reshape(DEC_BATCH, DEC_SEQ, D_MODEL)
    return (y_p, y_s, jnp.stack(new_ka, axis=1), jnp.stack(new_va, axis=1), jnp.stack(new_kb, axis=1),
            jnp.stack(new_vb, axis=1), jnp.stack(new_st, axis=1))
```
